```python
import jax
import jax.numpy as jnp
from jax import lax
import numpy as np

D_MODEL = 1024
BATCH = 2
SEQ = 8192
DEPTH = 2
DEC_BATCH = 128
DEC_SEQ = 1
PAST_LEN = 2048
PAGE_SIZE = 128

H_A = D_MODEL // 128
DH_A = 64
MOBA_BLOCK = 256
MOBA_TOPK = 3
Q_BLOCK = 128
H_B = D_MODEL // 256
DK_B = 128
DV_B = 128
HGRN_CHUNK = 32
H_C = D_MODEL // 128
DH_C = 64
C_PATTERNS = ((128, 1), (512, 4), (2048, 16))
BAND_BLOCK = 128
D_FF = ((8 * D_MODEL // 3 + 127) // 128) * 128
CONV_W = 3

RMS_EPS = 1e-6
NEG_INF = -1e30
N_AB_LAYERS = (DEPTH + 1) // 2
N_C_LAYERS = DEPTH // 2
W_A = H_A * DH_A
W_B = H_B * DV_B
IN_AB = 3 * W_A + 2 * H_B * DK_B + 2 * H_B * DV_B
IN_C = 3 * len(C_PATTERNS) * H_C * DH_C

kernel_name = "moba_hgrn2_dilated_convffn_step"


def rms_norm(x, g):
    xf = x.astype(jnp.float32)
    y = xf * lax.rsqrt(jnp.mean(xf * xf, axis=-1, keepdims=True) + RMS_EPS)
    return (y * g.astype(jnp.float32)).astype(x.dtype)


def moba_blocks(k_all, v_all):
    B, L, H, Dh = k_all.shape
    nb = -(-L // MOBA_BLOCK)
    pad = nb * MOBA_BLOCK - L

    def to_blocks(t):
        t = jnp.pad(t, ((0, 0), (0, pad), (0, 0), (0, 0)))
        return t.reshape(B, nb, MOBA_BLOCK, H, Dh).transpose(0, 3, 1, 2, 4)

    kb, vb = to_blocks(k_all), to_blocks(v_all)
    means = jnp.mean(kb.astype(jnp.float32), axis=3)
    return kb, vb, means


def moba_attend(q, q_pos, kb, vb, means):
    B, Q, H, Dh = q.shape
    nb = kb.shape[2]
    n_sel = min(MOBA_TOPK, nb)
    own = q_pos // MOBA_BLOCK
    gate = jnp.einsum('bqhd,bhnd->bqhn', q.astype(jnp.float32), means)
    fully_past = jnp.arange(nb)[None, :] < own[:, None]
    gate = jnp.where(fully_past[None, :, None, :], gate, NEG_INF)
    _, sel = lax.top_k(gate, n_sel)
    sel_ok = sel < own[None, :, None, None]
    own_b = jnp.broadcast_to(own[None, :, None, None], (B, Q, H, 1)).astype(sel.dtype)
    blocks = jnp.concatenate([sel, own_b], axis=-1)
    ok = jnp.concatenate([sel_ok, jnp.ones(own_b.shape, bool)], axis=-1)
    bi = jnp.arange(B)[:, None, None, None]
    hi = jnp.arange(H)[None, None, :, None]
    kg = kb[bi, hi, blocks]
    vg = vb[bi, hi, blocks]
    s = jnp.einsum('bqhd,bqhjkd->bqhjk', q, kg).astype(jnp.float32) * (Dh ** -0.5)
    key_pos = blocks[..., None] * MOBA_BLOCK + jnp.arange(MOBA_BLOCK)
    mask = ok[..., None] & (key_pos <= q_pos[None, :, None, None, None])
    s = jnp.where(mask, s, NEG_INF)
    J = blocks.shape[-1]
    p = jax.nn.softmax(s.reshape(B, Q, H, J * MOBA_BLOCK), axis=-1)
    p = p.reshape(B, Q, H, J, MOBA_BLOCK).astype(vg.dtype)
    return jnp.einsum('bqhjk,bqhjkd->bqhd', p, vg)


def moba_attention(q, k_all, v_all, pos0, blocked):
    B, T, H, Dh = q.shape
    kb, vb, means = moba_blocks(k_all, v_all)
    q_pos = pos0 + jnp.arange(T, dtype=jnp.int32)
    if not blocked:
        return moba_attend(q, q_pos, kb, vb, means)
    nq = T // Q_BLOCK
    qc = q.reshape(B, nq, Q_BLOCK, H, Dh).transpose(1, 0, 2, 3, 4)
    pc = q_pos.reshape(nq, Q_BLOCK)
    oc = lax.map(lambda a: moba_attend(a[0], a[1], kb, vb, means), (qc, pc))
    return oc.transpose(1, 0, 2, 3, 4).reshape(B, T, H, Dh)


def gated_recurrence(q, k, v, logf, S0):
    B, T, H, K = q.shape
    V = v.shape[-1]
    C = HGRN_CHUNK
    n = -(-T // C)
    pad = n * C - T

    def chunks(t):
        t = jnp.pad(t.astype(jnp.float32), ((0, 0), (0, pad), (0, 0), (0, 0)))
        return t.reshape(B, n, C, H, t.shape[-1]).transpose(0, 3, 1, 2, 4)

    q, k, v, g = chunks(q), chunks(k), chunks(v), chunks(logf)
    b = jnp.cumsum(g, axis=3)
    b_ref = b[:, :, :, C // 2 - 1:C // 2]
    b_last = b[:, :, :, C - 1:]
    causal = jnp.tril(jnp.ones((C, C), bool))
    a = jnp.einsum('bhnck,bhnsk->bhncs', q * jnp.exp(b - b_ref), k * jnp.exp(b_ref - b))
    a = jnp.where(causal, a, 0.0)
    o_intra = jnp.einsum('bhncs,bhnsv->bhncv', a, v)
    dS = jnp.einsum('bhnck,bhncv->bhnkv', k * jnp.exp(b_last - b), v)
    decay = jnp.exp(b_last[:, :, :, 0])

    def step(S, inp):
        d, ds = inp
        return d[..., None] * S + ds, S

    S_fin, S_in = lax.scan(step, S0.astype(jnp.float32),
                           (jnp.moveaxis(decay, 2, 0), jnp.moveaxis(dS, 2, 0)))
    o_inter = jnp.einsum('bhnck,nbhkv->bhncv', q * jnp.exp(b), S_in)
    o = (o_intra + o_inter).transpose(0, 2, 3, 1, 4).reshape(B, n * C, H, V)[:, :T]
    return o, S_fin


def hgrn2_mixer(q, f_logit, i, g_out, lb, gn, S0):
    f = lb + (1.0 - lb) * jax.nn.sigmoid(f_logit.astype(jnp.float32))
    o, S = gated_recurrence(jax.nn.silu(q.astype(jnp.float32)), 1.0 - f, i, jnp.log(f), S0)
    o = rms_norm(o, gn) * jax.nn.silu(g_out.astype(jnp.float32))
    return o.astype(q.dtype), S


def layer_ab(h, pos0, w_in, qn, kn, lb, gn, w_out, k_past, v_past, S0, blocked):
    B, T, _ = h.shape
    sizes = [W_A, W_A, W_A, H_B * DK_B, H_B * DK_B, H_B * DV_B, H_B * DV_B]
    qa, ka, va, qb, fb, ib, gb = jnp.split(h @ w_in, np.cumsum(sizes)[:-1].tolist(), axis=-1)
    qa = rms_norm(qa.reshape(B, T, H_A, DH_A), qn)
    ka = rms_norm(ka.reshape(B, T, H_A, DH_A), kn)
    va = va.reshape(B, T, H_A, DH_A)
    if k_past is None:
        k_all, v_all = ka, va
    else:
        k_all = jnp.concatenate([k_past.astype(ka.dtype), ka], axis=1)
        v_all = jnp.concatenate([v_past.astype(va.dtype), va], axis=1)
    oa = moba_attention(qa, k_all, v_all, pos0, blocked)
    ob, S = hgrn2_mixer(qb.reshape(B, T, H_B, DK_B), fb.reshape(B, T, H_B, DK_B),
                        ib.reshape(B, T, H_B, DV_B), gb.reshape(B, T, H_B, DV_B), lb, gn, S0)
    o = jnp.concatenate([oa.reshape(B, T, W_A), ob.reshape(B, T, W_B)], axis=-1) @ w_out
    return o, ka, va, S


def band_attention(q, k, v, n_back):
    N, L, H, Dh = q.shape
    P = BAND_BLOCK
    assert n_back <= P
    nb = -(-L // P)
    pad = nb * P - L

    def blocks(t):
        return jnp.pad(t, ((0, 0), (0, pad), (0, 0), (0, 0))).reshape(N, nb, P, H, Dh)

    def with_prev(t):
        prev = jnp.pad(t, ((0, 0), (1, 0), (0, 0), (0, 0), (0, 0)))[:, :-1]
        return jnp.concatenate([prev, t], axis=2)

    qb = blocks(q)
    kk, vv = with_prev(blocks(k)), with_prev(blocks(v))
    s = jnp.einsum('nbqhd,nbkhd->nbhqk', qb, kk).astype(jnp.float32) * (Dh ** -0.5)
    ki = jnp.arange(2 * P)[None, :]
    rel = (jnp.arange(P)[:, None] + P) - ki
    key_idx = jnp.arange(nb)[:, None, None] * P - P + ki[None]
    mask = (rel >= 0) & (rel <= n_back) & (key_idx >= 0)
    s = jnp.where(mask[None, :, None], s, NEG_INF)
    m = jnp.max(s, axis=-1, keepdims=True)
    p = jnp.exp(s - m)
    l = jnp.sum(p, axis=-1, keepdims=True)
    o = jnp.einsum('nbhqk,nbkhd->nbqhd', (p / l).astype(v.dtype), vv)
    lse = (m + jnp.log(l))[..., 0]
    o = o.reshape(N, nb * P, H, Dh)[:, :L]
    lse = lse.transpose(0, 1, 3, 2).reshape(N, nb * P, H)[:, :L]
    return o, lse


def dilated_prompt(q, k, v, window, dil):
    B, S, H, Dh = q.shape
    L = -(-S // dil)
    pad = L * dil - S

    def split(t):
        t = jnp.pad(t, ((0, 0), (0, pad), (0, 0), (0, 0)))
        return t.reshape(B, L, dil, H, Dh).transpose(0, 2, 1, 3, 4).reshape(B * dil, L, H, Dh)

    o, lse = band_attention(split(q), split(k), split(v), window // dil)
    o = o.reshape(B, dil, L, H, Dh).transpose(0, 2, 1, 3, 4).reshape(B, L * dil, H, Dh)[:, :S]
    lse = lse.reshape(B, dil, L, H).transpose(0, 2, 1, 3).reshape(B, L * dil, H)[:, :S]
    return o, lse


def dilated_step(q, k, v, k_buf, v_buf, window, dil):
    kk = jnp.concatenate([k_buf.astype(k.dtype), k], axis=1)
    vv = jnp.concatenate([v_buf.astype(v.dtype), v], axis=1)
    Lw = k_buf.shape[1]
    T = q.shape[1]
    n_keys = window // dil + 1
    idx = (Lw + jnp.arange(T))[:, None] - dil * jnp.arange(n_keys)[None, :]
    ok = idx >= 0
    kg = jnp.take(kk, jnp.maximum(idx, 0), axis=1)
    vg = jnp.take(vv, jnp.maximum(idx, 0), axis=1)
    s = jnp.einsum('bthd,btjhd->bthj', q, kg).astype(jnp.float32) * (q.shape[-1] ** -0.5)
    s = jnp.where(ok[None, :, None, :], s, NEG_INF)
    m = jnp.max(s, axis=-1, keepdims=True)
    p = jnp.exp(s - m)
    l = jnp.sum(p, axis=-1, keepdims=True)
    o = jnp.einsum('bthj,btjhd->bthd', (p / l).astype(vg.dtype), vg)
    lse = (m + jnp.log(l))[..., 0]
    return o, lse, kk[:, -Lw:], vv[:, -Lw:]


def layer_c(h, w_in, qn, kn, w_out, bufs):
    B, T, _ = h.shape
    G = len(C_PATTERNS)
    proj = (h @ w_in).reshape(B, T, 3, G, H_C, DH_C)
    q = rms_norm(proj[:, :, 0], qn)
    k = rms_norm(proj[:, :, 1], kn)
    v = proj[:, :, 2]
    outs, lses, new_bufs = [], [], []
    for g, (win, dil) in enumerate(C_PATTERNS):
        qg, kg, vg = q[:, :, g], k[:, :, g], v[:, :, g]
        if bufs is None:
            o, lse = dilated_prompt(qg, kg, vg, win, dil)
            keep = min(win, T)
            new_bufs.append((kg[:, T - keep:], vg[:, T - keep:]))
        else:
            o, lse, nk, nv = dilated_step(qg, kg, vg, bufs[g][0], bufs[g][1], win, dil)
            new_bufs.append((nk, nv))
        outs.append(o)
        lses.append(lse)
    w = jax.nn.softmax(jnp.stack(lses), axis=0)[..., None]
    o = jnp.sum(w * jnp.stack(outs).astype(jnp.float32), axis=0).astype(h.dtype)
    return o.reshape(B, T, H_C * DH_C) @ w_out, new_bufs


def conv_ffn(h, w_gate, w_up, conv_w, conv_b, w_down, conv_buf):
    B, T, _ = h.shape
    a = h @ w_gate
    if conv_buf is None:
        conv_buf = jnp.zeros((B, CONV_W - 1, a.shape[-1]), a.dtype)
    a_ext = jnp.concatenate([conv_buf.astype(a.dtype), a], axis=1)
    c = conv_b + a_ext[:, 0:T] * conv_w[0]
    for j in range(1, CONV_W):
        c = c + a_ext[:, j:j + T] * conv_w[j]
    y = (jax.nn.silu(c) * (h @ w_up)) @ w_down
    return y, a_ext[:, T:]


def setup_inputs(seed: int = 0) -> dict:
    key = jax.random.key(seed)
    keys = iter(jax.random.split(key, 32))
    n_pages = PAST_LEN // PAGE_SIZE
    n_used = DEC_BATCH * n_pages
    n_pool = n_used + n_used // 4

    def normal(shape, scale):
        return jax.random.normal(next(keys), shape, jnp.float32) * scale

    def gain(shape):
        return 1.0 + 0.05 * jax.random.normal(next(keys), shape, jnp.float32)

    wins = [min(w, PAST_LEN) for w, _ in C_PATTERNS]
    return {
        "x_prompt": normal((BATCH, SEQ, D_MODEL), 1.0),
        "x_sample": normal((DEC_BATCH, DEC_SEQ, D_MODEL), 1.0),
        "cache_k_a": normal((N_AB_LAYERS, n_pool, PAGE_SIZE, H_A, DH_A), 1.0),
        "cache_v_a": normal((N_AB_LAYERS, n_pool, PAGE_SIZE, H_A, DH_A), 1.0),
        "state_hgrn": normal((N_AB_LAYERS, DEC_BATCH, H_B, DK_B, DV_B), 0.5),
        "cache_k_c0": normal((N_C_LAYERS, DEC_BATCH, wins[0], H_C, DH_C), 1.0),
        "cache_v_c0": normal((N_C_LAYERS, DEC_BATCH, wins[0], H_C, DH_C), 1.0),
        "cache_k_c1": normal((N_C_LAYERS, DEC_BATCH, wins[1], H_C, DH_C), 1.0),
        "cache_v_c1": normal((N_C_LAYERS, DEC_BATCH, wins[1], H_C, DH_C), 1.0),
        "cache_k_c2": normal((N_C_LAYERS, DEC_BATCH, wins[2], H_C, DH_C), 1.0),
        "cache_v_c2": normal((N_C_LAYERS, DEC_BATCH, wins[2], H_C, DH_C), 1.0),
        "state_conv": normal((DEPTH, DEC_BATCH, CONV_W - 1, D_FF), 1.0),
        "page_table": jax.random.permutation(next(keys), n_pool)[:n_used].reshape(DEC_BATCH, n_pages).astype(jnp.int32),
        "norm_mix": gain((DEPTH, D_MODEL)),
        "norm_ffn": gain((DEPTH, D_MODEL)),
        "w_in_ab": normal((N_AB_LAYERS, D_MODEL, IN_AB), D_MODEL ** -0.5),
        "q_norm_a": gain((N_AB_LAYERS, DH_A)),
        "k_norm_a": gain((N_AB_LAYERS, DH_A)),
        "lb_logits": normal((N_AB_LAYERS + 1, H_B * DK_B), 0.1),
        "g_norm_b": gain((N_AB_LAYERS, DV_B)),
        "w_out_ab": normal((N_AB_LAYERS, W_A + W_B, D_MODEL), (W_A + W_B) ** -0.5),
        "w_in_c": normal((N_C_LAYERS, D_MODEL, IN_C), D_MODEL ** -0.5),
        "q_norm_c": gain((N_C_LAYERS, DH_C)),
        "k_norm_c": gain((N_C_LAYERS, DH_C)),
        "w_out_c": normal((N_C_LAYERS, H_C * DH_C, D_MODEL), (H_C * DH_C) ** -0.5),
        "w_gate": normal((DEPTH, D_MODEL, D_FF), D_MODEL ** -0.5),
        "w_up": normal((DEPTH, D_MODEL, D_FF), D_MODEL ** -0.5),
        "conv_w": normal((DEPTH, CONV_W, D_FF), CONV_W ** -0.5),
        "conv_b": normal((DEPTH, D_FF), 0.01),
        "w_down": normal((DEPTH, D_FF, D_MODEL), D_FF ** -0.5),
    }


def reference(x_prompt, x_sample, cache_k_a, cache_v_a, state_hgrn,
              cache_k_c0, cache_v_c0, cache_k_c1, cache_v_c1, cache_k_c2, cache_v_c2,
              state_conv, page_table, norm_mix, norm_ffn, w_in_ab, q_norm_a, k_norm_a,
              lb_logits, g_norm_b, w_out_ab, w_in_c, q_norm_c, k_norm_c, w_out_c,
              w_gate, w_up, conv_w, conv_b, w_down):
    dec_b, n_pages = page_table.shape
    past_len = n_pages * cache_k_a.shape[2]
    lb_all = jnp.cumsum(jax.nn.softmax(lb_logits.astype(jnp.float32), axis=0), axis=0)
    c_caches = ((cache_k_c0, cache_v_c0), (cache_k_c1, cache_v_c1), (cache_k_c2, cache_v_c2))
    xp, xs = x_prompt, x_sample
    a_kp, a_vp, a_ks, a_vs, hg_p, hg_s = [], [], [], [], [], []
    c_kp = [[] for _ in C_PATTERNS]
    c_vp = [[] for _ in C_PATTERNS]
    c_ks = [[] for _ in C_PATTERNS]
    c_vs = [[] for _ in C_PATTERNS]
    conv_p, conv_s = [], []
    for l in range(DEPTH):
        j = l // 2
        hp = rms_norm(xp, norm_mix[l])
        hs = rms_norm(xs, norm_mix[l])
        if l % 2 == 0:
            lb = lb_all[j].reshape(H_B, DK_B)
            w = (w_in_ab[j], q_norm_a[j], k_norm_a[j], lb, g_norm_b[j], w_out_ab[j])
            s0 = jnp.zeros((xp.shape[0], H_B, DK_B, DV_B), jnp.float32)
            op, kp, vp, sp = layer_ab(hp, 0, *w, None, None, s0, True)
            k_past = cache_k_a[j][page_table].reshape(dec_b, past_len, H_A, DH_A)
            v_past = cache_v_a[j][page_table].reshape(dec_b, past_len, H_A, DH_A)
            os_, ks, vs, ss = layer_ab(hs, past_len, *w, k_past, v_past, state_hgrn[j], False)
            a_kp.append(kp)
            a_vp.append(vp)
            a_ks.append(ks)
            a_vs.append(vs)
            hg_p.append(sp)
            hg_s.append(ss)
        else:
            w = (w_in_c[j], q_norm_c[j], k_norm_c[j], w_out_c[j])
            op, bufs_p = layer_c(hp, *w, None)
            os_, bufs_s = layer_c(hs, *w, [(ck[j], cv[j]) for ck, cv in c_caches])
            for g in range(len(C_PATTERNS)):
                c_kp[g].append(bufs_p[g][0])
                c_vp[g].append(bufs_p[g][1])
                c_ks[g].append(bufs_s[g][0])
                c_vs[g].append(bufs_s[g][1])
        xp = xp + op
        xs = xs + os_
        fp, cp = conv_ffn(rms_norm(xp, norm_ffn[l]), w_gate[l], w_up[l], conv_w[l], conv_b[l], w_down[l], None)
        fs, cs = conv_ffn(rms_norm(xs, norm_ffn[l]), w_gate[l], w_up[l], conv_w[l], conv_b[l], w_down[l], state_conv[l])
        xp = xp + fp
        xs = xs + fs
        conv_p.append(cp)
        conv_s.append(cs)
    return (xp, xs,
            jnp.stack(a_kp), jnp.stack(a_vp), jnp.stack(a_ks), jnp.stack(a_vs),
            jnp.stack(hg_p), jnp.stack(hg_s),
            jnp.stack(c_kp[0]), jnp.stack(c_vp[0]), jnp.stack(c_ks[0]), jnp.stack(c_vs[0]),
            jnp.stack(c_kp[1]), jnp.stack(c_vp[1]), jnp.stack(c_ks[1]), jnp.stack(c_vs[1]),
            jnp.stack(c_kp[2]), jnp.stack(c_vp[2]), jnp.stack(c_ks[2]), jnp.stack(c_vs[2]),
            jnp.stack(conv_p), jnp.stack(conv_s))
```

```python
import functools

import jax
import jax.numpy as jnp
from jax import lax
from jax.experimental import pallas as pl
from jax.experimental.pallas import tpu as pltpu

F32 = jnp.float32
BF16 = jnp.bfloat16
HIGHEST = lax.Precision.HIGHEST

RMS_EPS = 1e-6
NEG_INF = -1e30
LOWEST = -3e38

LANES = 128
DH = 64
HEADS = 8
W_ATT = HEADS * DH
SCALE = DH ** -0.5
MOBA_BLOCK = 256
MOBA_TOPK = 3
H_B = 4
D_B = 128
HGRN_CHUNK = 32
BAND = 128
C_PATTERNS = ((128, 1), (512, 4), (2048, 16))
VMEM_LIMIT = 48 * 1024 * 1024

_CONTRACT_LAST = (((1,), (1,)), ((), ()))


def _params(*sem):
    return pltpu.CompilerParams(dimension_semantics=sem, vmem_limit_bytes=VMEM_LIMIT)


def _sigmoid(x):
    return 1.0 / (1.0 + jnp.exp(-x))


def _silu(x):
    return x * _sigmoid(x)


def _rms(x, g):
    return x * lax.rsqrt(jnp.mean(x * x, axis=-1, keepdims=True) + RMS_EPS) * g


def _dot(a, b):
    return jnp.dot(a, b, preferred_element_type=F32)


def _dot_t(a, b, precision=None):
    return lax.dot_general(a, b, _CONTRACT_LAST, precision=precision, preferred_element_type=F32)


def _norm_proj_kernel(x_ref, g_ref, w_ref, hg_ref, p_ref, o_ref, h_scr, *, n_norm):
    j = pl.program_id(1)

    @pl.when(j == 0)
    def _():
        h_scr[...] = _rms(x_ref[...], g_ref[...]).astype(BF16)

    y = _dot(h_scr[...], w_ref[...])

    @pl.when(j < n_norm)
    def _():
        s = y * y
        hi = s.astype(BF16)
        lo = (s - hi.astype(F32)).astype(BF16)
        ms = _dot(hi, p_ref[...]) + _dot(lo, p_ref[...])
        o_ref[...] = y * lax.rsqrt(ms + RMS_EPS) * hg_ref[0]

    @pl.when(j >= n_norm)
    def _():
        o_ref[...] = y


def _norm_proj(x, g, w_bf, head_gains, n_norm, tm):
    n, d = x.shape
    m = w_bf.shape[1]
    nj = m // W_ATT
    blockdiag = (jnp.arange(W_ATT)[:, None] // DH == jnp.arange(W_ATT)[None, :] // DH)
    p = (blockdiag.astype(F32) / DH).astype(BF16)
    return pl.pallas_call(
        functools.partial(_norm_proj_kernel, n_norm=n_norm),
        grid=(n // tm, nj),
        in_specs=[
            pl.BlockSpec((tm, d), lambda i, j: (i, 0)),
            pl.BlockSpec((1, d), lambda i, j: (0, 0)),
            pl.BlockSpec((d, W_ATT), lambda i, j: (0, j)),
            pl.BlockSpec((1, 1, W_ATT), lambda i, j: (j, 0, 0)),
            pl.BlockSpec((W_ATT, W_ATT), lambda i, j: (0, 0)),
        ],
        out_specs=pl.BlockSpec((tm, W_ATT), lambda i, j: (i, j)),
        out_shape=jax.ShapeDtypeStruct((n, m), F32),
        scratch_shapes=[pltpu.VMEM((tm, d), BF16)],
        compiler_params=_params("parallel", "arbitrary"),
    )(x, g.reshape(1, d), w_bf, head_gains, p)


def _head_gains(gains, n_chunks):
    rows = [jnp.tile(gv.astype(F32), HEADS) for gv in gains]
    rows += [jnp.ones((W_ATT,), F32)] * (n_chunks - len(rows))
    return jnp.stack(rows).reshape(n_chunks, 1, W_ATT)


def _ffn_seq_kernel(x_ref, g_ref, wg_ref, wu_ref, cw_ref, cb_ref, wd_ref, o_ref, cs_ref,
                    h_scr, a_scr, acc_scr, *, tm, blocks_per_seq):
    i = pl.program_id(0)
    c = pl.program_id(1)

    @pl.when(c == 0)
    def _():
        h_scr[...] = _rms(x_ref[...], g_ref[...]).astype(BF16)

    @pl.when(i % blocks_per_seq == 0)
    def _():
        a_scr[c, 0:8, :] = jnp.zeros((8, a_scr.shape[2]), F32)

    h = h_scr[...]
    a = _dot(h, wg_ref[...])
    u = _dot(h, wu_ref[...])
    a_scr[c, 8:8 + tm, :] = a
    a1 = a_scr[c, 7:7 + tm, :]
    a2 = a_scr[c, 6:6 + tm, :]
    cw = cw_ref[...]
    conv = cb_ref[...] + a2 * cw[0:1] + a1 * cw[1:2] + a * cw[2:3]
    y = (_silu(conv) * u).astype(BF16)
    part = _dot(y, wd_ref[...])

    @pl.when(c == 0)
    def _():
        acc_scr[...] = part

    @pl.when(c > 0)
    def _():
        acc_scr[...] = acc_scr[...] + part

    a_scr[c, 0:8, :] = a_scr[c, tm:tm + 8, :]
    tc = a_scr.shape[2]
    for cc in range(a_scr.shape[0]):
        @pl.when(c == cc)
        def _(cc=cc):
            cs_ref[0, :, cc * tc:(cc + 1) * tc] = a_scr[cc, tm + 6:tm + 8, :]

    @pl.when(c == pl.num_programs(1) - 1)
    def _():
        o_ref[...] = x_ref[...] + acc_scr[...]


def _ffn_seq(x, g, wg, wu, cw, cb, wd, n_seq, tm, tc):
    n, d = x.shape
    dff = wg.shape[1]
    nc = dff // tc
    bps = n // n_seq // tm
    return pl.pallas_call(
        functools.partial(_ffn_seq_kernel, tm=tm, blocks_per_seq=bps),
        grid=(n // tm, nc),
        in_specs=[
            pl.BlockSpec((tm, d), lambda i, c: (i, 0)),
            pl.BlockSpec((1, d), lambda i, c: (0, 0)),
            pl.BlockSpec((d, tc), lambda i, c: (0, c)),
            pl.BlockSpec((d, tc), lambda i, c: (0, c)),
            pl.BlockSpec((3, tc), lambda i, c: (0, c)),
            pl.BlockSpec((1, tc), lambda i, c: (0, c)),
            pl.BlockSpec((tc, d), lambda i, c: (c, 0)),
        ],
        out_specs=[
            pl.BlockSpec((tm, d), lambda i, c: (i, 0)),
            pl.BlockSpec((1, 2, dff), lambda i, c: (i // bps, 0, 0)),
        ],
        out_shape=[jax.ShapeDtypeStruct((n, d), F32),
                   jax.ShapeDtypeStruct((n_seq, 2, dff), F32)],
        scratch_shapes=[pltpu.VMEM((tm, d), BF16),
                        pltpu.VMEM((nc, tm + 8, tc), F32),
                        pltpu.VMEM((tm, d), F32)],
        compiler_params=_params("arbitrary", "arbitrary"),
    )(x, g.reshape(1, d), wg, wu, cw, cb.reshape(1, dff), wd)


def _ffn_step_kernel(x_ref, g_ref, wg_ref, wu_ref, cw_ref, cb_ref, wd_ref, s0_ref, s1_ref,
                     o_ref, a_ref, h_scr, acc_scr):
    c = pl.program_id(0)

    @pl.when(c == 0)
    def _():
        h_scr[...] = _rms(x_ref[...], g_ref[...]).astype(BF16)

    h = h_scr[...]
    a = _dot(h, wg_ref[...])
    u = _dot(h, wu_ref[...])
    a_ref[...] = a
    cw = cw_ref[...]
    conv = cb_ref[...] + s0_ref[...] * cw[0:1] + s1_ref[...] * cw[1:2] + a * cw[2:3]
    y = (_silu(conv) * u).astype(BF16)
    part = _dot(y, wd_ref[...])

    @pl.when(c == 0)
    def _():
        acc_scr[...] = part

    @pl.when(c > 0)
    def _():
        acc_scr[...] = acc_scr[...] + part

    @pl.when(c == pl.num_programs(0) - 1)
    def _():
        o_ref[...] = x_ref[...] + acc_scr[...]


def _ffn_step(x, g, wg, wu, cw, cb, wd, s0, s1, tc):
    n, d = x.shape
    dff = wg.shape[1]
    return pl.pallas_call(
        _ffn_step_kernel,
        grid=(dff // tc,),
        in_specs=[
            pl.BlockSpec((n, d), lambda c: (0, 0)),
            pl.BlockSpec((1, d), lambda c: (0, 0)),
            pl.BlockSpec((d, tc), lambda c: (0, c)),
            pl.BlockSpec((d, tc), lambda c: (0, c)),
            pl.BlockSpec((3, tc), lambda c: (0, c)),
            pl.BlockSpec((1, tc), lambda c: (0, c)),
            pl.BlockSpec((tc, d), lambda c: (c, 0)),
            pl.BlockSpec((n, tc), lambda c: (0, c)),
            pl.BlockSpec((n, tc), lambda c: (0, c)),
        ],
        out_specs=[pl.BlockSpec((n, d), lambda c: (0, 0)),
                   pl.BlockSpec((n, tc), lambda c: (0, c))],
        out_shape=[jax.ShapeDtypeStruct((n, d), F32), jax.ShapeDtypeStruct((n, dff), F32)],
        scratch_shapes=[pltpu.VMEM((n, d), BF16), pltpu.VMEM((n, d), F32)],
        compiler_params=_params("arbitrary"),
    )(x, g.reshape(1, d), wg, wu, cw, cb.reshape(1, dff), wd, s0, s1)


def _out_ab_kernel(oa_ref, ob_ref, w_ref, x_ref, o_ref):
    ka = oa_ref.shape[1]
    o_ref[...] = (x_ref[...] + _dot(oa_ref[...].astype(BF16), w_ref[0:ka, :])
                  + _dot(ob_ref[...].astype(BF16), w_ref[ka:, :]))


def _out_ab(oa, ob, w_bf, x, tm):
    n, d = x.shape
    ka, kb = oa.shape[1], ob.shape[1]
    return pl.pallas_call(
        _out_ab_kernel,
        grid=(n // tm,),
        in_specs=[pl.BlockSpec((tm, ka), lambda i: (i, 0)),
                  pl.BlockSpec((tm, kb), lambda i: (i, 0)),
                  pl.BlockSpec((ka + kb, d), lambda i: (0, 0)),
                  pl.BlockSpec((tm, d), lambda i: (i, 0))],
        out_specs=pl.BlockSpec((tm, d), lambda i: (i, 0)),
        out_shape=jax.ShapeDtypeStruct((n, d), F32),
        compiler_params=_params("parallel"),
    )(oa, ob, w_bf, x)


def _merge_groups(os, ls):
    m = jnp.maximum(jnp.maximum(ls[0], ls[1]), ls[2])
    es = [jnp.exp(l - m) for l in ls]
    den = es[0] + es[1] + es[2]
    return (es[0] / den) * os[0] + (es[1] / den) * os[1] + (es[2] / den) * os[2]


def _out_c_kernel(o0, o1, o2, l0, l1, l2, w_ref, x_ref, o_ref):
    merged = _merge_groups([o0[...], o1[...], o2[...]], [l0[...], l1[...], l2[...]])
    o_ref[...] = x_ref[...] + _dot(merged.astype(BF16), w_ref[...])


def _out_c(os, ls, w_bf, x, tm):
    n, d = x.shape
    k = w_bf.shape[0]
    act = pl.BlockSpec((tm, k), lambda i: (i, 0))
    return pl.pallas_call(
        _out_c_kernel,
        grid=(n // tm,),
        in_specs=[act] * 6 + [pl.BlockSpec((k, d), lambda i: (0, 0)),
                              pl.BlockSpec((tm, d), lambda i: (i, 0))],
        out_specs=pl.BlockSpec((tm, d), lambda i: (i, 0)),
        out_shape=jax.ShapeDtypeStruct((n, d), F32),
        compiler_params=_params("parallel"),
    )(*os, *ls, w_bf, x)


def _out_plain_kernel(a_ref, w_ref, x_ref, o_ref):
    o_ref[...] = x_ref[...] + _dot(a_ref[...].astype(BF16), w_ref[...])


def _out_plain(a, w_bf, x):
    n, d = x.shape
    return pl.pallas_call(
        _out_plain_kernel,
        out_shape=jax.ShapeDtypeStruct((n, d), F32),
        compiler_params=pltpu.CompilerParams(vmem_limit_bytes=VMEM_LIMIT),
    )(a, w_bf, x)


def _top_k_mask(g, lane, k):
    sel = jnp.zeros(g.shape, F32)
    lane_f = lane.astype(F32)
    for _ in range(k):
        m = jnp.max(g, axis=1, keepdims=True)
        idx = jnp.min(jnp.where(g == m, lane_f, float(g.shape[1])), axis=1, keepdims=True)
        hit = lane_f == idx
        sel = jnp.where(hit, 1.0, sel)
        g = jnp.where(hit, LOWEST, g)
    return sel


def _moba_prep_kernel(q_ref, k_ref, v_ref, qa_ref, ka_ref, va_ref, kmean_scr):
    i = pl.program_id(1)

    @pl.when(i == 0)
    def _():
        kmean_scr[...] = jnp.zeros(kmean_scr.shape, F32)

    rows = q_ref.shape[1]
    lane = lax.broadcasted_iota(jnp.int32, (rows, LANES), 1)
    low = lane < DH
    for hp in range(HEADS // 2):
        cols = slice(hp * LANES, (hp + 1) * LANES)
        q2 = q_ref[0, :, cols]
        k2 = k_ref[0, :, cols]
        v2 = v_ref[0, :, cols]
        km2 = kmean_scr[:, cols]
        for e in range(2):
            mine = low if e == 0 else jnp.logical_not(low)
            gate = _dot_t(jnp.where(mine, q2, 0.0), km2, precision=HIGHEST)
            past = lane < i
            sel = _top_k_mask(jnp.where(past, gate, NEG_INF), lane, MOBA_TOPK)
            keep = jnp.where(past, sel, 0.0) + jnp.where(lane == i, 1.0, 0.0)
            bias = jnp.where(keep > 0.5, 0.0, NEG_INF)
            onehot = jnp.where(lane == i, 1.0, 0.0)
            if e == 0:
                bias = pltpu.roll(bias, DH, axis=1)
                onehot = jnp.where(lane == i + DH, 1.0, 0.0)
                ones_col = jnp.where(lane == DH, 1.0, 0.0)
            else:
                ones_col = jnp.where(lane == 0, 1.0, 0.0)
            h = 2 * hp + e
            qa_ref[0, h] = jnp.where(mine, q2 * SCALE, bias).astype(BF16)
            ka_ref[0, h] = jnp.where(mine, k2, onehot).astype(BF16)
            va_ref[0, h] = jnp.where(mine, v2, ones_col).astype(BF16)

    kmean_scr[pl.ds(i, 1), :] = jnp.mean(k_ref[0], axis=0, keepdims=True)


def _moba_prep(proj, n_blocks_max=LANES):
    b, s, _ = proj.shape
    nb = s // MOBA_BLOCK
    assert nb <= n_blocks_max // 2 and nb <= DH // 2
    blk = lambda j: pl.BlockSpec((1, MOBA_BLOCK, W_ATT), lambda bi, i: (bi, i, j))
    aug = pl.BlockSpec((1, HEADS, MOBA_BLOCK, LANES), lambda bi, i: (bi, 0, i, 0))
    shape = jax.ShapeDtypeStruct((b, HEADS, s, LANES), BF16)
    return pl.pallas_call(
        _moba_prep_kernel,
        grid=(b, nb),
        in_specs=[blk(0), blk(1), blk(2)],
        out_specs=[aug, aug, aug],
        out_shape=[shape, shape, shape],
        scratch_shapes=[pltpu.VMEM((LANES, W_ATT), F32)],
        compiler_params=_params("arbitrary", "arbitrary"),
    )(proj, proj, proj)


def _moba_attn_kernel(q_ref, k_ref, v_ref, o_ref):
    qi = pl.program_id(2)
    tq = q_ref.shape[2]
    row = lax.broadcasted_iota(jnp.int32, (tq, tq), 0)
    col = lax.broadcasted_iota(jnp.int32, (tq, tq), 1)
    lane = lax.broadcasted_iota(jnp.int32, (tq, LANES), 1)
    outs = []
    for e in range(2):
        q = q_ref[0, e]
        start = pl.multiple_of(qi * tq, tq)
        s = _dot_t(q, k_ref[0, e, pl.ds(start, tq), :])
        s = jnp.where(col <= row, s, NEG_INF)
        m = jnp.max(s, axis=1, keepdims=True)
        acc = _dot(jnp.exp(s - m).astype(BF16), v_ref[0, e, pl.ds(start, tq), :])

        def body(n, carry, q=q, e=e):
            m, acc = carry
            st = pl.multiple_of(n * tq, tq)
            s = _dot_t(q, k_ref[0, e, pl.ds(st, tq), :])
            m_new = jnp.maximum(m, jnp.max(s, axis=1, keepdims=True))
            p = jnp.exp(s - m_new).astype(BF16)
            acc = jnp.exp(m - m_new) * acc + _dot(p, v_ref[0, e, pl.ds(st, tq), :])
            return m_new, acc

        m, acc = lax.fori_loop(0, qi, body, (m, acc))
        lcol = DH if e == 0 else 0
        outs.append(acc / acc[:, lcol:lcol + 1])
    o_ref[0] = jnp.where(lane < DH, outs[0], outs[1])


def _moba_attn(qa, ka, va):
    b, h, s, _ = qa.shape
    tq = MOBA_BLOCK
    return pl.pallas_call(
        _moba_attn_kernel,
        grid=(b, h // 2, s // tq),
        in_specs=[pl.BlockSpec((1, 2, tq, LANES), lambda bi, hp, i: (bi, hp, i, 0)),
                  pl.BlockSpec((1, 2, s, LANES), lambda bi, hp, i: (bi, hp, 0, 0)),
                  pl.BlockSpec((1, 2, s, LANES), lambda bi, hp, i: (bi, hp, 0, 0))],
        out_specs=pl.BlockSpec((1, tq, LANES), lambda bi, hp, i: (bi, i, hp)),
        out_shape=jax.ShapeDtypeStruct((b, s, W_ATT), F32),
        compiler_params=_params("parallel", "parallel", "arbitrary"),
    )(qa, ka, va)


def _head_rows(x_row):
    hrow = lax.broadcasted_iota(jnp.int32, (HEADS, W_ATT), 0)
    lane = lax.broadcasted_iota(jnp.int32, (HEADS, W_ATT), 1)
    mask = (lane // DH) == hrow
    return jnp.where(mask, x_row, 0.0), mask


def _moba_dec_kernel(pt_ref, q_ref, kn_ref, vn_ref, e_ref, kp_ref, vp_ref, o_ref,
                     kall, vall, ksum, *, n_pages, page):
    p = pl.program_id(1)
    pages_per_block = MOBA_BLOCK // page

    @pl.when(p == 0)
    def _():
        ksum[...] = jnp.zeros(ksum.shape, F32)

    k = kp_ref[0]
    st = pl.multiple_of(p * page, page)
    kall[pl.ds(st, page), :] = k.astype(BF16)
    vall[pl.ds(st, page), :] = vp_ref[0].astype(BF16)
    blk_row = lax.broadcasted_iota(jnp.int32, ksum.shape, 0)
    ksum[...] = ksum[...] + jnp.where(blk_row == p // pages_per_block,
                                      jnp.sum(k, axis=0, keepdims=True), 0.0)

    @pl.when(p == n_pages - 1)
    def _():
        n_past = n_pages // pages_per_block
        qmat, headmask = _head_rows(q_ref[0])
        gate = _dot_t(qmat, ksum[...] * (1.0 / MOBA_BLOCK), precision=HIGHEST)
        lane = lax.broadcasted_iota(jnp.int32, gate.shape, 1)
        past = lane < n_past
        sel = _top_k_mask(jnp.where(past, gate, NEG_INF), lane, min(MOBA_TOPK, n_past + 1))
        sel = jnp.where(past, sel, 0.0).astype(BF16)
        sel_keys = _dot(sel, e_ref[...])
        qs = qmat * SCALE
        s = _dot_t(qs.astype(BF16), kall[...])
        s = jnp.where(sel_keys > 0.5, s, NEG_INF)
        s_self = jnp.sum(qs * kn_ref[0], axis=1, keepdims=True)
        m = jnp.maximum(jnp.max(s, axis=1, keepdims=True), s_self)
        pk = jnp.exp(s - m)
        p_self = jnp.exp(s_self - m)
        l = jnp.sum(pk, axis=1, keepdims=True) + p_self
        o8 = (_dot(pk.astype(BF16), vall[...]) + p_self * vn_ref[0]) / l
        o_ref[0] = jnp.sum(jnp.where(headmask, o8, 0.0), axis=0, keepdims=True)


def _moba_dec(q, kn, vn, cache_k, cache_v, page_table):
    b = q.shape[0]
    n_pages = page_table.shape[1]
    page = cache_k.shape[1]
    past = n_pages * page
    n_past = past // MOBA_BLOCK
    expand = (jnp.arange(LANES)[:, None] == jnp.arange(past)[None, :] // MOBA_BLOCK).astype(BF16)
    row = pl.BlockSpec((1, 1, W_ATT), lambda bi, p, pt: (bi, 0, 0))
    pg = pl.BlockSpec((1, page, W_ATT), lambda bi, p, pt: (pt[bi * n_pages + p], 0, 0))
    out = pl.pallas_call(
        functools.partial(_moba_dec_kernel, n_pages=n_pages, page=page),
        grid_spec=pltpu.PrefetchScalarGridSpec(
            num_scalar_prefetch=1,
            grid=(b, n_pages),
            in_specs=[row, row, row,
                      pl.BlockSpec((LANES, past), lambda bi, p, pt: (0, 0)), pg, pg],
            out_specs=row,
            scratch_shapes=[pltpu.VMEM((past, W_ATT), BF16), pltpu.VMEM((past, W_ATT), BF16),
                            pltpu.VMEM((LANES, W_ATT), F32)],
        ),
        out_shape=jax.ShapeDtypeStruct((b, 1, W_ATT), F32),
        compiler_params=_params("arbitrary", "arbitrary"),
    )(page_table.reshape(-1), q.reshape(b, 1, W_ATT), kn.reshape(b, 1, W_ATT),
      vn.reshape(b, 1, W_ATT), expand, cache_k, cache_v)
    del n_past
    return out.reshape(b, W_ATT)


def _hgrn_seq_kernel(q_ref, f_ref, i_ref, g_ref, lb_ref, gn_ref, o_ref, s_ref, st_scr, *, tb):
    t = pl.program_id(1)

    @pl.when(t == 0)
    def _():
        st_scr[...] = jnp.zeros(st_scr.shape, F32)

    c = HGRN_CHUNK
    lb = lb_ref[...]
    gn = gn_ref[...]
    r_i = lax.broadcasted_iota(jnp.int32, (c, c), 0)
    c_i = lax.broadcasted_iota(jnp.int32, (c, c), 1)
    tril = c_i <= r_i
    ones_tril = jnp.where(tril, 1.0, 0.0)
    for ch in range(tb // c):
        rows = slice(ch * c, (ch + 1) * c)
        f = lb + (1.0 - lb) * _sigmoid(f_ref[0, rows, :])
        k = 1.0 - f
        q = _silu(q_ref[0, rows, :])
        v = i_ref[0, rows, :]
        b = jnp.dot(ones_tril, jnp.log(f), precision=HIGHEST, preferred_element_type=F32)
        b_mid = b[c // 2 - 1:c // 2, :]
        b_last = b[c - 1:c, :]
        q_mid = (q * jnp.exp(b - b_mid)).astype(BF16)
        k_mid = (k * jnp.exp(b_mid - b)).astype(BF16)
        k_end = (k * jnp.exp(b_last - b)).astype(BF16)
        q_in = (q * jnp.exp(b)).astype(BF16)
        decay = jnp.exp(b_last)
        vb = v.astype(BF16)
        outs = []
        for h in range(H_B):
            cols = slice(h * D_B, (h + 1) * D_B)
            a = jnp.where(tril, _dot_t(q_mid[:, cols], k_mid[:, cols]), 0.0)
            st = st_scr[h]
            o = _dot(a.astype(BF16), vb[:, cols]) + _dot_t(q_in[:, cols], st.astype(BF16))
            st_scr[h] = st * decay[:, cols] + _dot(v[:, cols].T.astype(BF16), k_end[:, cols])
            outs.append(_rms(o, gn[:, cols]))
        o_ref[0, rows, :] = jnp.concatenate(outs, axis=1) * _silu(g_ref[0, rows, :])

    @pl.when(t == pl.num_programs(1) - 1)
    def _():
        for h in range(H_B):
            s_ref[0, h] = st_scr[h].T


def _hgrn_seq(proj, lb, gn, first_chunk, tb):
    b, s, _ = proj.shape
    w = H_B * D_B
    blk = lambda j: pl.BlockSpec((1, tb, w), lambda bi, t: (bi, t, first_chunk + j))
    vec = pl.BlockSpec((1, w), lambda bi, t: (0, 0))
    return pl.pallas_call(
        functools.partial(_hgrn_seq_kernel, tb=tb),
        grid=(b, s // tb),
        in_specs=[blk(0), blk(1), blk(2), blk(3), vec, vec],
        out_specs=[pl.BlockSpec((1, tb, w), lambda bi, t: (bi, t, 0)),
                   pl.BlockSpec((1, H_B, D_B, D_B), lambda bi, t: (bi, 0, 0, 0))],
        out_shape=[jax.ShapeDtypeStruct((b, s, w), F32),
                   jax.ShapeDtypeStruct((b, H_B, D_B, D_B), F32)],
        scratch_shapes=[pltpu.VMEM((H_B, D_B, D_B), F32)],
        compiler_params=_params("parallel", "arbitrary"),
    )(proj, proj, proj, proj, lb.reshape(1, w), jnp.tile(gn.astype(F32), H_B).reshape(1, w))


def _hgrn_step_kernel(q_ref, f_ref, i_ref, g_ref, lb_ref, gn_ref, s0_ref, o_ref, s_ref):
    bi = pl.program_id(0)
    lb = lb_ref[...]
    gn = gn_ref[...]
    f = lb + (1.0 - lb) * _sigmoid(f_ref[pl.ds(bi, 1), :])
    q = _silu(q_ref[pl.ds(bi, 1), :])
    v = i_ref[pl.ds(bi, 1), :]
    eye = (lax.broadcasted_iota(jnp.int32, (D_B, D_B), 0)
           == lax.broadcasted_iota(jnp.int32, (D_B, D_B), 1))
    outs = []
    for h in range(H_B):
        cols = slice(h * D_B, (h + 1) * D_B)
        f_col = jnp.sum(jnp.where(eye, f[:, cols], 0.0), axis=1, keepdims=True)
        q_col = jnp.sum(jnp.where(eye, q[:, cols], 0.0), axis=1, keepdims=True)
        s_new = f_col * s0_ref[0, h] + (1.0 - f_col) * v[:, cols]
        s_ref[0, h] = s_new
        o = jnp.sum(q_col * s_new, axis=0, keepdims=True)
        outs.append(_rms(o, gn[:, cols]))
    o_ref[0] = jnp.concatenate(outs, axis=1) * _silu(g_ref[pl.ds(bi, 1), :])


def _hgrn_step(proj, lb, gn, s0, first_chunk):
    b = proj.shape[0]
    w = H_B * D_B
    blk = lambda j: pl.BlockSpec((b, w), lambda bi: (0, first_chunk + j))
    vec = pl.BlockSpec((1, w), lambda bi: (0, 0))
    st = pl.BlockSpec((1, H_B, D_B, D_B), lambda bi: (bi, 0, 0, 0))
    o, s = pl.pallas_call(
        _hgrn_step_kernel,
        grid=(b,),
        in_specs=[blk(0), blk(1), blk(2), blk(3), vec, vec, st],
        out_specs=[pl.BlockSpec((1, 1, w), lambda bi: (bi, 0, 0)), st],
        out_shape=[jax.ShapeDtypeStruct((b, 1, w), F32),
                   jax.ShapeDtypeStruct((b, H_B, D_B, D_B), F32)],
        compiler_params=_params("parallel"),
    )(proj, proj, proj, proj, lb.reshape(1, w),
      jnp.tile(gn.astype(F32), H_B).reshape(1, w), s0)
    return o.reshape(b, w), s


def _band_kernel(q_ref, kc_ref, kp_ref, vc_ref, vp_ref, o_ref, l_ref, *, tq):
    first = pl.program_id(2) == 0
    row = lax.broadcasted_iota(jnp.int32, (BAND, 2 * BAND), 0)
    col = lax.broadcasted_iota(jnp.int32, (BAND, 2 * BAND), 1)
    band = jnp.logical_and(col >= row, col <= row + BAND)
    lane = lax.broadcasted_iota(jnp.int32, (BAND, LANES), 1)
    low = lane < DH
    for i in range(tq // BAND):
        rows = slice(i * BAND, (i + 1) * BAND)
        if i == 0:
            k_prev, v_prev = kp_ref[0], vp_ref[0]
            mask = jnp.logical_and(band, jnp.logical_or(col >= BAND, jnp.logical_not(first)))
        else:
            prev = slice((i - 1) * BAND, i * BAND)
            k_prev, v_prev = kc_ref[0, prev, :], vc_ref[0, prev, :]
            mask = band
        kk = jnp.concatenate([k_prev, kc_ref[0, rows, :]], axis=0).astype(BF16)
        vv = jnp.concatenate([v_prev, vc_ref[0, rows, :]], axis=0).astype(BF16)
        for hp in range(HEADS // 2):
            cols = slice(hp * LANES, (hp + 1) * LANES)
            q2 = q_ref[0, rows, cols] * SCALE
            res = []
            for e in range(2):
                mine = low if e == 0 else jnp.logical_not(low)
                s = _dot_t(jnp.where(mine, q2, 0.0).astype(BF16), kk[:, cols])
                s = jnp.where(mask, s, NEG_INF)
                m = jnp.max(s, axis=1, keepdims=True)
                p = jnp.exp(s - m)
                l = jnp.sum(p, axis=1, keepdims=True)
                res.append((_dot((p / l).astype(BF16), vv[:, cols]), m + jnp.log(l)))
            o_ref[0, rows, cols] = jnp.where(low, res[0][0], res[1][0])
            l_ref[0, rows, cols] = jnp.where(low, res[0][1], res[1][1])


def _band_attention(proj, g, dil, tq):
    b, s, m = proj.shape
    nchunk = m // W_ATT
    ln = s // dil
    view = proj.reshape(b, ln, dil * m)
    sub = tq // BAND
    cur = lambda comp: pl.BlockSpec((1, tq, W_ATT),
                                    lambda bi, r, l: (bi, l, r * nchunk + 3 * comp + g))
    prev = lambda comp: pl.BlockSpec((1, BAND, W_ATT),
                                     lambda bi, r, l: (bi, jnp.maximum(l * sub - 1, 0),
                                                       r * nchunk + 3 * comp + g))
    out = pl.BlockSpec((1, tq, W_ATT), lambda bi, r, l: (bi, l, r))
    shape = jax.ShapeDtypeStruct((b, ln, dil * W_ATT), F32)
    o, lse = pl.pallas_call(
        functools.partial(_band_kernel, tq=tq),
        grid=(b, dil, ln // tq),
        in_specs=[cur(0), cur(1), prev(1), cur(2), prev(2)],
        out_specs=[out, out],
        out_shape=[shape, shape],
        compiler_params=_params("parallel", "parallel", "arbitrary"),
    )(view, view, view, view, view)
    return o.reshape(b * s, W_ATT), lse.reshape(b * s, W_ATT)


def _dil_dec_kernel(p_ref, k0, v0, k1, v1, k2, v2, o_ref, *, bt):
    caches = ((k0, v0), (k1, v1), (k2, v2))
    ng = len(caches)
    for j in range(bt):
        os, ls = [], []
        for g, (k_ref, v_ref) in enumerate(caches):
            q = p_ref[j:j + 1, g * W_ATT:(g + 1) * W_ATT]
            kn = p_ref[j:j + 1, (ng + g) * W_ATT:(ng + g + 1) * W_ATT]
            vn = p_ref[j:j + 1, (2 * ng + g) * W_ATT:(2 * ng + g + 1) * W_ATT]
            qmat, headmask = _head_rows(q * SCALE)
            s = _dot_t(qmat.astype(BF16), k_ref[j].astype(BF16))
            s_self = jnp.sum(qmat * kn, axis=1, keepdims=True)
            m = jnp.maximum(jnp.max(s, axis=1, keepdims=True), s_self)
            p = jnp.exp(s - m)
            p_self = jnp.exp(s_self - m)
            l = jnp.sum(p, axis=1, keepdims=True) + p_self
            o8 = _dot((p / l).astype(BF16), v_ref[j].astype(BF16)) + (p_self / l) * vn
            lse = m + jnp.log(l)
            os.append(jnp.sum(jnp.where(headmask, o8, 0.0), axis=0, keepdims=True))
            ls.append(jnp.sum(jnp.where(headmask, lse, 0.0), axis=0, keepdims=True))
        o_ref[j:j + 1, :] = _merge_groups(os, ls)


def _dil_dec(proj, caches, bt):
    b, m = proj.shape
    ins, specs = [], [pl.BlockSpec((bt, m), lambda i: (i, 0))]
    for (win, dil), (ck, cv) in zip(C_PATTERNS, caches):
        n_keys = win // dil
        for c in (ck, cv):
            ins.append(c.reshape(b, n_keys, dil * W_ATT))
            specs.append(pl.BlockSpec((bt, n_keys, W_ATT), lambda i: (i, 0, 0)))
    return pl.pallas_call(
        functools.partial(_dil_dec_kernel, bt=bt),
        grid=(b // bt,),
        in_specs=specs,
        out_specs=pl.BlockSpec((bt, W_ATT), lambda i: (i, 0)),
        out_shape=jax.ShapeDtypeStruct((b, W_ATT), F32),
        compiler_params=_params("parallel"),
    )(proj, *ins)


def _shift_kernel(c_ref, n_ref, o_ref):
    lw = c_ref.shape[1]
    o_ref[0, 0:lw - 8, :] = c_ref[0, 1:lw - 7, :]
    tail = pltpu.roll(c_ref[0, lw - 8:lw, :], 7, axis=0)
    last = lax.broadcasted_iota(jnp.int32, tail.shape, 0) == 7
    o_ref[0, lw - 8:lw, :] = jnp.where(last, n_ref[0], tail)


def _shift_append(cache, new):
    b, lw, w = cache.shape
    return pl.pallas_call(
        _shift_kernel,
        grid=(b,),
        in_specs=[pl.BlockSpec((1, lw, w), lambda i: (i, 0, 0)),
                  pl.BlockSpec((1, 1, w), lambda i: (i, 0, 0))],
        out_specs=pl.BlockSpec((1, lw, w), lambda i: (i, 0, 0)),
        out_shape=jax.ShapeDtypeStruct((b, lw, w), F32),
        compiler_params=_params("parallel"),
    )(cache, new.reshape(b, 1, w))


def kernel(x_prompt, x_sample, cache_k_a, cache_v_a, state_hgrn, cache_k_c0, cache_v_c0, cache_k_c1, cache_v_c1, cache_k_c2, cache_v_c2, state_conv, page_table, norm_mix, norm_ffn, w_in_ab, q_norm_a, k_norm_a, lb_logits, g_norm_b, w_out_ab, w_in_c, q_norm_c, k_norm_c, w_out_c, w_gate, w_up, conv_w, conv_b, w_down):
    bsz, seq, d = x_prompt.shape
    nb = x_sample.shape[0]
    n = bsz * seq
    n_pool, page = cache_k_a.shape[1], cache_k_a.shape[2]
    dff = w_gate.shape[2]
    tm, tc = 512, dff // 2
    ng = len(C_PATTERNS)
    c_caches = ((cache_k_c0, cache_v_c0), (cache_k_c1, cache_v_c1), (cache_k_c2, cache_v_c2))

    lb_all = jnp.cumsum(jax.nn.softmax(lb_logits.astype(F32), axis=0), axis=0)
    bf = lambda w: w.astype(BF16)

    xp = x_prompt.reshape(n, d)
    xs = x_sample.reshape(nb, d)

    w_in = bf(w_in_ab[0])
    n_chunks = w_in.shape[1] // W_ATT
    gains = _head_gains([q_norm_a[0], k_norm_a[0]], n_chunks)
    w_out = bf(w_out_ab[0])
    lb = lb_all[0]

    pp = _norm_proj(xp, norm_mix[0], w_in, gains, 2, tm).reshape(bsz, seq, -1)
    qa, ka, va = _moba_prep(pp)
    oa = _moba_attn(qa, ka, va).reshape(n, W_ATT)
    ob, hg_p = _hgrn_seq(pp, lb, g_norm_b[0], 3, 256)
    xp = _out_ab(oa, ob.reshape(n, -1), w_out, xp, tm)
    k_a_p = pp[:, :, W_ATT:2 * W_ATT].reshape(1, bsz, seq, HEADS, DH)
    v_a_p = pp[:, :, 2 * W_ATT:3 * W_ATT].reshape(1, bsz, seq, HEADS, DH)

    ps = _norm_proj(xs, norm_mix[0], w_in, gains, 2, nb)
    kn, vn = ps[:, W_ATT:2 * W_ATT], ps[:, 2 * W_ATT:3 * W_ATT]
    oas = _moba_dec(ps[:, :W_ATT], kn, vn, cache_k_a[0].reshape(n_pool, page, W_ATT),
                    cache_v_a[0].reshape(n_pool, page, W_ATT), page_table)
    obs, hg_s = _hgrn_step(ps, lb, g_norm_b[0], state_hgrn[0], 3)
    xs = _out_ab(oas, obs, w_out, xs, nb)
    k_a_s = kn.reshape(1, nb, 1, HEADS, DH)
    v_a_s = vn.reshape(1, nb, 1, HEADS, DH)

    conv_p, conv_s = [], []

    def ffn(l, xp, xs):
        wg, wu, wd = bf(w_gate[l]), bf(w_up[l]), bf(w_down[l])
        xp, cp = _ffn_seq(xp, norm_ffn[l], wg, wu, conv_w[l], conv_b[l], wd, bsz, tm, tc)
        xs, a_s = _ffn_step(xs, norm_ffn[l], wg, wu, conv_w[l], conv_b[l], wd,
                            state_conv[l, :, 0, :], state_conv[l, :, 1, :], tc)
        conv_p.append(cp)
        conv_s.append(jnp.stack([state_conv[l, :, 1, :], a_s], axis=1))
        return xp, xs

    xp, xs = ffn(0, xp, xs)

    w_in = bf(w_in_c[0])
    n_chunks = w_in.shape[1] // W_ATT
    gains = _head_gains([q_norm_c[0]] * ng + [k_norm_c[0]] * ng, n_chunks)
    w_out = bf(w_out_c[0])

    pp = _norm_proj(xp, norm_mix[1], w_in, gains, 2 * ng, tm).reshape(bsz, seq, -1)
    os, ls = [], []
    for g, (win, dil) in enumerate(C_PATTERNS):
        o, lse = _band_attention(pp, g, dil, 512)
        os.append(o)
        ls.append(lse)
    xp = _out_c(os, ls, w_out, xp, tm)

    ps = _norm_proj(xs, norm_mix[1], w_in, gains, 2 * ng, nb)
    sample_caches = [(ck[0].reshape(nb, -1, W_ATT), cv[0].reshape(nb, -1, W_ATT))
                     for ck, cv in c_caches]
    ocs = _dil_dec(ps, sample_caches, 8)
    xs = _out_plain(ocs, w_out, xs)

    c_out = []
    for g, (win, dil) in enumerate(C_PATTERNS):
        keep = min(win, seq)
        kcol = slice((ng + g) * W_ATT, (ng + g + 1) * W_ATT)
        vcol = slice((2 * ng + g) * W_ATT, (2 * ng + g + 1) * W_ATT)
        c_out += [pp[:, seq - keep:, kcol].reshape(1, bsz, keep, HEADS, DH),
                  pp[:, seq - keep:, vcol].reshape(1, bsz, keep, HEADS, DH)]
        ck, cv = sample_caches[g]
        c_out += [_shift_append(ck, ps[:, kcol]).reshape(1, nb, -1, HEADS, DH),
                  _shift_append(cv, ps[:, vcol]).reshape(1, nb, -1, HEADS, DH)]

    xp, xs = ffn(1, xp, xs)

    return (xp.reshape(bsz, seq, d), xs.reshape(nb, 1, d),
            k_a_p, v_a_p, k_a_s, v_a_s, hg_p[None], hg_s[None],
            *c_out, jnp.stack(conv_p), jnp.stack(conv_s))
```

```python
import functools

import jax
import jax.numpy as jnp
from jax import lax
from jax.experimental import pallas as pl
from jax.experimental.pallas import tpu as pltpu

F32 = jnp.float32
BF16 = jnp.bfloat16
HIGHEST = lax.Precision.HIGHEST

RMS_EPS = 1e-6
NEG_INF = -1e30
LOWEST = -3e38

LANES = 128
DH = 64
HEADS = 8
W_ATT = HEADS * DH
SCALE = DH ** -0.5
MOBA_BLOCK = 256
MOBA_TOPK = 3
H_B = 4
D_B = 128
HGRN_CHUNK = 32
BAND = 128
C_PATTERNS = ((128, 1), (512, 4), (2048, 16))
VMEM_LIMIT = 48 * 1024 * 1024

_CONTRACT_LAST = (((1,), (1,)), ((), ()))


def _params(*sem):
    return pltpu.CompilerParams(dimension_semantics=sem, vmem_limit_bytes=VMEM_LIMIT)


def _sigmoid(x):
    return 1.0 / (1.0 + jnp.exp(-x))


def _silu(x):
    return x * _sigmoid(x)


def _rms(x, g):
    return x * lax.rsqrt(jnp.mean(x * x, axis=-1, keepdims=True) + RMS_EPS) * g


def _dot(a, b):
    return jnp.dot(a, b, preferred_element_type=F32)


def _dot_t(a, b, precision=None):
    return lax.dot_general(a, b, _CONTRACT_LAST, precision=precision, preferred_element_type=F32)


def _norm_proj_kernel(x_ref, g_ref, w_ref, hg_ref, p_ref, o_ref, h_scr, *, n_norm):
    j = pl.program_id(1)

    @pl.when(j == 0)
    def _():
        h_scr[...] = _rms(x_ref[...], g_ref[...]).astype(BF16)

    y = _dot(h_scr[...], w_ref[...])

    @pl.when(j < n_norm)
    def _():
        s = y * y
        hi = s.astype(BF16)
        lo = (s - hi.astype(F32)).astype(BF16)
        ms = _dot(hi, p_ref[...]) + _dot(lo, p_ref[...])
        o_ref[...] = y * lax.rsqrt(ms + RMS_EPS) * hg_ref[0]

    @pl.when(j >= n_norm)
    def _():
        o_ref[...] = y


def _norm_proj(x, g, w_bf, head_gains, n_norm, tm):
    n, d = x.shape
    m = w_bf.shape[1]
    nj = m // W_ATT
    blockdiag = (jnp.arange(W_ATT)[:, None] // DH == jnp.arange(W_ATT)[None, :] // DH)
    p = (blockdiag.astype(F32) / DH).astype(BF16)
    return pl.pallas_call(
        functools.partial(_norm_proj_kernel, n_norm=n_norm),
        grid=(n // tm, nj),
        in_specs=[
            pl.BlockSpec((tm, d), lambda i, j: (i, 0)),
            pl.BlockSpec((1, d), lambda i, j: (0, 0)),
            pl.BlockSpec((d, W_ATT), lambda i, j: (0, j)),
            pl.BlockSpec((1, 1, W_ATT), lambda i, j: (j, 0, 0)),
            pl.BlockSpec((W_ATT, W_ATT), lambda i, j: (0, 0)),
        ],
        out_specs=pl.BlockSpec((tm, W_ATT), lambda i, j: (i, j)),
        out_shape=jax.ShapeDtypeStruct((n, m), F32),
        scratch_shapes=[pltpu.VMEM((tm, d), BF16)],
        compiler_params=_params("parallel", "arbitrary"),
    )(x, g.reshape(1, d), w_bf, head_gains, p)


def _head_gains(gains, n_chunks):
    rows = [jnp.tile(gv.astype(F32), HEADS) for gv in gains]
    rows += [jnp.ones((W_ATT,), F32)] * (n_chunks - len(rows))
    return jnp.stack(rows).reshape(n_chunks, 1, W_ATT)


def _ffn_seq_kernel(x_ref, g_ref, wg_ref, wu_ref, cw_ref, cb_ref, wd_ref, o_ref, cs_ref,
                    h_scr, a_scr, acc_scr, *, tm, blocks_per_seq):
    i = pl.program_id(0)
    c = pl.program_id(1)

    @pl.when(c == 0)
    def _():
        h_scr[...] = _rms(x_ref[...], g_ref[...]).astype(BF16)

    @pl.when(i % blocks_per_seq == 0)
    def _():
        a_scr[c, 0:8, :] = jnp.zeros((8, a_scr.shape[2]), F32)

    h = h_scr[...]
    a = _dot(h, wg_ref[...])
    u = _dot(h, wu_ref[...])
    a_scr[c, 8:8 + tm, :] = a
    a1 = a_scr[c, 7:7 + tm, :]
    a2 = a_scr[c, 6:6 + tm, :]
    cw = cw_ref[...]
    conv = cb_ref[...] + a2 * cw[0:1] + a1 * cw[1:2] + a * cw[2:3]
    y = (_silu(conv) * u).astype(BF16)
    part = _dot(y, wd_ref[...])

    @pl.when(c == 0)
    def _():
        acc_scr[...] = part

    @pl.when(c > 0)
    def _():
        acc_scr[...] = acc_scr[...] + part

    a_scr[c, 0:8, :] = a_scr[c, tm:tm + 8, :]
    tc = a_scr.shape[2]
    for cc in range(a_scr.shape[0]):
        @pl.when(c == cc)
        def _(cc=cc):
            cs_ref[0, :, cc * tc:(cc + 1) * tc] = a_scr[cc, tm + 6:tm + 8, :]

    @pl.when(c == pl.num_programs(1) - 1)
    def _():
        o_ref[...] = x_ref[...] + acc_scr[...]


def _ffn_seq(x, g, wg, wu, cw, cb, wd, n_seq, tm, tc):
    n, d = x.shape
    dff = wg.shape[1]
    nc = dff // tc
    bps = n // n_seq // tm
    return pl.pallas_call(
        functools.partial(_ffn_seq_kernel, tm=tm, blocks_per_seq=bps),
        grid=(n // tm, nc),
        in_specs=[
            pl.BlockSpec((tm, d), lambda i, c: (i, 0)),
            pl.BlockSpec((1, d), lambda i, c: (0, 0)),
            pl.BlockSpec((d, tc), lambda i, c: (0, c)),
            pl.BlockSpec((d, tc), lambda i, c: (0, c)),
            pl.BlockSpec((3, tc), lambda i, c: (0, c)),
            pl.BlockSpec((1, tc), lambda i, c: (0, c)),
            pl.BlockSpec((tc, d), lambda i, c: (c, 0)),
        ],
        out_specs=[
            pl.BlockSpec((tm, d), lambda i, c: (i, 0)),
            pl.BlockSpec((1, 2, dff), lambda i, c: (i // bps, 0, 0)),
        ],
        out_shape=[jax.ShapeDtypeStruct((n, d), F32),
                   jax.ShapeDtypeStruct((n_seq, 2, dff), F32)],
        scratch_shapes=[pltpu.VMEM((tm, d), BF16),
                        pltpu.VMEM((nc, tm + 8, tc), F32),
                        pltpu.VMEM((tm, d), F32)],
        compiler_params=_params("arbitrary", "arbitrary"),
    )(x, g.reshape(1, d), wg, wu, cw, cb.reshape(1, dff), wd)


def _ffn_step_kernel(x_ref, g_ref, wg_ref, wu_ref, cw_ref, cb_ref, wd_ref, s0_ref, s1_ref,
                     o_ref, a_ref, h_scr, acc_scr):
    c = pl.program_id(0)

    @pl.when(c == 0)
    def _():
        h_scr[...] = _rms(x_ref[...], g_ref[...]).astype(BF16)

    h = h_scr[...]
    a = _dot(h, wg_ref[...])
    u = _dot(h, wu_ref[...])
    a_ref[...] = a
    cw = cw_ref[...]
    conv = cb_ref[...] + s0_ref[...] * cw[0:1] + s1_ref[...] * cw[1:2] + a * cw[2:3]
    y = (_silu(conv) * u).astype(BF16)
    part = _dot(y, wd_ref[...])

    @pl.when(c == 0)
    def _():
        acc_scr[...] = part

    @pl.when(c > 0)
    def _():
        acc_scr[...] = acc_scr[...] + part

    @pl.when(c == pl.num_programs(0) - 1)
    def _():
        o_ref[...] = x_ref[...] + acc_scr[...]


def _ffn_step(x, g, wg, wu, cw, cb, wd, s0, s1, tc):
    n, d = x.shape
    dff = wg.shape[1]
    return pl.pallas_call(
        _ffn_step_kernel,
        grid=(dff // tc,),
        in_specs=[
            pl.BlockSpec((n, d), lambda c: (0, 0)),
            pl.BlockSpec((1, d), lambda c: (0, 0)),
            pl.BlockSpec((d, tc), lambda c: (0, c)),
            pl.BlockSpec((d, tc), lambda c: (0, c)),
            pl.BlockSpec((3, tc), lambda c: (0, c)),
            pl.BlockSpec((1, tc), lambda c: (0, c)),
            pl.BlockSpec((tc, d), lambda c: (c, 0)),
            pl.BlockSpec((n, tc), lambda c: (0, c)),
            pl.BlockSpec((n, tc), lambda c: (0, c)),
        ],
        out_specs=[pl.BlockSpec((n, d), lambda c: (0, 0)),
                   pl.BlockSpec((n, tc), lambda c: (0, c))],
        out_shape=[jax.ShapeDtypeStruct((n, d), F32), jax.ShapeDtypeStruct((n, dff), F32)],
        scratch_shapes=[pltpu.VMEM((n, d), BF16), pltpu.VMEM((n, d), F32)],
        compiler_params=_params("arbitrary"),
    )(x, g.reshape(1, d), wg, wu, cw, cb.reshape(1, dff), wd, s0, s1)


def _out_ab_kernel(oa_ref, ob_ref, w_ref, x_ref, o_ref):
    ka = oa_ref.shape[1]
    o_ref[...] = (x_ref[...] + _dot(oa_ref[...].astype(BF16), w_ref[0:ka, :])
                  + _dot(ob_ref[...].astype(BF16), w_ref[ka:, :]))


def _out_ab(oa, ob, w_bf, x, tm):
    n, d = x.shape
    ka, kb = oa.shape[1], ob.shape[1]
    return pl.pallas_call(
        _out_ab_kernel,
        grid=(n // tm,),
        in_specs=[pl.BlockSpec((tm, ka), lambda i: (i, 0)),
                  pl.BlockSpec((tm, kb), lambda i: (i, 0)),
                  pl.BlockSpec((ka + kb, d), lambda i: (0, 0)),
                  pl.BlockSpec((tm, d), lambda i: (i, 0))],
        out_specs=pl.BlockSpec((tm, d), lambda i: (i, 0)),
        out_shape=jax.ShapeDtypeStruct((n, d), F32),
        compiler_params=_params("parallel"),
    )(oa, ob, w_bf, x)


def _merge_groups(os, ls):
    m = jnp.maximum(jnp.maximum(ls[0], ls[1]), ls[2])
    es = [jnp.exp(l - m) for l in ls]
    den = es[0] + es[1] + es[2]
    return (es[0] / den) * os[0] + (es[1] / den) * os[1] + (es[2] / den) * os[2]


def _out_c_kernel(o0, o1, o2, l0, l1, l2, w_ref, x_ref, o_ref):
    merged = _merge_groups([o0[...], o1[...], o2[...]], [l0[...], l1[...], l2[...]])
    o_ref[...] = x_ref[...] + _dot(merged.astype(BF16), w_ref[...])


def _out_c(os, ls, w_bf, x, tm):
    n, d = x.shape
    k = w_bf.shape[0]
    act = pl.BlockSpec((tm, k), lambda i: (i, 0))
    return pl.pallas_call(
        _out_c_kernel,
        grid=(n // tm,),
        in_specs=[act] * 6 + [pl.BlockSpec((k, d), lambda i: (0, 0)),
                              pl.BlockSpec((tm, d), lambda i: (i, 0))],
        out_specs=pl.BlockSpec((tm, d), lambda i: (i, 0)),
        out_shape=jax.ShapeDtypeStruct((n, d), F32),
        compiler_params=_params("parallel"),
    )(*os, *ls, w_bf, x)


def _out_ab_t_kernel(oat_ref, ob_ref, w_ref, x_ref, o_ref):
    ka = oat_ref.shape[0]
    o_ref[...] = (x_ref[...] + _dot(oat_ref[...].T.astype(BF16), w_ref[0:ka, :])
                  + _dot(ob_ref[...].astype(BF16), w_ref[ka:, :]))


def _out_ab_t(oa_t, ob, w_bf, x):
    return pl.pallas_call(
        _out_ab_t_kernel,
        out_shape=jax.ShapeDtypeStruct(x.shape, F32),
        compiler_params=pltpu.CompilerParams(vmem_limit_bytes=VMEM_LIMIT),
    )(oa_t, ob, w_bf, x)


def _out_c_t_kernel(o0, o1, o2, l0, l1, l2, w_ref, x_ref, o_ref):
    merged = _merge_groups([o0[...], o1[...], o2[...]], [l0[...], l1[...], l2[...]])
    o_ref[...] = x_ref[...] + _dot(merged.T.astype(BF16), w_ref[...])


def _out_c_t(os_t, ls_t, w_bf, x):
    return pl.pallas_call(
        _out_c_t_kernel,
        out_shape=jax.ShapeDtypeStruct(x.shape, F32),
        compiler_params=pltpu.CompilerParams(vmem_limit_bytes=VMEM_LIMIT),
    )(*os_t, *ls_t, w_bf, x)


def _top_k_mask(g, lane, k):
    sel = jnp.zeros(g.shape, F32)
    lane_f = lane.astype(F32)
    for _ in range(k):
        m = jnp.max(g, axis=1, keepdims=True)
        idx = jnp.min(jnp.where(g == m, lane_f, float(g.shape[1])), axis=1, keepdims=True)
        hit = lane_f == idx
        sel = jnp.where(hit, 1.0, sel)
        g = jnp.where(hit, LOWEST, g)
    return sel


def _moba_prep_kernel(q_ref, k_ref, v_ref, qa_ref, ka_ref, va_ref, kmean_scr):
    i = pl.program_id(1)

    @pl.when(i == 0)
    def _():
        kmean_scr[...] = jnp.zeros(kmean_scr.shape, F32)

    rows = q_ref.shape[1]
    lane = lax.broadcasted_iota(jnp.int32, (rows, LANES), 1)
    low = lane < DH
    for hp in range(HEADS // 2):
        cols = slice(hp * LANES, (hp + 1) * LANES)
        q2 = q_ref[0, :, cols]
        k2 = k_ref[0, :, cols]
        v2 = v_ref[0, :, cols]
        km2 = kmean_scr[:, cols]
        for e in range(2):
            mine = low if e == 0 else jnp.logical_not(low)
            gate = _dot_t(jnp.where(mine, q2, 0.0), km2, precision=HIGHEST)
            past = lane < i
            sel = _top_k_mask(jnp.where(past, gate, NEG_INF), lane, MOBA_TOPK)
            keep = jnp.where(past, sel, 0.0) + jnp.where(lane == i, 1.0, 0.0)
            bias = jnp.where(keep > 0.5, 0.0, NEG_INF)
            onehot = jnp.where(lane == i, 1.0, 0.0)
            if e == 0:
                bias = pltpu.roll(bias, DH, axis=1)
                onehot = jnp.where(lane == i + DH, 1.0, 0.0)
                ones_col = jnp.where(lane == DH, 1.0, 0.0)
            else:
                ones_col = jnp.where(lane == 0, 1.0, 0.0)
            h = 2 * hp + e
            qa_ref[0, h] = jnp.where(mine, q2 * SCALE, bias).astype(BF16)
            ka_ref[0, h] = jnp.where(mine, k2, onehot).astype(BF16)
            va_ref[0, h] = jnp.where(mine, v2, ones_col).astype(BF16)

    kmean_scr[pl.ds(i, 1), :] = jnp.mean(k_ref[0], axis=0, keepdims=True)


def _moba_prep(proj, n_blocks_max=LANES):
    b, s, _ = proj.shape
    nb = s // MOBA_BLOCK
    assert nb <= n_blocks_max // 2 and nb <= DH // 2
    blk = lambda j: pl.BlockSpec((1, MOBA_BLOCK, W_ATT), lambda bi, i: (bi, i, j))
    aug = pl.BlockSpec((1, HEADS, MOBA_BLOCK, LANES), lambda bi, i: (bi, 0, i, 0))
    shape = jax.ShapeDtypeStruct((b, HEADS, s, LANES), BF16)
    return pl.pallas_call(
        _moba_prep_kernel,
        grid=(b, nb),
        in_specs=[blk(0), blk(1), blk(2)],
        out_specs=[aug, aug, aug],
        out_shape=[shape, shape, shape],
        scratch_shapes=[pltpu.VMEM((LANES, W_ATT), F32)],
        compiler_params=_params("arbitrary", "arbitrary"),
    )(proj, proj, proj)


def _moba_attn_kernel(q_ref, k_ref, v_ref, o_ref, *, tq):
    qi = pl.program_id(2)
    tk = MOBA_BLOCK
    n_diag = tq // tk
    row = lax.broadcasted_iota(jnp.int32, (tq, tk), 0)
    col = lax.broadcasted_iota(jnp.int32, (tq, tk), 1)
    lane = lax.broadcasted_iota(jnp.int32, (tq, LANES), 1)
    q = [q_ref[0, e] for e in range(2)]

    def step(n, carry, causal_offset=None):
        st = pl.multiple_of(n * tk, tk)
        out = []
        for e in range(2):
            m, acc = carry[e]
            s = _dot_t(q[e], k_ref[0, e, pl.ds(st, tk), :])
            if causal_offset is not None:
                s = jnp.where(col + causal_offset <= row, s, NEG_INF)
            m_new = jnp.maximum(m, jnp.max(s, axis=1, keepdims=True))
            p = jnp.exp(s - m_new).astype(BF16)
            acc = jnp.exp(m - m_new) * acc + _dot(p, v_ref[0, e, pl.ds(st, tk), :])
            out.append((m_new, acc))
        return tuple(out)

    init = (jnp.full((tq, 1), LOWEST, F32), jnp.zeros((tq, LANES), F32))
    carry = (init, init)
    for j in range(n_diag):
        carry = step(qi * n_diag + j, carry, causal_offset=j * tk)
    carry = lax.fori_loop(0, qi * n_diag, step, carry)
    outs = []
    for e in range(2):
        acc = carry[e][1]
        lcol = DH if e == 0 else 0
        outs.append(acc / acc[:, lcol:lcol + 1])
    o_ref[0] = jnp.where(lane < DH, outs[0], outs[1])


def _moba_attn(qa, ka, va, tq):
    b, h, s, _ = qa.shape
    return pl.pallas_call(
        functools.partial(_moba_attn_kernel, tq=tq),
        grid=(b, h // 2, s // tq),
        in_specs=[pl.BlockSpec((1, 2, tq, LANES), lambda bi, hp, i: (bi, hp, i, 0)),
                  pl.BlockSpec((1, 2, s, LANES), lambda bi, hp, i: (bi, hp, 0, 0)),
                  pl.BlockSpec((1, 2, s, LANES), lambda bi, hp, i: (bi, hp, 0, 0))],
        out_specs=pl.BlockSpec((1, tq, LANES), lambda bi, hp, i: (bi, i, hp)),
        out_shape=jax.ShapeDtypeStruct((b, s, W_ATT), F32),
        compiler_params=_params("parallel", "parallel", "arbitrary"),
    )(qa, ka, va)


def _column(ref, rows, lane_is_seq):
    return jnp.sum(jnp.where(lane_is_seq, ref[rows, :], 0.0), axis=1, keepdims=True)


def _moba_dec_kernel(pt_ref, q_ref, kn_ref, vn_ref, *refs, n_pages, page):
    k_refs, v_refs, o_ref = refs[:n_pages], refs[n_pages:2 * n_pages], refs[2 * n_pages]
    bi = pl.program_id(0)
    pages_per_block = MOBA_BLOCK // page
    n_past = n_pages // pages_per_block

    @pl.when(bi == 0)
    def _():
        o_ref[...] = jnp.zeros(o_ref.shape, F32)

    lane_is_seq = lax.broadcasted_iota(jnp.int32, (DH, LANES), 1) == bi
    for h in range(HEADS):
        rows = slice(h * DH, (h + 1) * DH)
        qc = _column(q_ref, rows, lane_is_seq)
        kc = _column(kn_ref, rows, lane_is_seq)
        vc = _column(vn_ref, rows, lane_is_seq)
        raw = [jnp.sum(k_refs[j][0, h] * qc, axis=0, keepdims=True) for j in range(n_pages)]
        gates = []
        for n in range(n_past):
            tot = raw[n * pages_per_block]
            for j in range(1, pages_per_block):
                tot = tot + raw[n * pages_per_block + j]
            gates.append(jnp.sum(tot, axis=1, keepdims=True) * (1.0 / MOBA_BLOCK))
        sel = [jnp.zeros((1, 1), F32)] * n_past
        for _ in range(min(MOBA_TOPK, n_past + 1)):
            best = functools.reduce(jnp.maximum, gates)
            free = jnp.ones((1, 1), F32)
            for n in range(n_past):
                hit = jnp.where(gates[n] == best, free, 0.0)
                free = free - hit
                sel[n] = jnp.maximum(sel[n], hit)
                gates[n] = jnp.where(hit > 0.5, LOWEST, gates[n])
        s_self = jnp.sum(qc * kc, axis=0, keepdims=True) * SCALE
        s = [jnp.where(sel[j // pages_per_block] > 0.5, raw[j] * SCALE, NEG_INF)
             for j in range(n_pages)]
        m = s_self
        for sj in s:
            m = jnp.maximum(m, jnp.max(sj, axis=1, keepdims=True))
        p_self = jnp.exp(s_self - m)
        l = p_self
        acc = jnp.zeros((DH, page), F32)
        for j in range(n_pages):
            pj = jnp.exp(s[j] - m)
            l = l + jnp.sum(pj, axis=1, keepdims=True)
            acc = acc + v_refs[j][0, h] * pj
        o = (jnp.sum(acc, axis=1, keepdims=True) + p_self * vc) / l
        o_ref[rows, :] = jnp.where(lane_is_seq, o, o_ref[rows, :])


def _moba_dec(proj_t, cache_k, cache_v, page_table):
    b = proj_t.shape[1]
    n_pages = page_table.shape[1]
    page = cache_k.shape[3]
    assert b == LANES
    vec = lambda j: pl.BlockSpec((W_ATT, b), lambda bi, pt: (j, 0))
    pg = lambda j: pl.BlockSpec((1, HEADS, DH, page),
                                lambda bi, pt: (pt[bi * n_pages + j], 0, 0, 0))
    pages = [pg(j) for j in range(n_pages)]
    return pl.pallas_call(
        functools.partial(_moba_dec_kernel, n_pages=n_pages, page=page),
        grid_spec=pltpu.PrefetchScalarGridSpec(
            num_scalar_prefetch=1,
            grid=(b,),
            in_specs=[vec(0), vec(1), vec(2)] + pages + pages,
            out_specs=pl.BlockSpec((W_ATT, b), lambda bi, pt: (0, 0)),
        ),
        out_shape=jax.ShapeDtypeStruct((W_ATT, b), F32),
        compiler_params=_params("arbitrary"),
    )(page_table.reshape(-1), proj_t, proj_t, proj_t,
      *([cache_k] * n_pages), *([cache_v] * n_pages))


def _hgrn_seq_kernel(q_ref, f_ref, i_ref, g_ref, lb_ref, gn_ref, o_ref, s_ref, st_scr, *, tb):
    t = pl.program_id(1)

    @pl.when(t == 0)
    def _():
        st_scr[...] = jnp.zeros(st_scr.shape, F32)

    c = HGRN_CHUNK
    lb = lb_ref[...]
    gn = gn_ref[...]
    r_i = lax.broadcasted_iota(jnp.int32, (c, c), 0)
    c_i = lax.broadcasted_iota(jnp.int32, (c, c), 1)
    tril = c_i <= r_i
    ones_tril = jnp.where(tril, 1.0, 0.0)
    for ch in range(tb // c):
        rows = slice(ch * c, (ch + 1) * c)
        f = lb + (1.0 - lb) * _sigmoid(f_ref[0, rows, :])
        k = 1.0 - f
        q = _silu(q_ref[0, rows, :])
        v = i_ref[0, rows, :]
        b = jnp.dot(ones_tril, jnp.log(f), precision=HIGHEST, preferred_element_type=F32)
        b_mid = b[c // 2 - 1:c // 2, :]
        b_last = b[c - 1:c, :]
        q_mid = (q * jnp.exp(b - b_mid)).astype(BF16)
        k_mid = (k * jnp.exp(b_mid - b)).astype(BF16)
        k_end = (k * jnp.exp(b_last - b)).astype(BF16)
        q_in = (q * jnp.exp(b)).astype(BF16)
        decay = jnp.exp(b_last)
        vb = v.astype(BF16)
        outs = []
        for h in range(H_B):
            cols = slice(h * D_B, (h + 1) * D_B)
            a = jnp.where(tril, _dot_t(q_mid[:, cols], k_mid[:, cols]), 0.0)
            st = st_scr[h]
            o = _dot(a.astype(BF16), vb[:, cols]) + _dot_t(q_in[:, cols], st.astype(BF16))
            st_scr[h] = st * decay[:, cols] + _dot(v[:, cols].T.astype(BF16), k_end[:, cols])
            outs.append(_rms(o, gn[:, cols]))
        o_ref[0, rows, :] = jnp.concatenate(outs, axis=1) * _silu(g_ref[0, rows, :])

    @pl.when(t == pl.num_programs(1) - 1)
    def _():
        for h in range(H_B):
            s_ref[0, h] = st_scr[h].T


def _hgrn_seq(proj, lb, gn, first_chunk, tb):
    b, s, _ = proj.shape
    w = H_B * D_B
    blk = lambda j: pl.BlockSpec((1, tb, w), lambda bi, t: (bi, t, first_chunk + j))
    vec = pl.BlockSpec((1, w), lambda bi, t: (0, 0))
    return pl.pallas_call(
        functools.partial(_hgrn_seq_kernel, tb=tb),
        grid=(b, s // tb),
        in_specs=[blk(0), blk(1), blk(2), blk(3), vec, vec],
        out_specs=[pl.BlockSpec((1, tb, w), lambda bi, t: (bi, t, 0)),
                   pl.BlockSpec((1, H_B, D_B, D_B), lambda bi, t: (bi, 0, 0, 0))],
        out_shape=[jax.ShapeDtypeStruct((b, s, w), F32),
                   jax.ShapeDtypeStruct((b, H_B, D_B, D_B), F32)],
        scratch_shapes=[pltpu.VMEM((H_B, D_B, D_B), F32)],
        compiler_params=_params("parallel", "arbitrary"),
    )(proj, proj, proj, proj, lb.reshape(1, w), jnp.tile(gn.astype(F32), H_B).reshape(1, w))


def _hgrn_step_kernel(q_ref, f_ref, i_ref, g_ref, lb_ref, gn_ref, s0_ref, o_ref, s_ref):
    bi = pl.program_id(0)
    lb = lb_ref[...]
    gn = gn_ref[...]
    f = lb + (1.0 - lb) * _sigmoid(f_ref[pl.ds(bi, 1), :])
    q = _silu(q_ref[pl.ds(bi, 1), :])
    v = i_ref[pl.ds(bi, 1), :]
    eye = (lax.broadcasted_iota(jnp.int32, (D_B, D_B), 0)
           == lax.broadcasted_iota(jnp.int32, (D_B, D_B), 1))
    outs = []
    for h in range(H_B):
        cols = slice(h * D_B, (h + 1) * D_B)
        f_col = jnp.sum(jnp.where(eye, f[:, cols], 0.0), axis=1, keepdims=True)
        q_col = jnp.sum(jnp.where(eye, q[:, cols], 0.0), axis=1, keepdims=True)
        s_new = f_col * s0_ref[0, h] + (1.0 - f_col) * v[:, cols]
        s_ref[0, h] = s_new
        o = jnp.sum(q_col * s_new, axis=0, keepdims=True)
        outs.append(_rms(o, gn[:, cols]))
    o_ref[0] = jnp.concatenate(outs, axis=1) * _silu(g_ref[pl.ds(bi, 1), :])


def _hgrn_step(proj, lb, gn, s0, first_chunk):
    b = proj.shape[0]
    w = H_B * D_B
    blk = lambda j: pl.BlockSpec((b, w), lambda bi: (0, first_chunk + j))
    vec = pl.BlockSpec((1, w), lambda bi: (0, 0))
    st = pl.BlockSpec((1, H_B, D_B, D_B), lambda bi: (bi, 0, 0, 0))
    o, s = pl.pallas_call(
        _hgrn_step_kernel,
        grid=(b,),
        in_specs=[blk(0), blk(1), blk(2), blk(3), vec, vec, st],
        out_specs=[pl.BlockSpec((1, 1, w), lambda bi: (bi, 0, 0)), st],
        out_shape=[jax.ShapeDtypeStruct((b, 1, w), F32),
                   jax.ShapeDtypeStruct((b, H_B, D_B, D_B), F32)],
        compiler_params=_params("parallel"),
    )(proj, proj, proj, proj, lb.reshape(1, w),
      jnp.tile(gn.astype(F32), H_B).reshape(1, w), s0)
    return o.reshape(b, w), s


def _band_kernel(q_ref, kc_ref, kp_ref, vc_ref, vp_ref, o_ref, l_ref, *, tq):
    first = pl.program_id(2) == 0
    row = lax.broadcasted_iota(jnp.int32, (BAND, 2 * BAND), 0)
    col = lax.broadcasted_iota(jnp.int32, (BAND, 2 * BAND), 1)
    band = jnp.logical_and(col >= row, col <= row + BAND)
    lane = lax.broadcasted_iota(jnp.int32, (BAND, LANES), 1)
    low = lane < DH
    for i in range(tq // BAND):
        rows = slice(i * BAND, (i + 1) * BAND)
        if i == 0:
            k_prev, v_prev = kp_ref[0], vp_ref[0]
            mask = jnp.logical_and(band, jnp.logical_or(col >= BAND, jnp.logical_not(first)))
        else:
            prev = slice((i - 1) * BAND, i * BAND)
            k_prev, v_prev = kc_ref[0, prev, :], vc_ref[0, prev, :]
            mask = band
        kk = jnp.concatenate([k_prev, kc_ref[0, rows, :]], axis=0).astype(BF16)
        vv = jnp.concatenate([v_prev, vc_ref[0, rows, :]], axis=0).astype(BF16)
        for hp in range(HEADS // 2):
            cols = slice(hp * LANES, (hp + 1) * LANES)
            q2 = q_ref[0, rows, cols] * SCALE
            res = []
            for e in range(2):
                mine = low if e == 0 else jnp.logical_not(low)
                s = _dot_t(jnp.where(mine, q2, 0.0).astype(BF16), kk[:, cols])
                s = jnp.where(mask, s, NEG_INF)
                m = jnp.max(s, axis=1, keepdims=True)
                p = jnp.exp(s - m)
                l = jnp.sum(p, axis=1, keepdims=True)
                res.append((_dot((p / l).astype(BF16), vv[:, cols]), m + jnp.log(l)))
            o_ref[0, rows, cols] = jnp.where(low, res[0][0], res[1][0])
            l_ref[0, rows, cols] = jnp.where(low, res[0][1], res[1][1])


def _band_attention(proj, g, dil, tq):
    b, s, m = proj.shape
    nchunk = m // W_ATT
    ln = s // dil
    view = proj.reshape(b, ln, dil * m)
    sub = tq // BAND
    cur = lambda comp: pl.BlockSpec((1, tq, W_ATT),
                                    lambda bi, r, l: (bi, l, r * nchunk + 3 * comp + g))
    prev = lambda comp: pl.BlockSpec((1, BAND, W_ATT),
                                     lambda bi, r, l: (bi, jnp.maximum(l * sub - 1, 0),
                                                       r * nchunk + 3 * comp + g))
    out = pl.BlockSpec((1, tq, W_ATT), lambda bi, r, l: (bi, l, r))
    shape = jax.ShapeDtypeStruct((b, ln, dil * W_ATT), F32)
    o, lse = pl.pallas_call(
        functools.partial(_band_kernel, tq=tq),
        grid=(b, dil, ln // tq),
        in_specs=[cur(0), cur(1), prev(1), cur(2), prev(2)],
        out_specs=[out, out],
        out_shape=[shape, shape],
        compiler_params=_params("parallel", "parallel", "arbitrary"),
    )(view, view, view, view, view)
    return o.reshape(b * s, W_ATT), lse.reshape(b * s, W_ATT)


def _dil_cache_kernel(q_ref, kn_ref, vn_ref, ck_ref, cv_ref, ok_ref, ov_ref, o_ref, l_ref,
                      *, dil, hb):
    bi = pl.program_id(0)
    hg = pl.program_id(1)
    lw = ck_ref.shape[3]

    @pl.when(jnp.logical_and(bi == 0, hg == 0))
    def _():
        o_ref[...] = jnp.zeros(o_ref.shape, F32)
        l_ref[...] = jnp.zeros(l_ref.shape, F32)

    lane_is_seq = lax.broadcasted_iota(jnp.int32, (DH, LANES), 1) == bi
    pos = lax.broadcasted_iota(jnp.int32, (1, lw), 1)
    on_grid = (pos & (dil - 1)) == 0
    newest = lax.broadcasted_iota(jnp.int32, (DH, lw), 1) == lw - 1
    for hh in range(hb):
        rows = pl.ds(pl.multiple_of((hg * hb + hh) * DH, DH), DH)
        qc = _column(q_ref, rows, lane_is_seq) * SCALE
        kc = _column(kn_ref, rows, lane_is_seq)
        vc = _column(vn_ref, rows, lane_is_seq)
        k = ck_ref[0, hh]
        v = cv_ref[0, hh]
        s = jnp.where(on_grid, jnp.sum(k * qc, axis=0, keepdims=True), NEG_INF)
        s_self = jnp.sum(qc * kc, axis=0, keepdims=True)
        m = jnp.maximum(jnp.max(s, axis=1, keepdims=True), s_self)
        p = jnp.exp(s - m)
        p_self = jnp.exp(s_self - m)
        l = jnp.sum(p, axis=1, keepdims=True) + p_self
        o = jnp.sum(v * (p / l), axis=1, keepdims=True) + (p_self / l) * vc
        lse = m + jnp.log(l)
        ok_ref[0, hh] = jnp.where(newest, kc, pltpu.roll(k, lw - 1, axis=1))
        ov_ref[0, hh] = jnp.where(newest, vc, pltpu.roll(v, lw - 1, axis=1))
        o_ref[rows, :] = jnp.where(lane_is_seq, o, o_ref[rows, :])
        l_ref[rows, :] = jnp.where(lane_is_seq, lse, l_ref[rows, :])


def _dil_cache(proj_t, g, n_groups, dil, cache_k, cache_v, hb):
    b, h, dh, lw = cache_k.shape
    assert b == LANES and dil & (dil - 1) == 0
    vec = lambda j: pl.BlockSpec((W_ATT, b), lambda bi, hg: (j, 0))
    cache = pl.BlockSpec((1, hb, dh, lw), lambda bi, hg: (bi, hg, 0, 0))
    res = pl.BlockSpec((W_ATT, b), lambda bi, hg: (0, 0))
    return pl.pallas_call(
        functools.partial(_dil_cache_kernel, dil=dil, hb=hb),
        grid=(b, h // hb),
        in_specs=[vec(g), vec(n_groups + g), vec(2 * n_groups + g), cache, cache],
        out_specs=[cache, cache, res, res],
        out_shape=[jax.ShapeDtypeStruct(cache_k.shape, F32), jax.ShapeDtypeStruct(cache_v.shape, F32),
                   jax.ShapeDtypeStruct((W_ATT, b), F32), jax.ShapeDtypeStruct((W_ATT, b), F32)],
        compiler_params=_params("arbitrary", "arbitrary"),
    )(proj_t, proj_t, proj_t, cache_k, cache_v)


def kernel(x_prompt, x_sample, cache_k_a, cache_v_a, state_hgrn, cache_k_c0, cache_v_c0, cache_k_c1, cache_v_c1, cache_k_c2, cache_v_c2, state_conv, page_table, norm_mix, norm_ffn, w_in_ab, q_norm_a, k_norm_a, lb_logits, g_norm_b, w_out_ab, w_in_c, q_norm_c, k_norm_c, w_out_c, w_gate, w_up, conv_w, conv_b, w_down):
    bsz, seq, d = x_prompt.shape
    nb = x_sample.shape[0]
    n = bsz * seq
    dff = w_gate.shape[2]
    tm, tc = 512, dff // 2
    ng = len(C_PATTERNS)
    c_caches = ((cache_k_c0, cache_v_c0), (cache_k_c1, cache_v_c1), (cache_k_c2, cache_v_c2))

    lb_all = jnp.cumsum(jax.nn.softmax(lb_logits.astype(F32), axis=0), axis=0)
    bf = lambda w: w.astype(BF16)

    xp = x_prompt.reshape(n, d)
    xs = x_sample.reshape(nb, d)

    w_in = bf(w_in_ab[0])
    n_chunks = w_in.shape[1] // W_ATT
    gains = _head_gains([q_norm_a[0], k_norm_a[0]], n_chunks)
    w_out = bf(w_out_ab[0])
    lb = lb_all[0]

    pp = _norm_proj(xp, norm_mix[0], w_in, gains, 2, tm).reshape(bsz, seq, -1)
    qa, ka, va = _moba_prep(pp)
    oa = _moba_attn(qa, ka, va, 512).reshape(n, W_ATT)
    ob, hg_p = _hgrn_seq(pp, lb, g_norm_b[0], 3, 256)
    xp = _out_ab(oa, ob.reshape(n, -1), w_out, xp, tm)
    k_a_p = pp[:, :, W_ATT:2 * W_ATT].reshape(1, bsz, seq, HEADS, DH)
    v_a_p = pp[:, :, 2 * W_ATT:3 * W_ATT].reshape(1, bsz, seq, HEADS, DH)

    pos_minor = lambda c: jnp.transpose(c, (0, 2, 3, 1))
    pos_major = lambda c: jnp.transpose(c, (0, 3, 1, 2))

    ps = _norm_proj(xs, norm_mix[0], w_in, gains, 2, nb)
    ps_t = ps.T
    oas_t = _moba_dec(ps_t, pos_minor(cache_k_a[0]), pos_minor(cache_v_a[0]), page_table)
    obs, hg_s = _hgrn_step(ps, lb, g_norm_b[0], state_hgrn[0], 3)
    xs = _out_ab_t(oas_t, obs, w_out, xs)
    k_a_s = ps[:, W_ATT:2 * W_ATT].reshape(1, nb, 1, HEADS, DH)
    v_a_s = ps[:, 2 * W_ATT:3 * W_ATT].reshape(1, nb, 1, HEADS, DH)

    conv_p, conv_s = [], []

    def ffn(l, xp, xs):
        wg, wu, wd = bf(w_gate[l]), bf(w_up[l]), bf(w_down[l])
        xp, cp = _ffn_seq(xp, norm_ffn[l], wg, wu, conv_w[l], conv_b[l], wd, bsz, tm, tc)
        xs, a_s = _ffn_step(xs, norm_ffn[l], wg, wu, conv_w[l], conv_b[l], wd,
                            state_conv[l, :, 0, :], state_conv[l, :, 1, :], tc)
        conv_p.append(cp)
        conv_s.append(jnp.stack([state_conv[l, :, 1, :], a_s], axis=1))
        return xp, xs

    xp, xs = ffn(0, xp, xs)

    w_in = bf(w_in_c[0])
    n_chunks = w_in.shape[1] // W_ATT
    gains = _head_gains([q_norm_c[0]] * ng + [k_norm_c[0]] * ng, n_chunks)
    w_out = bf(w_out_c[0])

    pp = _norm_proj(xp, norm_mix[1], w_in, gains, 2 * ng, tm).reshape(bsz, seq, -1)
    os, ls = [], []
    for g, (win, dil) in enumerate(C_PATTERNS):
        o, lse = _band_attention(pp, g, dil, 512)
        os.append(o)
        ls.append(lse)
    xp = _out_c(os, ls, w_out, xp, tm)

    ps_t = _norm_proj(xs, norm_mix[1], w_in, gains, 2 * ng, nb).T
    c_out, os_t, ls_t = [], [], []
    for g, (win, dil) in enumerate(C_PATTERNS):
        keep = min(win, seq)
        kcol = slice((ng + g) * W_ATT, (ng + g + 1) * W_ATT)
        vcol = slice((2 * ng + g) * W_ATT, (2 * ng + g + 1) * W_ATT)
        c_out += [pp[:, seq - keep:, kcol].reshape(1, bsz, keep, HEADS, DH),
                  pp[:, seq - keep:, vcol].reshape(1, bsz, keep, HEADS, DH)]
        ck, cv = c_caches[g]
        heads_per_step = HEADS if ck.shape[2] <= 512 else HEADS // 2
        nk, nv, o_t, l_t = _dil_cache(ps_t, g, ng, dil, pos_minor(ck[0]), pos_minor(cv[0]),
                                      heads_per_step)
        c_out += [pos_major(nk)[None], pos_major(nv)[None]]
        os_t.append(o_t)
        ls_t.append(l_t)
    xs = _out_c_t(os_t, ls_t, w_out, xs)

    xp, xs = ffn(1, xp, xs)

    return (xp.reshape(bsz, seq, d), xs.reshape(nb, 1, d),
            k_a_p, v_a_p, k_a_s, v_a_s, hg_p[None], hg_s[None],
            *c_out, jnp.stack(conv_p), jnp.stack(conv_s))
```

```python
import functools

import jax
import jax.numpy as jnp
from jax import lax
from jax.experimental import pallas as pl
from jax.experimental.pallas import tpu as pltpu

F32 = jnp.float32
BF16 = jnp.bfloat16
HIGHEST = lax.Precision.HIGHEST

RMS_EPS = 1e-6
NEG_INF = -1e30
LOWEST = -3e38

LANES = 128
DH = 64
HEADS = 8
W_ATT = HEADS * DH
SCALE = DH ** -0.5
MOBA_BLOCK = 256
MOBA_TOPK = 3
H_B = 4
D_B = 128
HGRN_CHUNK = 32
BAND = 128
C_PATTERNS = ((128, 1), (512, 4), (2048, 16))
VMEM_LIMIT = 48 * 1024 * 1024

_CONTRACT_LAST = (((1,), (1,)), ((), ()))


def _params(*sem):
    return pltpu.CompilerParams(dimension_semantics=sem, vmem_limit_bytes=VMEM_LIMIT)


def _sigmoid(x):
    return 1.0 / (1.0 + jnp.exp(-x))


def _silu(x):
    return x * _sigmoid(x)


def _rms(x, g):
    return x * lax.rsqrt(jnp.mean(x * x, axis=-1, keepdims=True) + RMS_EPS) * g


def _dot(a, b):
    return jnp.dot(a, b, preferred_element_type=F32)


def _dot_t(a, b, precision=None):
    return lax.dot_general(a, b, _CONTRACT_LAST, precision=precision, preferred_element_type=F32)


def _norm_proj_kernel(x_ref, g_ref, w_ref, hg_ref, p_ref, o_ref, *, n_norm):
    h = _rms(x_ref[...], g_ref[...]).astype(BF16)
    for j in range(w_ref.shape[1] // W_ATT):
        cols = slice(j * W_ATT, (j + 1) * W_ATT)
        y = _dot(h, w_ref[:, cols])
        if j < n_norm:
            ms = _dot((y * y).astype(BF16), p_ref[...])
            y = y * lax.rsqrt(ms + RMS_EPS) * hg_ref[j:j + 1, :]
        o_ref[:, cols] = y


def _norm_proj(x, g, w_bf, head_gains, tm):
    n, d = x.shape
    m = w_bf.shape[1]
    n_norm = head_gains.shape[0]
    blockdiag = (jnp.arange(W_ATT)[:, None] // DH == jnp.arange(W_ATT)[None, :] // DH)
    p = (blockdiag.astype(F32) / DH).astype(BF16)
    return pl.pallas_call(
        functools.partial(_norm_proj_kernel, n_norm=n_norm),
        grid=(n // tm,),
        in_specs=[
            pl.BlockSpec((tm, d), lambda i: (i, 0)),
            pl.BlockSpec((1, d), lambda i: (0, 0)),
            pl.BlockSpec((d, m), lambda i: (0, 0)),
            pl.BlockSpec((n_norm, W_ATT), lambda i: (0, 0)),
            pl.BlockSpec((W_ATT, W_ATT), lambda i: (0, 0)),
        ],
        out_specs=pl.BlockSpec((tm, m), lambda i: (i, 0)),
        out_shape=jax.ShapeDtypeStruct((n, m), F32),
        compiler_params=_params("parallel"),
    )(x, g.reshape(1, d), w_bf, head_gains, p)


def _head_gains(gains):
    return jnp.stack([jnp.tile(gv.astype(F32), HEADS) for gv in gains])


def _ffn_seq_kernel(x_ref, g_ref, wg_ref, wu_ref, cw_ref, cb_ref, wd_ref, o_ref, cs_ref,
                    h_scr, a_scr, acc_scr, *, tm, blocks_per_seq):
    i = pl.program_id(0)
    c = pl.program_id(1)

    @pl.when(c == 0)
    def _():
        h_scr[...] = _rms(x_ref[...], g_ref[...]).astype(BF16)

    @pl.when(i % blocks_per_seq == 0)
    def _():
        a_scr[c, 0:8, :] = jnp.zeros((8, a_scr.shape[2]), F32)

    h = h_scr[...]
    a = _dot(h, wg_ref[...])
    u = _dot(h, wu_ref[...])
    a_scr[c, 8:8 + tm, :] = a
    a1 = a_scr[c, 7:7 + tm, :]
    a2 = a_scr[c, 6:6 + tm, :]
    cw = cw_ref[...]
    conv = cb_ref[...] + a2 * cw[0:1] + a1 * cw[1:2] + a * cw[2:3]
    y = (_silu(conv) * u).astype(BF16)
    part = _dot(y, wd_ref[...])

    @pl.when(c == 0)
    def _():
        acc_scr[...] = part

    @pl.when(c > 0)
    def _():
        acc_scr[...] = acc_scr[...] + part

    a_scr[c, 0:8, :] = a_scr[c, tm:tm + 8, :]
    tc = a_scr.shape[2]
    for cc in range(a_scr.shape[0]):
        @pl.when(c == cc)
        def _(cc=cc):
            cs_ref[0, :, cc * tc:(cc + 1) * tc] = a_scr[cc, tm + 6:tm + 8, :]

    @pl.when(c == pl.num_programs(1) - 1)
    def _():
        o_ref[...] = x_ref[...] + acc_scr[...]


def _ffn_seq(x, g, wg, wu, cw, cb, wd, n_seq, tm, tc):
    n, d = x.shape
    dff = wg.shape[1]
    nc = dff // tc
    bps = n // n_seq // tm
    return pl.pallas_call(
        functools.partial(_ffn_seq_kernel, tm=tm, blocks_per_seq=bps),
        grid=(n // tm, nc),
        in_specs=[
            pl.BlockSpec((tm, d), lambda i, c: (i, 0)),
            pl.BlockSpec((1, d), lambda i, c: (0, 0)),
            pl.BlockSpec((d, tc), lambda i, c: (0, c)),
            pl.BlockSpec((d, tc), lambda i, c: (0, c)),
            pl.BlockSpec((3, tc), lambda i, c: (0, c)),
            pl.BlockSpec((1, tc), lambda i, c: (0, c)),
            pl.BlockSpec((tc, d), lambda i, c: (c, 0)),
        ],
        out_specs=[
            pl.BlockSpec((tm, d), lambda i, c: (i, 0)),
            pl.BlockSpec((1, 2, dff), lambda i, c: (i // bps, 0, 0)),
        ],
        out_shape=[jax.ShapeDtypeStruct((n, d), F32),
                   jax.ShapeDtypeStruct((n_seq, 2, dff), F32)],
        scratch_shapes=[pltpu.VMEM((tm, d), BF16),
                        pltpu.VMEM((nc, tm + 8, tc), F32),
                        pltpu.VMEM((tm, d), F32)],
        compiler_params=_params("arbitrary", "arbitrary"),
    )(x, g.reshape(1, d), wg, wu, cw, cb.reshape(1, dff), wd)


def _ffn_step_kernel(x_ref, g_ref, wg_ref, wu_ref, cw_ref, cb_ref, wd_ref, s0_ref, s1_ref,
                     o_ref, a_ref, h_scr, acc_scr):
    c = pl.program_id(0)

    @pl.when(c == 0)
    def _():
        h_scr[...] = _rms(x_ref[...], g_ref[...]).astype(BF16)

    h = h_scr[...]
    a = _dot(h, wg_ref[...])
    u = _dot(h, wu_ref[...])
    a_ref[...] = a
    cw = cw_ref[...]
    conv = cb_ref[...] + s0_ref[...] * cw[0:1] + s1_ref[...] * cw[1:2] + a * cw[2:3]
    y = (_silu(conv) * u).astype(BF16)
    part = _dot(y, wd_ref[...])

    @pl.when(c == 0)
    def _():
        acc_scr[...] = part

    @pl.when(c > 0)
    def _():
        acc_scr[...] = acc_scr[...] + part

    @pl.when(c == pl.num_programs(0) - 1)
    def _():
        o_ref[...] = x_ref[...] + acc_scr[...]


def _ffn_step(x, g, wg, wu, cw, cb, wd, s0, s1, tc):
    n, d = x.shape
    dff = wg.shape[1]
    return pl.pallas_call(
        _ffn_step_kernel,
        grid=(dff // tc,),
        in_specs=[
            pl.BlockSpec((n, d), lambda c: (0, 0)),
            pl.BlockSpec((1, d), lambda c: (0, 0)),
            pl.BlockSpec((d, tc), lambda c: (0, c)),
            pl.BlockSpec((d, tc), lambda c: (0, c)),
            pl.BlockSpec((3, tc), lambda c: (0, c)),
            pl.BlockSpec((1, tc), lambda c: (0, c)),
            pl.BlockSpec((tc, d), lambda c: (c, 0)),
            pl.BlockSpec((n, tc), lambda c: (0, c)),
            pl.BlockSpec((n, tc), lambda c: (0, c)),
        ],
        out_specs=[pl.BlockSpec((n, d), lambda c: (0, 0)),
                   pl.BlockSpec((n, tc), lambda c: (0, c))],
        out_shape=[jax.ShapeDtypeStruct((n, d), F32), jax.ShapeDtypeStruct((n, dff), F32)],
        scratch_shapes=[pltpu.VMEM((n, d), BF16), pltpu.VMEM((n, d), F32)],
        compiler_params=_params("arbitrary"),
    )(x, g.reshape(1, d), wg, wu, cw, cb.reshape(1, dff), wd, s0, s1)


def _out_ab_kernel(oa_ref, ob_ref, w_ref, x_ref, o_ref):
    ka = oa_ref.shape[1]
    o_ref[...] = (x_ref[...] + _dot(oa_ref[...].astype(BF16), w_ref[0:ka, :])
                  + _dot(ob_ref[...].astype(BF16), w_ref[ka:, :]))


def _out_ab(oa, ob, w_bf, x, tm):
    n, d = x.shape
    ka, kb = oa.shape[1], ob.shape[1]
    return pl.pallas_call(
        _out_ab_kernel,
        grid=(n // tm,),
        in_specs=[pl.BlockSpec((tm, ka), lambda i: (i, 0)),
                  pl.BlockSpec((tm, kb), lambda i: (i, 0)),
                  pl.BlockSpec((ka + kb, d), lambda i: (0, 0)),
                  pl.BlockSpec((tm, d), lambda i: (i, 0))],
        out_specs=pl.BlockSpec((tm, d), lambda i: (i, 0)),
        out_shape=jax.ShapeDtypeStruct((n, d), F32),
        compiler_params=_params("parallel"),
    )(oa, ob, w_bf, x)


def _merge_groups(os, ls):
    m = jnp.maximum(jnp.maximum(ls[0], ls[1]), ls[2])
    es = [jnp.exp(l - m) for l in ls]
    den = es[0] + es[1] + es[2]
    return (es[0] / den) * os[0] + (es[1] / den) * os[1] + (es[2] / den) * os[2]


def _out_c_kernel(o0, o1, o2, l0, l1, l2, w_ref, x_ref, o_ref):
    merged = _merge_groups([o0[...], o1[...], o2[...]], [l0[...], l1[...], l2[...]])
    o_ref[...] = x_ref[...] + _dot(merged.astype(BF16), w_ref[...])


def _out_c(os, ls, w_bf, x, tm):
    n, d = x.shape
    k = w_bf.shape[0]
    act = pl.BlockSpec((tm, k), lambda i: (i, 0))
    return pl.pallas_call(
        _out_c_kernel,
        grid=(n // tm,),
        in_specs=[act] * 6 + [pl.BlockSpec((k, d), lambda i: (0, 0)),
                              pl.BlockSpec((tm, d), lambda i: (i, 0))],
        out_specs=pl.BlockSpec((tm, d), lambda i: (i, 0)),
        out_shape=jax.ShapeDtypeStruct((n, d), F32),
        compiler_params=_params("parallel"),
    )(*os, *ls, w_bf, x)


def _out_ab_t_kernel(oat_ref, ob_ref, w_ref, x_ref, o_ref):
    ka = oat_ref.shape[0]
    o_ref[...] = (x_ref[...] + _dot(oat_ref[...].T.astype(BF16), w_ref[0:ka, :])
                  + _dot(ob_ref[...].astype(BF16), w_ref[ka:, :]))


def _out_ab_t(oa_t, ob, w_bf, x):
    return pl.pallas_call(
        _out_ab_t_kernel,
        out_shape=jax.ShapeDtypeStruct(x.shape, F32),
        compiler_params=pltpu.CompilerParams(vmem_limit_bytes=VMEM_LIMIT),
    )(oa_t, ob, w_bf, x)


def _out_c_t_kernel(o0, o1, o2, l0, l1, l2, w_ref, x_ref, o_ref):
    merged = _merge_groups([o0[...], o1[...], o2[...]], [l0[...], l1[...], l2[...]])
    o_ref[...] = x_ref[...] + _dot(merged.T.astype(BF16), w_ref[...])


def _out_c_t(os_t, ls_t, w_bf, x):
    return pl.pallas_call(
        _out_c_t_kernel,
        out_shape=jax.ShapeDtypeStruct(x.shape, F32),
        compiler_params=pltpu.CompilerParams(vmem_limit_bytes=VMEM_LIMIT),
    )(*os_t, *ls_t, w_bf, x)


def _top_k_mask(g, lane, k):
    sel = jnp.zeros(g.shape, F32)
    lane_f = lane.astype(F32)
    for _ in range(k):
        m = jnp.max(g, axis=1, keepdims=True)
        idx = jnp.min(jnp.where(g == m, lane_f, float(g.shape[1])), axis=1, keepdims=True)
        hit = lane_f == idx
        sel = jnp.where(hit, 1.0, sel)
        g = jnp.where(hit, LOWEST, g)
    return sel


def _moba_prep_kernel(q_ref, k_ref, v_ref, qa_ref, ka_ref, va_ref, kmean_scr):
    i = pl.program_id(1)

    @pl.when(i == 0)
    def _():
        kmean_scr[...] = jnp.zeros(kmean_scr.shape, F32)

    rows = q_ref.shape[1]
    lane = lax.broadcasted_iota(jnp.int32, (rows, LANES), 1)
    low = lane < DH
    for hp in range(HEADS // 2):
        cols = slice(hp * LANES, (hp + 1) * LANES)
        q2 = q_ref[0, :, cols]
        k2 = k_ref[0, :, cols]
        v2 = v_ref[0, :, cols]
        km2 = kmean_scr[:, cols]
        for e in range(2):
            mine = low if e == 0 else jnp.logical_not(low)
            gate = _dot_t(jnp.where(mine, q2, 0.0), km2, precision=HIGHEST)
            past = lane < i
            sel = _top_k_mask(jnp.where(past, gate, NEG_INF), lane, MOBA_TOPK)
            keep = jnp.where(past, sel, 0.0) + jnp.where(lane == i, 1.0, 0.0)
            bias = jnp.where(keep > 0.5, 0.0, NEG_INF)
            onehot = jnp.where(lane == i, 1.0, 0.0)
            if e == 0:
                bias = pltpu.roll(bias, DH, axis=1)
                onehot = jnp.where(lane == i + DH, 1.0, 0.0)
                ones_col = jnp.where(lane == DH, 1.0, 0.0)
            else:
                ones_col = jnp.where(lane == 0, 1.0, 0.0)
            h = 2 * hp + e
            qa_ref[0, h] = jnp.where(mine, q2 * SCALE, bias).astype(BF16)
            ka_ref[0, h] = jnp.where(mine, k2, onehot).astype(BF16)
            va_ref[0, h] = jnp.where(mine, v2, ones_col).astype(BF16)

    kmean_scr[pl.ds(i, 1), :] = jnp.mean(k_ref[0], axis=0, keepdims=True)


def _moba_prep(proj, n_blocks_max=LANES):
    b, s, _ = proj.shape
    nb = s // MOBA_BLOCK
    assert nb <= n_blocks_max // 2 and nb <= DH // 2
    blk = lambda j: pl.BlockSpec((1, MOBA_BLOCK, W_ATT), lambda bi, i: (bi, i, j))
    aug = pl.BlockSpec((1, HEADS, MOBA_BLOCK, LANES), lambda bi, i: (bi, 0, i, 0))
    shape = jax.ShapeDtypeStruct((b, HEADS, s, LANES), BF16)
    return pl.pallas_call(
        _moba_prep_kernel,
        grid=(b, nb),
        in_specs=[blk(0), blk(1), blk(2)],
        out_specs=[aug, aug, aug],
        out_shape=[shape, shape, shape],
        scratch_shapes=[pltpu.VMEM((LANES, W_ATT), F32)],
        compiler_params=_params("arbitrary", "arbitrary"),
    )(proj, proj, proj)


def _moba_attn_kernel(q_ref, k_ref, v_ref, o_ref, m_scr, acc_scr, *, tq):
    qi = pl.program_id(2)
    tk = MOBA_BLOCK
    n_diag = tq // tk
    row = lax.broadcasted_iota(jnp.int32, (tq, tk), 0)
    col = lax.broadcasted_iota(jnp.int32, (tq, tk), 1)
    lane = lax.broadcasted_iota(jnp.int32, (tq, LANES), 1)
    hs = q_ref.shape[1]

    def step(n, causal_offset=None):
        st = pl.multiple_of(n * tk, tk)
        for e in range(hs):
            s = _dot_t(q_ref[0, e], k_ref[0, e, pl.ds(st, tk), :])
            if causal_offset is not None:
                s = jnp.where(col + causal_offset <= row, s, NEG_INF)
            m = m_scr[e]
            m_new = jnp.maximum(m, jnp.max(s, axis=1, keepdims=True))
            m_scr[e] = m_new
            p = jnp.exp(s - jnp.concatenate([m_new] * (tk // LANES), axis=1)).astype(BF16)
            acc_scr[e] = (jnp.exp(m - m_new) * acc_scr[e]
                          + _dot(p, v_ref[0, e, pl.ds(st, tk), :]))

    m_scr[...] = jnp.full(m_scr.shape, LOWEST, F32)
    acc_scr[...] = jnp.zeros(acc_scr.shape, F32)
    for j in range(n_diag):
        step(qi * n_diag + j, causal_offset=j * tk)

    def body(n, carry):
        step(n)
        return carry

    lax.fori_loop(0, qi * n_diag, body, 0)
    for e in range(0, hs, 2):
        even, odd = acc_scr[e], acc_scr[e + 1]
        o_ref[0, :, e // 2 * LANES:(e // 2 + 1) * LANES] = jnp.where(
            lane < DH, even / even[:, DH:DH + 1], odd / odd[:, 0:1])


def _moba_attn(qa, ka, va, tq, hs):
    b, h, s, _ = qa.shape
    return pl.pallas_call(
        functools.partial(_moba_attn_kernel, tq=tq),
        grid=(b, h // hs, s // tq),
        in_specs=[pl.BlockSpec((1, hs, tq, LANES), lambda bi, hp, i: (bi, hp, i, 0)),
                  pl.BlockSpec((1, hs, s, LANES), lambda bi, hp, i: (bi, hp, 0, 0),
                               pipeline_mode=pl.Buffered(1)),
                  pl.BlockSpec((1, hs, s, LANES), lambda bi, hp, i: (bi, hp, 0, 0),
                               pipeline_mode=pl.Buffered(1))],
        out_specs=pl.BlockSpec((1, tq, hs // 2 * LANES), lambda bi, hp, i: (bi, i, hp)),
        out_shape=jax.ShapeDtypeStruct((b, s, W_ATT), F32),
        scratch_shapes=[pltpu.VMEM((hs, tq, LANES), F32), pltpu.VMEM((hs, tq, LANES), F32)],
        compiler_params=_params("parallel", "parallel", "arbitrary"),
    )(qa, ka, va)


def _column(ref, rows, lane_is_seq):
    return jnp.sum(jnp.where(lane_is_seq, ref[rows, :], 0.0), axis=1, keepdims=True)


def _moba_dec_kernel(pt_ref, q_ref, kn_ref, vn_ref, *refs, n_pages, page):
    k_refs, v_refs, o_ref = refs[:n_pages], refs[n_pages:2 * n_pages], refs[2 * n_pages]
    bi = pl.program_id(0)
    pages_per_block = MOBA_BLOCK // page
    n_past = n_pages // pages_per_block

    @pl.when(bi == 0)
    def _():
        o_ref[...] = jnp.zeros(o_ref.shape, F32)

    lane_is_seq = lax.broadcasted_iota(jnp.int32, (DH, LANES), 1) == bi
    for h in range(HEADS):
        rows = slice(h * DH, (h + 1) * DH)
        qc = _column(q_ref, rows, lane_is_seq)
        kc = _column(kn_ref, rows, lane_is_seq)
        vc = _column(vn_ref, rows, lane_is_seq)
        raw = [jnp.sum(k_refs[j][0, h] * qc, axis=0, keepdims=True) for j in range(n_pages)]
        gates = []
        for n in range(n_past):
            tot = raw[n * pages_per_block]
            for j in range(1, pages_per_block):
                tot = tot + raw[n * pages_per_block + j]
            gates.append(jnp.sum(tot, axis=1, keepdims=True) * (1.0 / MOBA_BLOCK))
        sel = [jnp.zeros((1, 1), F32)] * n_past
        for _ in range(min(MOBA_TOPK, n_past + 1)):
            best = functools.reduce(jnp.maximum, gates)
            free = jnp.ones((1, 1), F32)
            for n in range(n_past):
                hit = jnp.where(gates[n] == best, free, 0.0)
                free = free - hit
                sel[n] = jnp.maximum(sel[n], hit)
                gates[n] = jnp.where(hit > 0.5, LOWEST, gates[n])
        s_self = jnp.sum(qc * kc, axis=0, keepdims=True) * SCALE
        s = [jnp.where(sel[j // pages_per_block] > 0.5, raw[j] * SCALE, NEG_INF)
             for j in range(n_pages)]
        m = s_self
        for sj in s:
            m = jnp.maximum(m, jnp.max(sj, axis=1, keepdims=True))
        p_self = jnp.exp(s_self - m)
        l = p_self
        acc = jnp.zeros((DH, page), F32)
        for j in range(n_pages):
            pj = jnp.exp(s[j] - m)
            l = l + jnp.sum(pj, axis=1, keepdims=True)
            acc = acc + v_refs[j][0, h] * pj
        o = (jnp.sum(acc, axis=1, keepdims=True) + p_self * vc) / l
        o_ref[rows, :] = jnp.where(lane_is_seq, o, o_ref[rows, :])


def _moba_dec(proj_t, cache_k, cache_v, page_table):
    b = proj_t.shape[1]
    n_pages = page_table.shape[1]
    page = cache_k.shape[3]
    assert b == LANES
    vec = lambda j: pl.BlockSpec((W_ATT, b), lambda bi, pt: (j, 0))
    pg = lambda j: pl.BlockSpec((1, HEADS, DH, page),
                                lambda bi, pt: (pt[bi * n_pages + j], 0, 0, 0))
    pages = [pg(j) for j in range(n_pages)]
    return pl.pallas_call(
        functools.partial(_moba_dec_kernel, n_pages=n_pages, page=page),
        grid_spec=pltpu.PrefetchScalarGridSpec(
            num_scalar_prefetch=1,
            grid=(b,),
            in_specs=[vec(0), vec(1), vec(2)] + pages + pages,
            out_specs=pl.BlockSpec((W_ATT, b), lambda bi, pt: (0, 0)),
        ),
        out_shape=jax.ShapeDtypeStruct((W_ATT, b), F32),
        compiler_params=_params("arbitrary"),
    )(page_table.reshape(-1), proj_t, proj_t, proj_t,
      *([cache_k] * n_pages), *([cache_v] * n_pages))


def _hgrn_seq_kernel(q_ref, f_ref, i_ref, g_ref, lb_ref, gn_ref, o_ref, s_ref, st_scr, *, tb):
    t = pl.program_id(1)

    @pl.when(t == 0)
    def _():
        st_scr[...] = jnp.zeros(st_scr.shape, F32)

    c = HGRN_CHUNK
    lb = lb_ref[...]
    gn = gn_ref[...]
    r_i = lax.broadcasted_iota(jnp.int32, (c, c), 0)
    c_i = lax.broadcasted_iota(jnp.int32, (c, c), 1)
    tril = c_i <= r_i
    ones_tril = jnp.where(tril, 1.0, 0.0)
    for ch in range(tb // c):
        rows = slice(ch * c, (ch + 1) * c)
        f = lb + (1.0 - lb) * _sigmoid(f_ref[0, rows, :])
        k = 1.0 - f
        q = _silu(q_ref[0, rows, :])
        v = i_ref[0, rows, :]
        b = jnp.dot(ones_tril, jnp.log(f), precision=HIGHEST, preferred_element_type=F32)
        b_mid = b[c // 2 - 1:c // 2, :]
        b_last = b[c - 1:c, :]
        q_mid = (q * jnp.exp(b - b_mid)).astype(BF16)
        k_mid = (k * jnp.exp(b_mid - b)).astype(BF16)
        k_end = (k * jnp.exp(b_last - b)).astype(BF16)
        q_in = (q * jnp.exp(b)).astype(BF16)
        decay = jnp.exp(b_last)
        vb = v.astype(BF16)
        outs = []
        for h in range(H_B):
            cols = slice(h * D_B, (h + 1) * D_B)
            a = jnp.where(tril, _dot_t(q_mid[:, cols], k_mid[:, cols]), 0.0)
            st = st_scr[h]
            o = _dot(a.astype(BF16), vb[:, cols]) + _dot_t(q_in[:, cols], st.astype(BF16))
            st_scr[h] = st * decay[:, cols] + _dot(v[:, cols].T.astype(BF16), k_end[:, cols])
            outs.append(_rms(o, gn[:, cols]))
        o_ref[0, rows, :] = jnp.concatenate(outs, axis=1) * _silu(g_ref[0, rows, :])

    @pl.when(t == pl.num_programs(1) - 1)
    def _():
        for h in range(H_B):
            s_ref[0, h] = st_scr[h].T


def _hgrn_seq(proj, lb, gn, first_chunk, tb):
    b, s, _ = proj.shape
    w = H_B * D_B
    blk = lambda j: pl.BlockSpec((1, tb, w), lambda bi, t: (bi, t, first_chunk + j))
    vec = pl.BlockSpec((1, w), lambda bi, t: (0, 0))
    return pl.pallas_call(
        functools.partial(_hgrn_seq_kernel, tb=tb),
        grid=(b, s // tb),
        in_specs=[blk(0), blk(1), blk(2), blk(3), vec, vec],
        out_specs=[pl.BlockSpec((1, tb, w), lambda bi, t: (bi, t, 0)),
                   pl.BlockSpec((1, H_B, D_B, D_B), lambda bi, t: (bi, 0, 0, 0))],
        out_shape=[jax.ShapeDtypeStruct((b, s, w), F32),
                   jax.ShapeDtypeStruct((b, H_B, D_B, D_B), F32)],
        scratch_shapes=[pltpu.VMEM((H_B, D_B, D_B), F32)],
        compiler_params=_params("parallel", "arbitrary"),
    )(proj, proj, proj, proj, lb.reshape(1, w), jnp.tile(gn.astype(F32), H_B).reshape(1, w))


def _hgrn_step_kernel(q_ref, f_ref, i_ref, g_ref, lb_ref, gn_ref, s0_ref, o_ref, s_ref):
    bi = pl.program_id(0)
    lb = lb_ref[...]
    gn = gn_ref[...]
    f = lb + (1.0 - lb) * _sigmoid(f_ref[pl.ds(bi, 1), :])
    q = _silu(q_ref[pl.ds(bi, 1), :])
    v = i_ref[pl.ds(bi, 1), :]
    eye = (lax.broadcasted_iota(jnp.int32, (D_B, D_B), 0)
           == lax.broadcasted_iota(jnp.int32, (D_B, D_B), 1))
    outs = []
    for h in range(H_B):
        cols = slice(h * D_B, (h + 1) * D_B)
        f_col = jnp.sum(jnp.where(eye, f[:, cols], 0.0), axis=1, keepdims=True)
        q_col = jnp.sum(jnp.where(eye, q[:, cols], 0.0), axis=1, keepdims=True)
        s_new = f_col * s0_ref[0, h] + (1.0 - f_col) * v[:, cols]
        s_ref[0, h] = s_new
        o = jnp.sum(q_col * s_new, axis=0, keepdims=True)
        outs.append(_rms(o, gn[:, cols]))
    o_ref[0] = jnp.concatenate(outs, axis=1) * _silu(g_ref[pl.ds(bi, 1), :])


def _hgrn_step(proj, lb, gn, s0, first_chunk):
    b = proj.shape[0]
    w = H_B * D_B
    blk = lambda j: pl.BlockSpec((b, w), lambda bi: (0, first_chunk + j))
    vec = pl.BlockSpec((1, w), lambda bi: (0, 0))
    st = pl.BlockSpec((1, H_B, D_B, D_B), lambda bi: (bi, 0, 0, 0))
    o, s = pl.pallas_call(
        _hgrn_step_kernel,
        grid=(b,),
        in_specs=[blk(0), blk(1), blk(2), blk(3), vec, vec, st],
        out_specs=[pl.BlockSpec((1, 1, w), lambda bi: (bi, 0, 0)), st],
        out_shape=[jax.ShapeDtypeStruct((b, 1, w), F32),
                   jax.ShapeDtypeStruct((b, H_B, D_B, D_B), F32)],
        compiler_params=_params("parallel"),
    )(proj, proj, proj, proj, lb.reshape(1, w),
      jnp.tile(gn.astype(F32), H_B).reshape(1, w), s0)
    return o.reshape(b, w), s


def _band_kernel(q_ref, kc_ref, kp_ref, vc_ref, vp_ref, o_ref, l_ref, *, tq):
    first = pl.program_id(1) == 0
    row = lax.broadcasted_iota(jnp.int32, (BAND, 2 * BAND), 0)
    col = lax.broadcasted_iota(jnp.int32, (BAND, 2 * BAND), 1)
    band = jnp.logical_and(col >= row, col <= row + BAND)
    lane = lax.broadcasted_iota(jnp.int32, (BAND, LANES), 1)
    low = lane < DH
    for i in range(tq // BAND):
        rows = slice(i * BAND, (i + 1) * BAND)
        if i == 0:
            k_prev, v_prev = kp_ref[0], vp_ref[0]
            mask = jnp.logical_and(band, jnp.logical_or(col >= BAND, jnp.logical_not(first)))
        else:
            prev = slice((i - 1) * BAND, i * BAND)
            k_prev, v_prev = kc_ref[0, prev, :], vc_ref[0, prev, :]
            mask = band
        kk = jnp.concatenate([k_prev, kc_ref[0, rows, :]], axis=0).astype(BF16)
        vv = jnp.concatenate([v_prev, vc_ref[0, rows, :]], axis=0).astype(BF16)
        for hp in range(HEADS // 2):
            cols = slice(hp * LANES, (hp + 1) * LANES)
            q2 = q_ref[0, rows, cols] * SCALE
            res = []
            for e in range(2):
                mine = low if e == 0 else jnp.logical_not(low)
                s = _dot_t(jnp.where(mine, q2, 0.0).astype(BF16), kk[:, cols])
                s = jnp.where(mask, s, NEG_INF)
                m = jnp.max(s, axis=1, keepdims=True)
                p = jnp.exp(s - m)
                l = jnp.sum(p, axis=1, keepdims=True)
                res.append((_dot((p / l).astype(BF16), vv[:, cols]), m + jnp.log(l)))
            o_ref[0, rows, cols] = jnp.where(low, res[0][0], res[1][0])
            l_ref[0, rows, cols] = jnp.where(low, res[0][1], res[1][1])


def _band_attention(proj, tq):
    b, ln, _ = proj.shape
    sub = tq // BAND
    cur = lambda comp: pl.BlockSpec((1, tq, W_ATT), lambda bi, l: (bi, l, comp))
    prev = lambda comp: pl.BlockSpec((1, BAND, W_ATT),
                                     lambda bi, l: (bi, jnp.maximum(l * sub - 1, 0), comp))
    out = pl.BlockSpec((1, tq, W_ATT), lambda bi, l: (bi, l, 0))
    shape = jax.ShapeDtypeStruct((b, ln, W_ATT), F32)
    return pl.pallas_call(
        functools.partial(_band_kernel, tq=tq),
        grid=(b, ln // tq),
        in_specs=[cur(0), cur(1), prev(1), cur(2), prev(2)],
        out_specs=[out, out],
        out_shape=[shape, shape],
        compiler_params=_params("parallel", "arbitrary"),
    )(proj, proj, proj, proj, proj)


def _dil_cache_kernel(q_ref, kn_ref, vn_ref, ck_ref, cv_ref, ok_ref, ov_ref, o_ref, l_ref,
                      *, dil, hb):
    bi = pl.program_id(0)
    hg = pl.program_id(1)
    lw = ck_ref.shape[3]

    @pl.when(jnp.logical_and(bi == 0, hg == 0))
    def _():
        o_ref[...] = jnp.zeros(o_ref.shape, F32)
        l_ref[...] = jnp.zeros(l_ref.shape, F32)

    nr = hb * DH
    rows = pl.ds(pl.multiple_of(hg * nr, nr), nr)
    lane_is_seq = lax.broadcasted_iota(jnp.int32, (nr, LANES), 1) == bi
    pos = lax.broadcasted_iota(jnp.int32, (hb, lw), 1)
    on_grid = (pos & (dil - 1)) == 0
    newest = lax.broadcasted_iota(jnp.int32, (DH, lw), 1) == lw - 1
    qc = _column(q_ref, rows, lane_is_seq) * SCALE
    kc = _column(kn_ref, rows, lane_is_seq)
    vc = _column(vn_ref, rows, lane_is_seq)
    head = lambda x, hh: x[hh * DH:(hh + 1) * DH]
    s = jnp.concatenate([jnp.sum(ck_ref[0, hh] * head(qc, hh), axis=0, keepdims=True)
                         for hh in range(hb)], axis=0)
    s = jnp.where(on_grid, s, NEG_INF)
    qk = qc * kc
    s_self = jnp.concatenate([jnp.sum(head(qk, hh), axis=0, keepdims=True)
                              for hh in range(hb)], axis=0)
    m = jnp.maximum(jnp.max(s, axis=1, keepdims=True), s_self)
    p = jnp.exp(s - m)
    p_self = jnp.exp(s_self - m)
    l = jnp.sum(p, axis=1, keepdims=True) + p_self
    pn = p / l
    w_self = p_self / l
    lse = m + jnp.log(l)
    os, ls = [], []
    for hh in range(hb):
        k = ck_ref[0, hh]
        v = cv_ref[0, hh]
        os.append(jnp.sum(v * pn[hh:hh + 1, :], axis=1, keepdims=True)
                  + w_self[hh:hh + 1, :] * head(vc, hh))
        ls.append(jnp.broadcast_to(lse[hh:hh + 1, :], (DH, 1)))
        ok_ref[0, hh] = jnp.where(newest, head(kc, hh), pltpu.roll(k, lw - 1, axis=1))
        ov_ref[0, hh] = jnp.where(newest, head(vc, hh), pltpu.roll(v, lw - 1, axis=1))
    o_ref[rows, :] = jnp.where(lane_is_seq, jnp.concatenate(os, axis=0), o_ref[rows, :])
    l_ref[rows, :] = jnp.where(lane_is_seq, jnp.concatenate(ls, axis=0), l_ref[rows, :])


def _dil_cache(proj_t, g, n_groups, dil, cache_k, cache_v, hb):
    b, h, dh, lw = cache_k.shape
    assert b == LANES and dil & (dil - 1) == 0
    vec = lambda j: pl.BlockSpec((W_ATT, b), lambda bi, hg: (j, 0))
    cache = pl.BlockSpec((1, hb, dh, lw), lambda bi, hg: (bi, hg, 0, 0))
    res = pl.BlockSpec((W_ATT, b), lambda bi, hg: (0, 0))
    return pl.pallas_call(
        functools.partial(_dil_cache_kernel, dil=dil, hb=hb),
        grid=(b, h // hb),
        in_specs=[vec(g), vec(n_groups + g), vec(2 * n_groups + g), cache, cache],
        out_specs=[cache, cache, res, res],
        out_shape=[jax.ShapeDtypeStruct(cache_k.shape, F32), jax.ShapeDtypeStruct(cache_v.shape, F32),
                   jax.ShapeDtypeStruct((W_ATT, b), F32), jax.ShapeDtypeStruct((W_ATT, b), F32)],
        compiler_params=_params("arbitrary", "arbitrary"),
    )(proj_t, proj_t, proj_t, cache_k, cache_v)


def kernel(x_prompt, x_sample, cache_k_a, cache_v_a, state_hgrn, cache_k_c0, cache_v_c0, cache_k_c1, cache_v_c1, cache_k_c2, cache_v_c2, state_conv, page_table, norm_mix, norm_ffn, w_in_ab, q_norm_a, k_norm_a, lb_logits, g_norm_b, w_out_ab, w_in_c, q_norm_c, k_norm_c, w_out_c, w_gate, w_up, conv_w, conv_b, w_down):
    bsz, seq, d = x_prompt.shape
    nb = x_sample.shape[0]
    n = bsz * seq
    dff = w_gate.shape[2]
    tm, tc = 512, dff // 2
    ng = len(C_PATTERNS)
    c_caches = ((cache_k_c0, cache_v_c0), (cache_k_c1, cache_v_c1), (cache_k_c2, cache_v_c2))

    lb_all = jnp.cumsum(jax.nn.softmax(lb_logits.astype(F32), axis=0), axis=0)
    bf = lambda w: w.astype(BF16)

    xp = x_prompt.reshape(n, d)
    xs = x_sample.reshape(nb, d)

    w_in = bf(w_in_ab[0])
    gains = _head_gains([q_norm_a[0], k_norm_a[0]])
    w_out = bf(w_out_ab[0])
    lb = lb_all[0]

    pp = _norm_proj(xp, norm_mix[0], w_in, gains, tm).reshape(bsz, seq, -1)
    qa, ka, va = _moba_prep(pp)
    oa = _moba_attn(qa, ka, va, 512, HEADS).reshape(n, W_ATT)
    ob, hg_p = _hgrn_seq(pp, lb, g_norm_b[0], 3, 256)
    xp = _out_ab(oa, ob.reshape(n, -1), w_out, xp, tm)
    k_a_p = pp[:, :, W_ATT:2 * W_ATT].reshape(1, bsz, seq, HEADS, DH)
    v_a_p = pp[:, :, 2 * W_ATT:3 * W_ATT].reshape(1, bsz, seq, HEADS, DH)

    pos_minor = lambda c: jnp.transpose(c, (0, 2, 3, 1))
    pos_major = lambda c: jnp.transpose(c, (0, 3, 1, 2))

    ps = _norm_proj(xs, norm_mix[0], w_in, gains, nb)
    ps_t = ps.T
    oas_t = _moba_dec(ps_t, pos_minor(cache_k_a[0]), pos_minor(cache_v_a[0]), page_table)
    obs, hg_s = _hgrn_step(ps, lb, g_norm_b[0], state_hgrn[0], 3)
    xs = _out_ab_t(oas_t, obs, w_out, xs)
    k_a_s = ps[:, W_ATT:2 * W_ATT].reshape(1, nb, 1, HEADS, DH)
    v_a_s = ps[:, 2 * W_ATT:3 * W_ATT].reshape(1, nb, 1, HEADS, DH)

    conv_p, conv_s = [], []

    def ffn(l, xp, xs):
        wg, wu, wd = bf(w_gate[l]), bf(w_up[l]), bf(w_down[l])
        xp, cp = _ffn_seq(xp, norm_ffn[l], wg, wu, conv_w[l], conv_b[l], wd, bsz, tm, tc)
        xs, a_s = _ffn_step(xs, norm_ffn[l], wg, wu, conv_w[l], conv_b[l], wd,
                            state_conv[l, :, 0, :], state_conv[l, :, 1, :], tc)
        conv_p.append(cp)
        conv_s.append(jnp.stack([state_conv[l, :, 1, :], a_s], axis=1))
        return xp, xs

    xp, xs = ffn(0, xp, xs)

    w_in = bf(w_in_c[0])
    w_out = bf(w_out_c[0])
    gains = _head_gains([q_norm_c[0], k_norm_c[0]])

    os, ls, c_prompt = [], [], []
    for g, (win, dil) in enumerate(C_PATTERNS):
        ln = seq // dil
        w_g = jnp.concatenate([w_in[:, (c * ng + g) * W_ATT:(c * ng + g + 1) * W_ATT]
                               for c in range(3)], axis=1)
        xr = xp.reshape(bsz, ln, dil, d).transpose(0, 2, 1, 3).reshape(n, d)
        pg = _norm_proj(xr, norm_mix[1], w_g, gains, tm)
        o, lse = _band_attention(pg.reshape(bsz * dil, ln, 3 * W_ATT), 512)
        by_pos = lambda t: t.reshape(bsz, dil, ln, W_ATT).transpose(0, 2, 1, 3).reshape(n, W_ATT)
        os.append(by_pos(o))
        ls.append(by_pos(lse))
        keep = min(win, seq)
        tail = pg.reshape(bsz, dil, ln, 3, HEADS, DH)[:, :, ln - keep // dil:, 1:]
        tail = tail.transpose(3, 0, 2, 1, 4, 5).reshape(2, 1, bsz, keep, HEADS, DH)
        c_prompt.append((tail[0], tail[1]))
    xp = _out_c(os, ls, w_out, xp, tm)

    gains = _head_gains([q_norm_c[0]] * ng + [k_norm_c[0]] * ng)
    ps_t = _norm_proj(xs, norm_mix[1], w_in, gains, nb).T
    c_out, os_t, ls_t = [], [], []
    for g, (win, dil) in enumerate(C_PATTERNS):
        c_out += list(c_prompt[g])
        ck, cv = c_caches[g]
        heads_per_step = HEADS if ck.shape[2] <= 512 else HEADS // 2
        nk, nv, o_t, l_t = _dil_cache(ps_t, g, ng, dil, pos_minor(ck[0]), pos_minor(cv[0]),
                                      heads_per_step)
        c_out += [pos_major(nk)[None], pos_major(nv)[None]]
        os_t.append(o_t)
        ls_t.append(l_t)
    xs = _out_c_t(os_t, ls_t, w_out, xs)

    xp, xs = ffn(1, xp, xs)

    return (xp.reshape(bsz, seq, d), xs.reshape(nb, 1, d),
            k_a_p, v_a_p, k_a_s, v_a_s, hg_p[None], hg_s[None],
            *c_out, jnp.stack(conv_p), jnp.stack(conv_s))
```

```python
import functools

import jax
import jax.numpy as jnp
from jax import lax
from jax.experimental import pallas as pl
from jax.experimental.pallas import tpu as pltpu

F32 = jnp.float32
BF16 = jnp.bfloat16
HIGHEST = lax.Precision.HIGHEST

RMS_EPS = 1e-6
NEG_INF = -1e30
LOWEST = -3e38

LANES = 128
DH = 64
HEADS = 8
W_ATT = HEADS * DH
SCALE = DH ** -0.5
MOBA_BLOCK = 256
MOBA_TOPK = 3
H_B = 4
D_B = 128
HGRN_CHUNK = 32
BAND = 128
C_PATTERNS = ((128, 1), (512, 4), (2048, 16))
VMEM_LIMIT = 48 * 1024 * 1024

_CONTRACT_LAST = (((1,), (1,)), ((), ()))


def _params(*sem):
    return pltpu.CompilerParams(dimension_semantics=sem, vmem_limit_bytes=VMEM_LIMIT)


def _sigmoid(x):
    return 1.0 / (1.0 + jnp.exp(-x))


def _silu(x):
    return x * _sigmoid(x)


def _rms(x, g):
    return x * lax.rsqrt(jnp.mean(x * x, axis=-1, keepdims=True) + RMS_EPS) * g


def _dot(a, b):
    return jnp.dot(a, b, preferred_element_type=F32)


def _dot_t(a, b, precision=None):
    return lax.dot_general(a, b, _CONTRACT_LAST, precision=precision, preferred_element_type=F32)


def _norm_proj_kernel(x_ref, g_ref, w_ref, hg_ref, p_ref, o_ref, *, n_norm):
    h = _rms(x_ref[...], g_ref[...]).astype(BF16)
    for j in range(w_ref.shape[1] // W_ATT):
        cols = slice(j * W_ATT, (j + 1) * W_ATT)
        y = _dot(h, w_ref[:, cols])
        if j < n_norm:
            ms = _dot((y * y).astype(BF16), p_ref[...])
            y = y * lax.rsqrt(ms + RMS_EPS) * hg_ref[j:j + 1, :]
        o_ref[:, cols] = y


def _norm_proj(x, g, w_bf, head_gains, tm):
    n, d = x.shape
    m = w_bf.shape[1]
    n_norm = head_gains.shape[0]
    blockdiag = (jnp.arange(W_ATT)[:, None] // DH == jnp.arange(W_ATT)[None, :] // DH)
    p = (blockdiag.astype(F32) / DH).astype(BF16)
    return pl.pallas_call(
        functools.partial(_norm_proj_kernel, n_norm=n_norm),
        grid=(n // tm,),
        in_specs=[
            pl.BlockSpec((tm, d), lambda i: (i, 0)),
            pl.BlockSpec((1, d), lambda i: (0, 0)),
            pl.BlockSpec((d, m), lambda i: (0, 0)),
            pl.BlockSpec((n_norm, W_ATT), lambda i: (0, 0)),
            pl.BlockSpec((W_ATT, W_ATT), lambda i: (0, 0)),
        ],
        out_specs=pl.BlockSpec((tm, m), lambda i: (i, 0)),
        out_shape=jax.ShapeDtypeStruct((n, m), F32),
        compiler_params=_params("parallel"),
    )(x, g.reshape(1, d), w_bf, head_gains, p)


def _head_gains(gains):
    return jnp.stack([jnp.tile(gv.astype(F32), HEADS) for gv in gains])


def _ffn_seq_kernel(x_ref, g_ref, wg_ref, wu_ref, cw_ref, cb_ref, wd_ref, o_ref, cs_ref,
                    h_scr, a_scr, acc_scr, *, tm, blocks_per_seq):
    i = pl.program_id(0)
    c = pl.program_id(1)

    @pl.when(c == 0)
    def _():
        h_scr[...] = _rms(x_ref[...], g_ref[...]).astype(BF16)

    @pl.when(i % blocks_per_seq == 0)
    def _():
        a_scr[c, 0:8, :] = jnp.zeros((8, a_scr.shape[2]), F32)

    h = h_scr[...]
    a = _dot(h, wg_ref[...])
    u = _dot(h, wu_ref[...])
    a_scr[c, 8:8 + tm, :] = a
    a1 = a_scr[c, 7:7 + tm, :]
    a2 = a_scr[c, 6:6 + tm, :]
    cw = cw_ref[...]
    conv = cb_ref[...] + a2 * cw[0:1] + a1 * cw[1:2] + a * cw[2:3]
    y = (_silu(conv) * u).astype(BF16)
    part = _dot(y, wd_ref[...])

    @pl.when(c == 0)
    def _():
        acc_scr[...] = part

    @pl.when(c > 0)
    def _():
        acc_scr[...] = acc_scr[...] + part

    a_scr[c, 0:8, :] = a_scr[c, tm:tm + 8, :]
    tc = a_scr.shape[2]
    for cc in range(a_scr.shape[0]):
        @pl.when(c == cc)
        def _(cc=cc):
            cs_ref[0, :, cc * tc:(cc + 1) * tc] = a_scr[cc, tm + 6:tm + 8, :]

    @pl.when(c == pl.num_programs(1) - 1)
    def _():
        o_ref[...] = x_ref[...] + acc_scr[...]


def _ffn_seq(x, g, wg, wu, cw, cb, wd, n_seq, tm, tc):
    n, d = x.shape
    dff = wg.shape[1]
    nc = dff // tc
    bps = n // n_seq // tm
    return pl.pallas_call(
        functools.partial(_ffn_seq_kernel, tm=tm, blocks_per_seq=bps),
        grid=(n // tm, nc),
        in_specs=[
            pl.BlockSpec((tm, d), lambda i, c: (i, 0)),
            pl.BlockSpec((1, d), lambda i, c: (0, 0)),
            pl.BlockSpec((d, tc), lambda i, c: (0, c)),
            pl.BlockSpec((d, tc), lambda i, c: (0, c)),
            pl.BlockSpec((3, tc), lambda i, c: (0, c)),
            pl.BlockSpec((1, tc), lambda i, c: (0, c)),
            pl.BlockSpec((tc, d), lambda i, c: (c, 0)),
        ],
        out_specs=[
            pl.BlockSpec((tm, d), lambda i, c: (i, 0)),
            pl.BlockSpec((1, 2, dff), lambda i, c: (i // bps, 0, 0)),
        ],
        out_shape=[jax.ShapeDtypeStruct((n, d), F32),
                   jax.ShapeDtypeStruct((n_seq, 2, dff), F32)],
        scratch_shapes=[pltpu.VMEM((tm, d), BF16),
                        pltpu.VMEM((nc, tm + 8, tc), F32),
                        pltpu.VMEM((tm, d), F32)],
        compiler_params=_params("arbitrary", "arbitrary"),
    )(x, g.reshape(1, d), wg, wu, cw, cb.reshape(1, dff), wd)


def _ffn_step_kernel(x_ref, g_ref, wg_ref, wu_ref, cw_ref, cb_ref, wd_ref, s0_ref, s1_ref,
                     o_ref, a_ref, h_scr, acc_scr):
    c = pl.program_id(0)

    @pl.when(c == 0)
    def _():
        h_scr[...] = _rms(x_ref[...], g_ref[...]).astype(BF16)

    h = h_scr[...]
    a = _dot(h, wg_ref[...])
    u = _dot(h, wu_ref[...])
    a_ref[...] = a
    cw = cw_ref[...]
    conv = cb_ref[...] + s0_ref[...] * cw[0:1] + s1_ref[...] * cw[1:2] + a * cw[2:3]
    y = (_silu(conv) * u).astype(BF16)
    part = _dot(y, wd_ref[...])

    @pl.when(c == 0)
    def _():
        acc_scr[...] = part

    @pl.when(c > 0)
    def _():
        acc_scr[...] = acc_scr[...] + part

    @pl.when(c == pl.num_programs(0) - 1)
    def _():
        o_ref[...] = x_ref[...] + acc_scr[...]


def _ffn_step(x, g, wg, wu, cw, cb, wd, s0, s1, tc):
    n, d = x.shape
    dff = wg.shape[1]
    return pl.pallas_call(
        _ffn_step_kernel,
        grid=(dff // tc,),
        in_specs=[
            pl.BlockSpec((n, d), lambda c: (0, 0)),
            pl.BlockSpec((1, d), lambda c: (0, 0)),
            pl.BlockSpec((d, tc), lambda c: (0, c)),
            pl.BlockSpec((d, tc), lambda c: (0, c)),
            pl.BlockSpec((3, tc), lambda c: (0, c)),
            pl.BlockSpec((1, tc), lambda c: (0, c)),
            pl.BlockSpec((tc, d), lambda c: (c, 0)),
            pl.BlockSpec((n, tc), lambda c: (0, c)),
            pl.BlockSpec((n, tc), lambda c: (0, c)),
        ],
        out_specs=[pl.BlockSpec((n, d), lambda c: (0, 0)),
                   pl.BlockSpec((n, tc), lambda c: (0, c))],
        out_shape=[jax.ShapeDtypeStruct((n, d), F32), jax.ShapeDtypeStruct((n, dff), F32)],
        scratch_shapes=[pltpu.VMEM((n, d), BF16), pltpu.VMEM((n, d), F32)],
        compiler_params=_params("arbitrary"),
    )(x, g.reshape(1, d), wg, wu, cw, cb.reshape(1, dff), wd, s0, s1)


def _out_ab_kernel(oa_ref, ob_ref, w_ref, x_ref, o_ref):
    ka = oa_ref.shape[1]
    o_ref[...] = (x_ref[...] + _dot(oa_ref[...].astype(BF16), w_ref[0:ka, :])
                  + _dot(ob_ref[...].astype(BF16), w_ref[ka:, :]))


def _out_ab(oa, ob, w_bf, x, tm):
    n, d = x.shape
    ka, kb = oa.shape[1], ob.shape[1]
    return pl.pallas_call(
        _out_ab_kernel,
        grid=(n // tm,),
        in_specs=[pl.BlockSpec((tm, ka), lambda i: (i, 0)),
                  pl.BlockSpec((tm, kb), lambda i: (i, 0)),
                  pl.BlockSpec((ka + kb, d), lambda i: (0, 0)),
                  pl.BlockSpec((tm, d), lambda i: (i, 0))],
        out_specs=pl.BlockSpec((tm, d), lambda i: (i, 0)),
        out_shape=jax.ShapeDtypeStruct((n, d), F32),
        compiler_params=_params("parallel"),
    )(oa, ob, w_bf, x)


def _merge_groups(os, ls):
    m = jnp.maximum(jnp.maximum(ls[0], ls[1]), ls[2])
    es = [jnp.exp(l - m) for l in ls]
    den = es[0] + es[1] + es[2]
    return (es[0] / den) * os[0] + (es[1] / den) * os[1] + (es[2] / den) * os[2]


def _out_c_kernel(o0, o1, o2, l0, l1, l2, w_ref, x_ref, o_ref):
    merged = jnp.concatenate(
        [_merge_groups([o0[0, hp], o1[0, hp], o2[0, hp]], [l0[0, hp], l1[0, hp], l2[0, hp]])
         for hp in range(HEADS // 2)], axis=1)
    o_ref[...] = x_ref[...] + _dot(merged.astype(BF16), w_ref[...])


def _out_c(os, ls, w_bf, x, tm):
    n, d = x.shape
    k = w_bf.shape[0]
    per_seq = os[0].shape[2] // tm
    act = pl.BlockSpec((1, HEADS // 2, tm, LANES), lambda i: (i // per_seq, 0, i % per_seq, 0))
    return pl.pallas_call(
        _out_c_kernel,
        grid=(n // tm,),
        in_specs=[act] * 6 + [pl.BlockSpec((k, d), lambda i: (0, 0)),
                              pl.BlockSpec((tm, d), lambda i: (i, 0))],
        out_specs=pl.BlockSpec((tm, d), lambda i: (i, 0)),
        out_shape=jax.ShapeDtypeStruct((n, d), F32),
        compiler_params=_params("parallel"),
    )(*os, *ls, w_bf, x)


def _out_ab_t_kernel(oat_ref, ob_ref, w_ref, x_ref, o_ref):
    ka = oat_ref.shape[0]
    o_ref[...] = (x_ref[...] + _dot(oat_ref[...].T.astype(BF16), w_ref[0:ka, :])
                  + _dot(ob_ref[...].astype(BF16), w_ref[ka:, :]))


def _out_ab_t(oa_t, ob, w_bf, x):
    return pl.pallas_call(
        _out_ab_t_kernel,
        out_shape=jax.ShapeDtypeStruct(x.shape, F32),
        compiler_params=pltpu.CompilerParams(vmem_limit_bytes=VMEM_LIMIT),
    )(oa_t, ob, w_bf, x)


def _out_c_t_kernel(o0, o1, o2, l0, l1, l2, w_ref, x_ref, o_ref):
    merged = _merge_groups([o0[...], o1[...], o2[...]], [l0[...], l1[...], l2[...]])
    o_ref[...] = x_ref[...] + _dot(merged.T.astype(BF16), w_ref[...])


def _out_c_t(os_t, ls_t, w_bf, x):
    return pl.pallas_call(
        _out_c_t_kernel,
        out_shape=jax.ShapeDtypeStruct(x.shape, F32),
        compiler_params=pltpu.CompilerParams(vmem_limit_bytes=VMEM_LIMIT),
    )(*os_t, *ls_t, w_bf, x)


def _top_k_mask(g, lane, k):
    sel = jnp.zeros(g.shape, F32)
    lane_f = lane.astype(F32)
    for _ in range(k):
        m = jnp.max(g, axis=1, keepdims=True)
        idx = jnp.min(jnp.where(g == m, lane_f, float(g.shape[1])), axis=1, keepdims=True)
        hit = lane_f == idx
        sel = jnp.where(hit, 1.0, sel)
        g = jnp.where(hit, LOWEST, g)
    return sel


def _moba_prep_kernel(q_ref, k_ref, v_ref, qa_ref, ka_ref, va_ref, kmean_scr):
    i = pl.program_id(1)

    @pl.when(i == 0)
    def _():
        kmean_scr[...] = jnp.zeros(kmean_scr.shape, F32)

    rows = q_ref.shape[1]
    lane = lax.broadcasted_iota(jnp.int32, (rows, LANES), 1)
    low = lane < DH
    for hp in range(HEADS // 2):
        cols = slice(hp * LANES, (hp + 1) * LANES)
        q2 = q_ref[0, :, cols]
        k2 = k_ref[0, :, cols]
        v2 = v_ref[0, :, cols]
        km2 = kmean_scr[:, cols]
        for e in range(2):
            mine = low if e == 0 else jnp.logical_not(low)
            gate = _dot_t(jnp.where(mine, q2, 0.0), km2, precision=HIGHEST)
            past = lane < i
            sel = _top_k_mask(jnp.where(past, gate, NEG_INF), lane, MOBA_TOPK)
            keep = jnp.where(past, sel, 0.0) + jnp.where(lane == i, 1.0, 0.0)
            bias = jnp.where(keep > 0.5, 0.0, NEG_INF)
            onehot = jnp.where(lane == i, 1.0, 0.0)
            if e == 0:
                bias = pltpu.roll(bias, DH, axis=1)
                onehot = jnp.where(lane == i + DH, 1.0, 0.0)
                ones_col = jnp.where(lane == DH, 1.0, 0.0)
            else:
                ones_col = jnp.where(lane == 0, 1.0, 0.0)
            h = 2 * hp + e
            qa_ref[0, h] = jnp.where(mine, q2 * SCALE, bias).astype(BF16)
            ka_ref[0, h] = jnp.where(mine, k2, onehot).astype(BF16)
            va_ref[0, h] = jnp.where(mine, v2, ones_col).astype(BF16)

    kmean_scr[pl.ds(i, 1), :] = jnp.mean(k_ref[0], axis=0, keepdims=True)


def _moba_prep(proj, n_blocks_max=LANES):
    b, s, _ = proj.shape
    nb = s // MOBA_BLOCK
    assert nb <= n_blocks_max // 2 and nb <= DH // 2
    blk = lambda j: pl.BlockSpec((1, MOBA_BLOCK, W_ATT), lambda bi, i: (bi, i, j))
    aug = pl.BlockSpec((1, HEADS, MOBA_BLOCK, LANES), lambda bi, i: (bi, 0, i, 0))
    shape = jax.ShapeDtypeStruct((b, HEADS, s, LANES), BF16)
    return pl.pallas_call(
        _moba_prep_kernel,
        grid=(b, nb),
        in_specs=[blk(0), blk(1), blk(2)],
        out_specs=[aug, aug, aug],
        out_shape=[shape, shape, shape],
        scratch_shapes=[pltpu.VMEM((LANES, W_ATT), F32)],
        compiler_params=_params("arbitrary", "arbitrary"),
    )(proj, proj, proj)


def _moba_attn_kernel(q_ref, k_ref, v_ref, o_ref, m_scr, acc_scr, *, tq):
    qi = pl.program_id(2)
    tk = MOBA_BLOCK
    n_diag = tq // tk
    row = lax.broadcasted_iota(jnp.int32, (tq, tk), 0)
    col = lax.broadcasted_iota(jnp.int32, (tq, tk), 1)
    lane = lax.broadcasted_iota(jnp.int32, (tq, LANES), 1)
    hs = q_ref.shape[1]

    def step(n, causal_offset=None):
        st = pl.multiple_of(n * tk, tk)
        for e in range(hs):
            s = _dot_t(q_ref[0, e], k_ref[0, e, pl.ds(st, tk), :])
            if causal_offset is not None:
                s = jnp.where(col + causal_offset <= row, s, NEG_INF)
            m = m_scr[e]
            m_new = jnp.maximum(m, jnp.max(s, axis=1, keepdims=True))
            m_scr[e] = m_new
            p = jnp.exp(s - jnp.concatenate([m_new] * (tk // LANES), axis=1)).astype(BF16)
            acc_scr[e] = (jnp.exp(m - m_new) * acc_scr[e]
                          + _dot(p, v_ref[0, e, pl.ds(st, tk), :]))

    m_scr[...] = jnp.full(m_scr.shape, LOWEST, F32)
    acc_scr[...] = jnp.zeros(acc_scr.shape, F32)
    for j in range(n_diag):
        step(qi * n_diag + j, causal_offset=j * tk)

    def body(n, carry):
        step(n)
        return carry

    lax.fori_loop(0, qi * n_diag, body, 0)
    for e in range(0, hs, 2):
        even, odd = acc_scr[e], acc_scr[e + 1]
        o_ref[0, :, e // 2 * LANES:(e // 2 + 1) * LANES] = jnp.where(
            lane < DH, even / even[:, DH:DH + 1], odd / odd[:, 0:1])


def _moba_attn(qa, ka, va, tq, hs):
    b, h, s, _ = qa.shape
    return pl.pallas_call(
        functools.partial(_moba_attn_kernel, tq=tq),
        grid=(b, h // hs, s // tq),
        in_specs=[pl.BlockSpec((1, hs, tq, LANES), lambda bi, hp, i: (bi, hp, i, 0)),
                  pl.BlockSpec((1, hs, s, LANES), lambda bi, hp, i: (bi, hp, 0, 0),
                               pipeline_mode=pl.Buffered(1)),
                  pl.BlockSpec((1, hs, s, LANES), lambda bi, hp, i: (bi, hp, 0, 0),
                               pipeline_mode=pl.Buffered(1))],
        out_specs=pl.BlockSpec((1, tq, hs // 2 * LANES), lambda bi, hp, i: (bi, i, hp)),
        out_shape=jax.ShapeDtypeStruct((b, s, W_ATT), F32),
        scratch_shapes=[pltpu.VMEM((hs, tq, LANES), F32), pltpu.VMEM((hs, tq, LANES), F32)],
        compiler_params=_params("parallel", "parallel", "arbitrary"),
    )(qa, ka, va)


def _column(ref, rows, lane_is_seq):
    return jnp.sum(jnp.where(lane_is_seq, ref[rows, :], 0.0), axis=1, keepdims=True)


def _moba_dec_kernel(pt_ref, q_ref, kn_ref, vn_ref, *refs, n_pages, page):
    k_refs, v_refs, o_ref = refs[:n_pages], refs[n_pages:2 * n_pages], refs[2 * n_pages]
    bi = pl.program_id(0)
    pages_per_block = MOBA_BLOCK // page
    n_past = n_pages // pages_per_block

    @pl.when(bi == 0)
    def _():
        o_ref[...] = jnp.zeros(o_ref.shape, F32)

    lane_is_seq = lax.broadcasted_iota(jnp.int32, (DH, LANES), 1) == bi
    for h in range(HEADS):
        rows = slice(h * DH, (h + 1) * DH)
        qc = _column(q_ref, rows, lane_is_seq)
        kc = _column(kn_ref, rows, lane_is_seq)
        vc = _column(vn_ref, rows, lane_is_seq)
        raw = [jnp.sum(k_refs[j][0, h] * qc, axis=0, keepdims=True) for j in range(n_pages)]
        gates = []
        for n in range(n_past):
            tot = raw[n * pages_per_block]
            for j in range(1, pages_per_block):
                tot = tot + raw[n * pages_per_block + j]
            gates.append(jnp.sum(tot, axis=1, keepdims=True) * (1.0 / MOBA_BLOCK))
        sel = [jnp.zeros((1, 1), F32)] * n_past
        for _ in range(min(MOBA_TOPK, n_past + 1)):
            best = functools.reduce(jnp.maximum, gates)
            free = jnp.ones((1, 1), F32)
            for n in range(n_past):
                hit = jnp.where(gates[n] == best, free, 0.0)
                free = free - hit
                sel[n] = jnp.maximum(sel[n], hit)
                gates[n] = jnp.where(hit > 0.5, LOWEST, gates[n])
        s_self = jnp.sum(qc * kc, axis=0, keepdims=True) * SCALE
        s = [jnp.where(sel[j // pages_per_block] > 0.5, raw[j] * SCALE, NEG_INF)
             for j in range(n_pages)]
        m = s_self
        for sj in s:
            m = jnp.maximum(m, jnp.max(sj, axis=1, keepdims=True))
        p_self = jnp.exp(s_self - m)
        l = p_self
        acc = jnp.zeros((DH, page), F32)
        for j in range(n_pages):
            pj = jnp.exp(s[j] - m)
            l = l + jnp.sum(pj, axis=1, keepdims=True)
            acc = acc + v_refs[j][0, h] * pj
        o = (jnp.sum(acc, axis=1, keepdims=True) + p_self * vc) / l
        o_ref[rows, :] = jnp.where(lane_is_seq, o, o_ref[rows, :])


def _moba_dec(proj_t, cache_k, cache_v, page_table):
    b = proj_t.shape[1]
    n_pages = page_table.shape[1]
    page = cache_k.shape[3]
    assert b == LANES
    vec = lambda j: pl.BlockSpec((W_ATT, b), lambda bi, pt: (j, 0))
    pg = lambda j: pl.BlockSpec((1, HEADS, DH, page),
                                lambda bi, pt: (pt[bi * n_pages + j], 0, 0, 0))
    pages = [pg(j) for j in range(n_pages)]
    return pl.pallas_call(
        functools.partial(_moba_dec_kernel, n_pages=n_pages, page=page),
        grid_spec=pltpu.PrefetchScalarGridSpec(
            num_scalar_prefetch=1,
            grid=(b,),
            in_specs=[vec(0), vec(1), vec(2)] + pages + pages,
            out_specs=pl.BlockSpec((W_ATT, b), lambda bi, pt: (0, 0)),
        ),
        out_shape=jax.ShapeDtypeStruct((W_ATT, b), F32),
        compiler_params=_params("arbitrary"),
    )(page_table.reshape(-1), proj_t, proj_t, proj_t,
      *([cache_k] * n_pages), *([cache_v] * n_pages))


def _hgrn_seq_kernel(q_ref, f_ref, i_ref, g_ref, lb_ref, gn_ref, o_ref, s_ref, st_scr, *, tb):
    t = pl.program_id(1)

    @pl.when(t == 0)
    def _():
        st_scr[...] = jnp.zeros(st_scr.shape, F32)

    c = HGRN_CHUNK
    lb = lb_ref[...]
    gn = gn_ref[...]
    r_i = lax.broadcasted_iota(jnp.int32, (c, c), 0)
    c_i = lax.broadcasted_iota(jnp.int32, (c, c), 1)
    tril = c_i <= r_i
    ones_tril = jnp.where(tril, 1.0, 0.0)
    for ch in range(tb // c):
        rows = slice(ch * c, (ch + 1) * c)
        f = lb + (1.0 - lb) * _sigmoid(f_ref[0, rows, :])
        k = 1.0 - f
        q = _silu(q_ref[0, rows, :])
        v = i_ref[0, rows, :]
        b = jnp.dot(ones_tril, jnp.log(f), precision=HIGHEST, preferred_element_type=F32)
        b_mid = b[c // 2 - 1:c // 2, :]
        b_last = b[c - 1:c, :]
        q_mid = (q * jnp.exp(b - b_mid)).astype(BF16)
        k_mid = (k * jnp.exp(b_mid - b)).astype(BF16)
        k_end = (k * jnp.exp(b_last - b)).astype(BF16)
        q_in = (q * jnp.exp(b)).astype(BF16)
        decay = jnp.exp(b_last)
        vb = v.astype(BF16)
        outs = []
        for h in range(H_B):
            cols = slice(h * D_B, (h + 1) * D_B)
            a = jnp.where(tril, _dot_t(q_mid[:, cols], k_mid[:, cols]), 0.0)
            st = st_scr[h]
            o = _dot(a.astype(BF16), vb[:, cols]) + _dot_t(q_in[:, cols], st.astype(BF16))
            st_scr[h] = st * decay[:, cols] + _dot(v[:, cols].T.astype(BF16), k_end[:, cols])
            outs.append(_rms(o, gn[:, cols]))
        o_ref[0, rows, :] = jnp.concatenate(outs, axis=1) * _silu(g_ref[0, rows, :])

    @pl.when(t == pl.num_programs(1) - 1)
    def _():
        for h in range(H_B):
            s_ref[0, h] = st_scr[h].T


def _hgrn_seq(proj, lb, gn, first_chunk, tb):
    b, s, _ = proj.shape
    w = H_B * D_B
    blk = lambda j: pl.BlockSpec((1, tb, w), lambda bi, t: (bi, t, first_chunk + j))
    vec = pl.BlockSpec((1, w), lambda bi, t: (0, 0))
    return pl.pallas_call(
        functools.partial(_hgrn_seq_kernel, tb=tb),
        grid=(b, s // tb),
        in_specs=[blk(0), blk(1), blk(2), blk(3), vec, vec],
        out_specs=[pl.BlockSpec((1, tb, w), lambda bi, t: (bi, t, 0)),
                   pl.BlockSpec((1, H_B, D_B, D_B), lambda bi, t: (bi, 0, 0, 0))],
        out_shape=[jax.ShapeDtypeStruct((b, s, w), F32),
                   jax.ShapeDtypeStruct((b, H_B, D_B, D_B), F32)],
        scratch_shapes=[pltpu.VMEM((H_B, D_B, D_B), F32)],
        compiler_params=_params("parallel", "arbitrary"),
    )(proj, proj, proj, proj, lb.reshape(1, w), jnp.tile(gn.astype(F32), H_B).reshape(1, w))


def _hgrn_step_kernel(q_ref, f_ref, i_ref, g_ref, lb_ref, gn_ref, s0_ref, o_ref, s_ref):
    bi = pl.program_id(0)
    lb = lb_ref[...]
    gn = gn_ref[...]
    f = lb + (1.0 - lb) * _sigmoid(f_ref[pl.ds(bi, 1), :])
    q = _silu(q_ref[pl.ds(bi, 1), :])
    v = i_ref[pl.ds(bi, 1), :]
    eye = (lax.broadcasted_iota(jnp.int32, (D_B, D_B), 0)
           == lax.broadcasted_iota(jnp.int32, (D_B, D_B), 1))
    outs = []
    for h in range(H_B):
        cols = slice(h * D_B, (h + 1) * D_B)
        f_col = jnp.sum(jnp.where(eye, f[:, cols], 0.0), axis=1, keepdims=True)
        q_col = jnp.sum(jnp.where(eye, q[:, cols], 0.0), axis=1, keepdims=True)
        s_new = f_col * s0_ref[0, h] + (1.0 - f_col) * v[:, cols]
        s_ref[0, h] = s_new
        o = jnp.sum(q_col * s_new, axis=0, keepdims=True)
        outs.append(_rms(o, gn[:, cols]))
    o_ref[0] = jnp.concatenate(outs, axis=1) * _silu(g_ref[pl.ds(bi, 1), :])


def _hgrn_step(proj, lb, gn, s0, first_chunk):
    b = proj.shape[0]
    w = H_B * D_B
    blk = lambda j: pl.BlockSpec((b, w), lambda bi: (0, first_chunk + j))
    vec = pl.BlockSpec((1, w), lambda bi: (0, 0))
    st = pl.BlockSpec((1, H_B, D_B, D_B), lambda bi: (bi, 0, 0, 0))
    o, s = pl.pallas_call(
        _hgrn_step_kernel,
        grid=(b,),
        in_specs=[blk(0), blk(1), blk(2), blk(3), vec, vec, st],
        out_specs=[pl.BlockSpec((1, 1, w), lambda bi: (bi, 0, 0)), st],
        out_shape=[jax.ShapeDtypeStruct((b, 1, w), F32),
                   jax.ShapeDtypeStruct((b, H_B, D_B, D_B), F32)],
        compiler_params=_params("parallel"),
    )(proj, proj, proj, proj, lb.reshape(1, w),
      jnp.tile(gn.astype(F32), H_B).reshape(1, w), s0)
    return o.reshape(b, w), s


def _band_kernel(q_ref, kc_ref, kp_ref, vc_ref, vp_ref, o_ref, l_ref, *, tq, dil):
    first = pl.program_id(1) == 0
    r = pl.program_id(2)
    row = lax.broadcasted_iota(jnp.int32, (BAND, 2 * BAND), 0)
    col = lax.broadcasted_iota(jnp.int32, (BAND, 2 * BAND), 1)
    band = jnp.logical_and(col >= row, col <= row + BAND)
    lane = lax.broadcasted_iota(jnp.int32, (BAND, LANES), 1)
    low = lane < DH
    for i in range(tq // BAND):
        rows = slice(i * BAND, (i + 1) * BAND)
        if i == 0:
            k_prev, v_prev = kp_ref[0, 0], vp_ref[0, 0]
            mask = jnp.logical_and(band, jnp.logical_or(col >= BAND, jnp.logical_not(first)))
        else:
            prev = slice((i - 1) * BAND, i * BAND)
            k_prev, v_prev = kc_ref[0, 0, prev, :], vc_ref[0, 0, prev, :]
            mask = band
        kk = jnp.concatenate([k_prev, kc_ref[0, 0, rows, :]], axis=0).astype(BF16)
        vv = jnp.concatenate([v_prev, vc_ref[0, 0, rows, :]], axis=0).astype(BF16)
        out_rows = rows if dil == 1 else pl.ds(r + i * BAND * dil, BAND, stride=dil)
        for hp in range(HEADS // 2):
            cols = slice(hp * LANES, (hp + 1) * LANES)
            q2 = q_ref[0, 0, rows, cols] * SCALE
            res = []
            for e in range(2):
                mine = low if e == 0 else jnp.logical_not(low)
                s = _dot_t(jnp.where(mine, q2, 0.0).astype(BF16), kk[:, cols])
                s = jnp.where(mask, s, NEG_INF)
                m = jnp.max(s, axis=1, keepdims=True)
                p = jnp.exp(s - m)
                l = jnp.sum(p, axis=1, keepdims=True)
                res.append((_dot((p / l).astype(BF16), vv[:, cols]), m + jnp.log(l)))
            o_ref[0, hp, out_rows, :] = jnp.where(low, res[0][0], res[1][0])
            l_ref[0, hp, out_rows, :] = jnp.where(low, res[0][1], res[1][1])


def _band_attention(proj, tq):
    b, dil, ln, _ = proj.shape
    sub = tq // BAND
    cur = lambda comp: pl.BlockSpec((1, 1, tq, W_ATT), lambda bi, l, r: (bi, r, l, comp))
    prev = lambda comp: pl.BlockSpec((1, 1, BAND, W_ATT),
                                     lambda bi, l, r: (bi, r, jnp.maximum(l * sub - 1, 0), comp))
    out = pl.BlockSpec((1, HEADS // 2, tq * dil, LANES), lambda bi, l, r: (bi, 0, l, 0))
    shape = jax.ShapeDtypeStruct((b, HEADS // 2, ln * dil, LANES), F32)
    return pl.pallas_call(
        functools.partial(_band_kernel, tq=tq, dil=dil),
        grid=(b, ln // tq, dil),
        in_specs=[cur(0), cur(1), prev(1), cur(2), prev(2)],
        out_specs=[out, out],
        out_shape=[shape, shape],
        compiler_params=_params("parallel", "arbitrary", "arbitrary"),
    )(proj, proj, proj, proj, proj)


def _dil_cache_kernel(*refs, dils, hb):
    ng = len(dils)
    for g, dil in enumerate(dils):
        ins = refs[5 * g:5 * g + 5]
        outs = refs[5 * ng + 4 * g:5 * ng + 4 * g + 4]
        _dil_cache_group(*ins, *outs, dil=dil, hb=hb)


def _dil_cache_group(q_ref, kn_ref, vn_ref, ck_ref, cv_ref, ok_ref, ov_ref, o_ref, l_ref,
                     *, dil, hb):
    bi = pl.program_id(0)
    hg = pl.program_id(1)
    lw = ck_ref.shape[3]

    @pl.when(jnp.logical_and(bi == 0, hg == 0))
    def _():
        o_ref[...] = jnp.zeros(o_ref.shape, F32)
        l_ref[...] = jnp.zeros(l_ref.shape, F32)

    nr = hb * DH
    rows = pl.ds(pl.multiple_of(hg * nr, nr), nr)
    lane_is_seq = lax.broadcasted_iota(jnp.int32, (nr, LANES), 1) == bi
    pos = lax.broadcasted_iota(jnp.int32, (hb, lw), 1)
    on_grid = (pos & (dil - 1)) == 0
    newest = lax.broadcasted_iota(jnp.int32, (DH, lw), 1) == lw - 1
    qc = _column(q_ref, rows, lane_is_seq) * SCALE
    kc = _column(kn_ref, rows, lane_is_seq)
    vc = _column(vn_ref, rows, lane_is_seq)
    head = lambda x, hh: x[hh * DH:(hh + 1) * DH]
    s = jnp.concatenate([jnp.sum(ck_ref[0, hh] * head(qc, hh), axis=0, keepdims=True)
                         for hh in range(hb)], axis=0)
    s = jnp.where(on_grid, s, NEG_INF)
    qk = qc * kc
    s_self = jnp.concatenate([jnp.sum(head(qk, hh), axis=0, keepdims=True)
                              for hh in range(hb)], axis=0)
    m = jnp.maximum(jnp.max(s, axis=1, keepdims=True), s_self)
    p = jnp.exp(s - m)
    p_self = jnp.exp(s_self - m)
    l = jnp.sum(p, axis=1, keepdims=True) + p_self
    pn = p / l
    w_self = p_self / l
    lse = m + jnp.log(l)
    os, ls = [], []
    for hh in range(hb):
        k = ck_ref[0, hh]
        v = cv_ref[0, hh]
        os.append(jnp.sum(v * pn[hh:hh + 1, :], axis=1, keepdims=True)
                  + w_self[hh:hh + 1, :] * head(vc, hh))
        ls.append(jnp.broadcast_to(lse[hh:hh + 1, :], (DH, 1)))
        ok_ref[0, hh] = jnp.where(newest, head(kc, hh), pltpu.roll(k, lw - 1, axis=1))
        ov_ref[0, hh] = jnp.where(newest, head(vc, hh), pltpu.roll(v, lw - 1, axis=1))
    o_ref[rows, :] = jnp.where(lane_is_seq, jnp.concatenate(os, axis=0), o_ref[rows, :])
    l_ref[rows, :] = jnp.where(lane_is_seq, jnp.concatenate(ls, axis=0), l_ref[rows, :])


def _dil_cache(proj_t, dils, caches, hb):
    ng = len(dils)
    b, h, dh, _ = caches[0][0].shape
    assert b == LANES and all(dil & (dil - 1) == 0 for dil in dils)
    vec = lambda j: pl.BlockSpec((W_ATT, b), lambda bi, hg: (j, 0))
    res = pl.BlockSpec((W_ATT, b), lambda bi, hg: (0, 0))
    res_shape = jax.ShapeDtypeStruct((W_ATT, b), F32)
    in_specs, out_specs, out_shape, args = [], [], [], []
    for g, (ck, cv) in enumerate(caches):
        cache = pl.BlockSpec((1, hb, dh, ck.shape[3]), lambda bi, hg: (bi, hg, 0, 0))
        in_specs += [vec(g), vec(ng + g), vec(2 * ng + g), cache, cache]
        args += [proj_t, proj_t, proj_t, ck, cv]
        out_specs += [cache, cache, res, res]
        out_shape += [jax.ShapeDtypeStruct(ck.shape, F32), jax.ShapeDtypeStruct(cv.shape, F32),
                      res_shape, res_shape]
    outs = pl.pallas_call(
        functools.partial(_dil_cache_kernel, dils=tuple(dils), hb=hb),
        grid=(b, h // hb),
        in_specs=in_specs,
        out_specs=out_specs,
        out_shape=out_shape,
        compiler_params=_params("arbitrary", "arbitrary"),
    )(*args)
    return [outs[4 * g:4 * g + 4] for g in range(ng)]


def kernel(x_prompt, x_sample, cache_k_a, cache_v_a, state_hgrn, cache_k_c0, cache_v_c0, cache_k_c1, cache_v_c1, cache_k_c2, cache_v_c2, state_conv, page_table, norm_mix, norm_ffn, w_in_ab, q_norm_a, k_norm_a, lb_logits, g_norm_b, w_out_ab, w_in_c, q_norm_c, k_norm_c, w_out_c, w_gate, w_up, conv_w, conv_b, w_down):
    bsz, seq, d = x_prompt.shape
    nb = x_sample.shape[0]
    n = bsz * seq
    dff = w_gate.shape[2]
    tm, tc = 512, dff // 2
    ng = len(C_PATTERNS)
    c_caches = ((cache_k_c0, cache_v_c0), (cache_k_c1, cache_v_c1), (cache_k_c2, cache_v_c2))

    lb_all = jnp.cumsum(jax.nn.softmax(lb_logits.astype(F32), axis=0), axis=0)
    bf = lambda w: w.astype(BF16)

    xp = x_prompt.reshape(n, d)
    xs = x_sample.reshape(nb, d)

    w_in = bf(w_in_ab[0])
    gains = _head_gains([q_norm_a[0], k_norm_a[0]])
    w_out = bf(w_out_ab[0])
    lb = lb_all[0]

    pp = _norm_proj(xp, norm_mix[0], w_in, gains, tm).reshape(bsz, seq, -1)
    qa, ka, va = _moba_prep(pp)
    oa = _moba_attn(qa, ka, va, 512, HEADS).reshape(n, W_ATT)
    ob, hg_p = _hgrn_seq(pp, lb, g_norm_b[0], 3, 256)
    xp = _out_ab(oa, ob.reshape(n, -1), w_out, xp, tm)
    k_a_p = pp[:, :, W_ATT:2 * W_ATT].reshape(1, bsz, seq, HEADS, DH)
    v_a_p = pp[:, :, 2 * W_ATT:3 * W_ATT].reshape(1, bsz, seq, HEADS, DH)

    pos_minor = lambda c: jnp.transpose(c, (0, 2, 3, 1))
    pos_major = lambda c: jnp.transpose(c, (0, 3, 1, 2))

    ps = _norm_proj(xs, norm_mix[0], w_in, gains, nb)
    ps_t = ps.T
    oas_t = _moba_dec(ps_t, pos_minor(cache_k_a[0]), pos_minor(cache_v_a[0]), page_table)
    obs, hg_s = _hgrn_step(ps, lb, g_norm_b[0], state_hgrn[0], 3)
    xs = _out_ab_t(oas_t, obs, w_out, xs)
    k_a_s = ps[:, W_ATT:2 * W_ATT].reshape(1, nb, 1, HEADS, DH)
    v_a_s = ps[:, 2 * W_ATT:3 * W_ATT].reshape(1, nb, 1, HEADS, DH)

    conv_p, conv_s = [], []

    def ffn(l, xp, xs):
        wg, wu, wd = bf(w_gate[l]), bf(w_up[l]), bf(w_down[l])
        xp, cp = _ffn_seq(xp, norm_ffn[l], wg, wu, conv_w[l], conv_b[l], wd, bsz, tm, tc)
        xs, a_s = _ffn_step(xs, norm_ffn[l], wg, wu, conv_w[l], conv_b[l], wd,
                            state_conv[l, :, 0, :], state_conv[l, :, 1, :], tc)
        conv_p.append(cp)
        conv_s.append(jnp.stack([state_conv[l, :, 1, :], a_s], axis=1))
        return xp, xs

    xp, xs = ffn(0, xp, xs)

    w_in = bf(w_in_c[0])
    w_out = bf(w_out_c[0])
    gains = _head_gains([q_norm_c[0], k_norm_c[0]])

    os, ls, c_prompt = [], [], []
    for g, (win, dil) in enumerate(C_PATTERNS):
        ln = seq // dil
        w_g = jnp.concatenate([w_in[:, (c * ng + g) * W_ATT:(c * ng + g + 1) * W_ATT]
                               for c in range(3)], axis=1)
        xr = xp.reshape(bsz, ln, dil, d).transpose(0, 2, 1, 3).reshape(n, d)
        pg = _norm_proj(xr, norm_mix[1], w_g, gains, tm)
        o, lse = _band_attention(pg.reshape(bsz, dil, ln, 3 * W_ATT), min(512, 2048 // dil))
        os.append(o)
        ls.append(lse)
        keep = min(win, seq)
        kd = keep // dil
        tail = lax.optimization_barrier(lax.slice(
            pg.reshape(bsz * dil, ln, 3 * W_ATT), (0, ln - kd, W_ATT), (bsz * dil, ln, 3 * W_ATT)))
        tail = tail.reshape(bsz, dil, kd, 2, HEADS, DH).transpose(3, 0, 2, 1, 4, 5)
        tail = tail.reshape(2, 1, bsz, keep, HEADS, DH)
        c_prompt.append((tail[0], tail[1]))
    xp = _out_c(os, ls, w_out, xp, tm)

    gains = _head_gains([q_norm_c[0]] * ng + [k_norm_c[0]] * ng)
    ps_t = _norm_proj(xs, norm_mix[1], w_in, gains, nb).T
    decoded = _dil_cache(ps_t, [dil for _, dil in C_PATTERNS],
                         [(pos_minor(ck[0]), pos_minor(cv[0])) for ck, cv in c_caches], HEADS // 2)
    c_out, os_t, ls_t = [], [], []
    for g, (nk, nv, o_t, l_t) in enumerate(decoded):
        c_out += list(c_prompt[g])
        c_out += [pos_major(nk)[None], pos_major(nv)[None]]
        os_t.append(o_t)
        ls_t.append(l_t)
    xs = _out_c_t(os_t, ls_t, w_out, xs)

    xp, xs = ffn(1, xp, xs)

    return (xp.reshape(bsz, seq, d), xs.reshape(nb, 1, d),
            k_a_p, v_a_p, k_a_s, v_a_s, hg_p[None], hg_s[None],
            *c_out, jnp.stack(conv_p), jnp.stack(conv_s))
```

```python
import functools

import jax
import jax.numpy as jnp
from jax import lax
from jax.experimental import pallas as pl
from jax.experimental.pallas import tpu as pltpu

F32 = jnp.float32
BF16 = jnp.bfloat16
HIGHEST = lax.Precision.HIGHEST

RMS_EPS = 1e-6
NEG_INF = -1e30
LOWEST = -3e38

LANES = 128
DH = 64
HEADS = 8
W_ATT = HEADS * DH
SCALE = DH ** -0.5
MOBA_BLOCK = 256
MOBA_TOPK = 3
H_B = 4
D_B = 128
HGRN_CHUNK = 32
BAND = 128
C_PATTERNS = ((128, 1), (512, 4), (2048, 16))
VMEM_LIMIT = 48 * 1024 * 1024
VMEM_LIMIT_HOSTED = 56 * 1024 * 1024

_CONTRACT_LAST = (((1,), (1,)), ((), ()))


def _params(*sem, vmem=VMEM_LIMIT):
    return pltpu.CompilerParams(dimension_semantics=sem, vmem_limit_bytes=vmem)


def _sigmoid(x):
    return 1.0 / (1.0 + jnp.exp(-x))


def _silu(x):
    return x * _sigmoid(x)


def _rms(x, g):
    return x * lax.rsqrt(jnp.mean(x * x, axis=-1, keepdims=True) + RMS_EPS) * g


def _dot(a, b):
    return jnp.dot(a, b, preferred_element_type=F32)


def _dot_t(a, b, precision=None):
    return lax.dot_general(a, b, _CONTRACT_LAST, precision=precision, preferred_element_type=F32)


def _norm_proj_kernel(x_ref, g_ref, w_ref, hg_ref, p_ref, o_ref, *, n_norm):
    h = _rms(x_ref[...], g_ref[...]).astype(BF16)
    for j in range(w_ref.shape[1] // W_ATT):
        cols = slice(j * W_ATT, (j + 1) * W_ATT)
        y = _dot(h, w_ref[:, cols])
        if j < n_norm:
            ms = _dot((y * y).astype(BF16), p_ref[...])
            y = y * lax.rsqrt(ms + RMS_EPS) * hg_ref[j:j + 1, :]
        o_ref[:, cols] = y


def _norm_proj(x, g, w_bf, head_gains, tm):
    n, d = x.shape
    m = w_bf.shape[1]
    n_norm = head_gains.shape[0]
    blockdiag = (jnp.arange(W_ATT)[:, None] // DH == jnp.arange(W_ATT)[None, :] // DH)
    p = (blockdiag.astype(F32) / DH).astype(BF16)
    return pl.pallas_call(
        functools.partial(_norm_proj_kernel, n_norm=n_norm),
        grid=(n // tm,),
        in_specs=[
            pl.BlockSpec((tm, d), lambda i: (i, 0)),
            pl.BlockSpec((1, d), lambda i: (0, 0)),
            pl.BlockSpec((d, m), lambda i: (0, 0)),
            pl.BlockSpec((n_norm, W_ATT), lambda i: (0, 0)),
            pl.BlockSpec((W_ATT, W_ATT), lambda i: (0, 0)),
        ],
        out_specs=pl.BlockSpec((tm, m), lambda i: (i, 0)),
        out_shape=jax.ShapeDtypeStruct((n, m), F32),
        compiler_params=_params("parallel"),
    )(x, g.reshape(1, d), w_bf, head_gains, p)


def _head_gains(gains):
    return jnp.stack([jnp.tile(gv.astype(F32), HEADS) for gv in gains])


class _Guest:
    def __init__(self, in_specs, args, out_specs, out_shape, init, body, prefetch=None):
        self.in_specs, self.args = in_specs, args
        self.out_specs, self.out_shape = out_specs, out_shape
        self.init, self.body, self.prefetch = init, body, prefetch


def _ffn_seq_kernel(*refs, tm, blocks_per_seq, guest):
    refs = list(refs)
    pt_ref = refs.pop(0) if guest is not None and guest.prefetch is not None else None
    n_gi = len(guest.in_specs) if guest is not None else 0
    n_go = len(guest.out_specs) if guest is not None else 0
    x_ref, g_ref, wg_ref, wu_ref, cw_ref, cb_ref, wd_ref = refs[:7]
    guest_in = refs[7:7 + n_gi]
    o_ref, cs_ref = refs[7 + n_gi:9 + n_gi]
    guest_out = refs[9 + n_gi:9 + n_gi + n_go]
    a_scr, acc_scr = refs[9 + n_gi + n_go:]
    i = pl.program_id(0)
    c = pl.program_id(1)
    step = i * pl.num_programs(1) + c

    @pl.when(i % blocks_per_seq == 0)
    def _():
        a_scr[c, 0:8, :] = jnp.zeros((8, a_scr.shape[2]), F32)

    @pl.when(c == 0)
    def _():
        acc_scr[...] = jnp.zeros(acc_scr.shape, F32)

    if guest is not None:
        @pl.when(step == 0)
        def _():
            guest.init(guest_out)

    h = _rms(x_ref[...], g_ref[...]).astype(BF16)
    a = _dot(h, wg_ref[c])
    u = _dot(h, wu_ref[c])
    a_scr[c, 8:8 + tm, :] = a
    a1 = a_scr[c, 7:7 + tm, :]
    a2 = a_scr[c, 6:6 + tm, :]
    cw = cw_ref[c]
    conv = cb_ref[c] + a2 * cw[0:1] + a1 * cw[1:2] + a * cw[2:3]
    y = (_silu(conv) * u).astype(BF16)
    acc = acc_scr[...] + _dot(y, wd_ref[c])
    acc_scr[...] = acc
    a_scr[c, 0:8, :] = a_scr[c, tm:tm + 8, :]
    cs_ref[0, c] = a_scr[c, tm + 6:tm + 8, :]
    o_ref[...] = x_ref[...] + acc
    if guest is not None:
        guest.body(step, pt_ref, guest_in, guest_out)


def _ffn_seq(x, g, wg, wu, cw, cb, wd, n_seq, tm, nc, guest=None):
    n, d = x.shape
    dff = wg.shape[1]
    tc = dff // nc
    bps = n // n_seq // tm
    chunked = lambda w: w.reshape(w.shape[0], nc, tc).transpose(1, 0, 2)
    resident = lambda shape: pl.BlockSpec(shape, lambda i, c, *_: (0,) * len(shape),
                                          pipeline_mode=pl.Buffered(1))
    in_specs = [
        pl.BlockSpec((tm, d), lambda i, c, *_: (i, 0)),
        resident((1, d)),
        resident((nc, d, tc)), resident((nc, d, tc)), resident((nc, 3, tc)), resident((nc, 1, tc)),
        resident((nc, tc, d)),
    ]
    args = [x, g.reshape(1, d), chunked(wg), chunked(wu), chunked(cw),
            chunked(cb.reshape(1, dff)), wd.reshape(nc, tc, d)]
    out_specs = [pl.BlockSpec((tm, d), lambda i, c, *_: (i, 0)),
                 pl.BlockSpec((1, nc, 2, tc), lambda i, c, *_: (i // bps, 0, 0, 0))]
    out_shape = [jax.ShapeDtypeStruct((n, d), F32), jax.ShapeDtypeStruct((n_seq, nc, 2, tc), F32)]
    prefetch = []
    if guest is not None:
        by_step = lambda f: (lambda i, c, *pf: f(i * nc + c, *pf))
        in_specs += [pl.BlockSpec(shape, by_step(f)) for shape, f in guest.in_specs]
        out_specs += [pl.BlockSpec(shape, by_step(f)) for shape, f in guest.out_specs]
        args += guest.args
        out_shape += guest.out_shape
        prefetch = [] if guest.prefetch is None else [guest.prefetch]
    outs = pl.pallas_call(
        functools.partial(_ffn_seq_kernel, tm=tm, blocks_per_seq=bps, guest=guest),
        grid_spec=pltpu.PrefetchScalarGridSpec(
            num_scalar_prefetch=len(prefetch),
            grid=(n // tm, nc),
            in_specs=in_specs,
            out_specs=out_specs,
            scratch_shapes=[pltpu.VMEM((nc, tm + 8, tc), F32), pltpu.VMEM((tm, d), F32)],
        ),
        out_shape=out_shape,
        compiler_params=_params("arbitrary", "arbitrary", vmem=VMEM_LIMIT_HOSTED),
    )(*prefetch, *args)
    conv_state = outs[1].transpose(0, 2, 1, 3).reshape(n_seq, 2, dff)
    return outs[0], conv_state, outs[2:]


def _ffn_step_kernel(x_ref, g_ref, wg_ref, wu_ref, cw_ref, cb_ref, wd_ref, s0_ref, s1_ref,
                     o_ref, a_ref, h_scr, acc_scr):
    c = pl.program_id(0)

    @pl.when(c == 0)
    def _():
        h_scr[...] = _rms(x_ref[...], g_ref[...]).astype(BF16)

    h = h_scr[...]
    a = _dot(h, wg_ref[...])
    u = _dot(h, wu_ref[...])
    a_ref[...] = a
    cw = cw_ref[...]
    conv = cb_ref[...] + s0_ref[...] * cw[0:1] + s1_ref[...] * cw[1:2] + a * cw[2:3]
    y = (_silu(conv) * u).astype(BF16)
    part = _dot(y, wd_ref[...])

    @pl.when(c == 0)
    def _():
        acc_scr[...] = part

    @pl.when(c > 0)
    def _():
        acc_scr[...] = acc_scr[...] + part

    @pl.when(c == pl.num_programs(0) - 1)
    def _():
        o_ref[...] = x_ref[...] + acc_scr[...]


def _ffn_step(x, g, wg, wu, cw, cb, wd, s0, s1, tc):
    n, d = x.shape
    dff = wg.shape[1]
    return pl.pallas_call(
        _ffn_step_kernel,
        grid=(dff // tc,),
        in_specs=[
            pl.BlockSpec((n, d), lambda c: (0, 0)),
            pl.BlockSpec((1, d), lambda c: (0, 0)),
            pl.BlockSpec((d, tc), lambda c: (0, c)),
            pl.BlockSpec((d, tc), lambda c: (0, c)),
            pl.BlockSpec((3, tc), lambda c: (0, c)),
            pl.BlockSpec((1, tc), lambda c: (0, c)),
            pl.BlockSpec((tc, d), lambda c: (c, 0)),
            pl.BlockSpec((n, tc), lambda c: (0, c)),
            pl.BlockSpec((n, tc), lambda c: (0, c)),
        ],
        out_specs=[pl.BlockSpec((n, d), lambda c: (0, 0)),
                   pl.BlockSpec((n, tc), lambda c: (0, c))],
        out_shape=[jax.ShapeDtypeStruct((n, d), F32), jax.ShapeDtypeStruct((n, dff), F32)],
        scratch_shapes=[pltpu.VMEM((n, d), BF16), pltpu.VMEM((n, d), F32)],
        compiler_params=_params("arbitrary"),
    )(x, g.reshape(1, d), wg, wu, cw, cb.reshape(1, dff), wd, s0, s1)


def _out_ab_kernel(oa_ref, ob_ref, w_ref, x_ref, o_ref):
    ka = oa_ref.shape[1]
    o_ref[...] = (x_ref[...] + _dot(oa_ref[...].astype(BF16), w_ref[0:ka, :])
                  + _dot(ob_ref[...].astype(BF16), w_ref[ka:, :]))


def _out_ab(oa, ob, w_bf, x, tm):
    n, d = x.shape
    ka, kb = oa.shape[1], ob.shape[1]
    return pl.pallas_call(
        _out_ab_kernel,
        grid=(n // tm,),
        in_specs=[pl.BlockSpec((tm, ka), lambda i: (i, 0)),
                  pl.BlockSpec((tm, kb), lambda i: (i, 0)),
                  pl.BlockSpec((ka + kb, d), lambda i: (0, 0)),
                  pl.BlockSpec((tm, d), lambda i: (i, 0))],
        out_specs=pl.BlockSpec((tm, d), lambda i: (i, 0)),
        out_shape=jax.ShapeDtypeStruct((n, d), F32),
        compiler_params=_params("parallel"),
    )(oa, ob, w_bf, x)


def _merge_groups(os, ls):
    m = jnp.maximum(jnp.maximum(ls[0], ls[1]), ls[2])
    es = [jnp.exp(l - m) for l in ls]
    den = es[0] + es[1] + es[2]
    return (es[0] / den) * os[0] + (es[1] / den) * os[1] + (es[2] / den) * os[2]


def _out_c_kernel(o0, o1, o2, l0, l1, l2, w_ref, x_ref, o_ref):
    merged = jnp.concatenate(
        [_merge_groups([o0[0, hp], o1[0, hp], o2[0, hp]], [l0[0, hp], l1[0, hp], l2[0, hp]])
         for hp in range(HEADS // 2)], axis=1)
    o_ref[...] = x_ref[...] + _dot(merged.astype(BF16), w_ref[...])


def _out_c(os, ls, w_bf, x, tm):
    n, d = x.shape
    k = w_bf.shape[0]
    per_seq = os[0].shape[2] // tm
    act = pl.BlockSpec((1, HEADS // 2, tm, LANES), lambda i: (i // per_seq, 0, i % per_seq, 0))
    return pl.pallas_call(
        _out_c_kernel,
        grid=(n // tm,),
        in_specs=[act] * 6 + [pl.BlockSpec((k, d), lambda i: (0, 0)),
                              pl.BlockSpec((tm, d), lambda i: (i, 0))],
        out_specs=pl.BlockSpec((tm, d), lambda i: (i, 0)),
        out_shape=jax.ShapeDtypeStruct((n, d), F32),
        compiler_params=_params("parallel"),
    )(*os, *ls, w_bf, x)


def _out_ab_t_kernel(oat_ref, ob_ref, w_ref, x_ref, o_ref):
    ka = oat_ref.shape[0]
    o_ref[...] = (x_ref[...] + _dot(oat_ref[...].T.astype(BF16), w_ref[0:ka, :])
                  + _dot(ob_ref[...].astype(BF16), w_ref[ka:, :]))


def _out_ab_t(oa_t, ob, w_bf, x):
    return pl.pallas_call(
        _out_ab_t_kernel,
        out_shape=jax.ShapeDtypeStruct(x.shape, F32),
        compiler_params=pltpu.CompilerParams(vmem_limit_bytes=VMEM_LIMIT),
    )(oa_t, ob, w_bf, x)


def _out_c_t_kernel(o0, o1, o2, l0, l1, l2, w_ref, x_ref, o_ref):
    merged = _merge_groups([o0[...], o1[...], o2[...]], [l0[...], l1[...], l2[...]])
    o_ref[...] = x_ref[...] + _dot(merged.T.astype(BF16), w_ref[...])


def _out_c_t(os_t, ls_t, w_bf, x):
    return pl.pallas_call(
        _out_c_t_kernel,
        out_shape=jax.ShapeDtypeStruct(x.shape, F32),
        compiler_params=pltpu.CompilerParams(vmem_limit_bytes=VMEM_LIMIT),
    )(*os_t, *ls_t, w_bf, x)


def _top_k_mask(g, lane, k):
    sel = jnp.zeros(g.shape, F32)
    lane_f = lane.astype(F32)
    for _ in range(k):
        m = jnp.max(g, axis=1, keepdims=True)
        idx = jnp.min(jnp.where(g == m, lane_f, float(g.shape[1])), axis=1, keepdims=True)
        hit = lane_f == idx
        sel = jnp.where(hit, 1.0, sel)
        g = jnp.where(hit, LOWEST, g)
    return sel


def _moba_prep_kernel(q_ref, k_ref, v_ref, qa_ref, ka_ref, va_ref, kmean_scr):
    i = pl.program_id(1)

    @pl.when(i == 0)
    def _():
        kmean_scr[...] = jnp.zeros(kmean_scr.shape, F32)

    rows = q_ref.shape[1]
    lane = lax.broadcasted_iota(jnp.int32, (rows, LANES), 1)
    low = lane < DH
    for hp in range(HEADS // 2):
        cols = slice(hp * LANES, (hp + 1) * LANES)
        q2 = q_ref[0, :, cols]
        k2 = k_ref[0, :, cols]
        v2 = v_ref[0, :, cols]
        km2 = kmean_scr[:, cols]
        for e in range(2):
            mine = low if e == 0 else jnp.logical_not(low)
            gate = _dot_t(jnp.where(mine, q2, 0.0), km2, precision=HIGHEST)
            past = lane < i
            sel = _top_k_mask(jnp.where(past, gate, NEG_INF), lane, MOBA_TOPK)
            keep = jnp.where(past, sel, 0.0) + jnp.where(lane == i, 1.0, 0.0)
            bias = jnp.where(keep > 0.5, 0.0, NEG_INF)
            onehot = jnp.where(lane == i, 1.0, 0.0)
            if e == 0:
                bias = pltpu.roll(bias, DH, axis=1)
                onehot = jnp.where(lane == i + DH, 1.0, 0.0)
                ones_col = jnp.where(lane == DH, 1.0, 0.0)
            else:
                ones_col = jnp.where(lane == 0, 1.0, 0.0)
            h = 2 * hp + e
            qa_ref[0, h] = jnp.where(mine, q2 * SCALE, bias).astype(BF16)
            ka_ref[0, h] = jnp.where(mine, k2, onehot).astype(BF16)
            va_ref[0, h] = jnp.where(mine, v2, ones_col).astype(BF16)

    kmean_scr[pl.ds(i, 1), :] = jnp.mean(k_ref[0], axis=0, keepdims=True)


def _moba_prep(proj, n_blocks_max=LANES):
    b, s, _ = proj.shape
    nb = s // MOBA_BLOCK
    assert nb <= n_blocks_max // 2 and nb <= DH // 2
    blk = lambda j: pl.BlockSpec((1, MOBA_BLOCK, W_ATT), lambda bi, i: (bi, i, j))
    aug = pl.BlockSpec((1, HEADS, MOBA_BLOCK, LANES), lambda bi, i: (bi, 0, i, 0))
    shape = jax.ShapeDtypeStruct((b, HEADS, s, LANES), BF16)
    return pl.pallas_call(
        _moba_prep_kernel,
        grid=(b, nb),
        in_specs=[blk(0), blk(1), blk(2)],
        out_specs=[aug, aug, aug],
        out_shape=[shape, shape, shape],
        scratch_shapes=[pltpu.VMEM((LANES, W_ATT), F32)],
        compiler_params=_params("arbitrary", "arbitrary"),
    )(proj, proj, proj)


def _moba_attn_kernel(q_ref, k_ref, v_ref, o_ref, m_scr, acc_scr, *, tq):
    qi = pl.program_id(2)
    tk = MOBA_BLOCK
    n_diag = tq // tk
    row = lax.broadcasted_iota(jnp.int32, (tq, tk), 0)
    col = lax.broadcasted_iota(jnp.int32, (tq, tk), 1)
    lane = lax.broadcasted_iota(jnp.int32, (tq, LANES), 1)
    hs = q_ref.shape[1]

    def step(n, causal_offset=None):
        st = pl.multiple_of(n * tk, tk)
        for e in range(hs):
            s = _dot_t(q_ref[0, e], k_ref[0, e, pl.ds(st, tk), :])
            if causal_offset is not None:
                s = jnp.where(col + causal_offset <= row, s, NEG_INF)
            m = m_scr[e]
            m_new = jnp.maximum(m, jnp.max(s, axis=1, keepdims=True))
            m_scr[e] = m_new
            p = jnp.exp(s - jnp.concatenate([m_new] * (tk // LANES), axis=1)).astype(BF16)
            acc_scr[e] = (jnp.exp(m - m_new) * acc_scr[e]
                          + _dot(p, v_ref[0, e, pl.ds(st, tk), :]))

    m_scr[...] = jnp.full(m_scr.shape, LOWEST, F32)
    acc_scr[...] = jnp.zeros(acc_scr.shape, F32)
    for j in range(n_diag):
        step(qi * n_diag + j, causal_offset=j * tk)

    def body(n, carry):
        step(n)
        return carry

    lax.fori_loop(0, qi * n_diag, body, 0)
    for e in range(0, hs, 2):
        even, odd = acc_scr[e], acc_scr[e + 1]
        o_ref[0, :, e // 2 * LANES:(e // 2 + 1) * LANES] = jnp.where(
            lane < DH, even / even[:, DH:DH + 1], odd / odd[:, 0:1])


def _moba_attn(qa, ka, va, tq, hs):
    b, h, s, _ = qa.shape
    return pl.pallas_call(
        functools.partial(_moba_attn_kernel, tq=tq),
        grid=(b, h // hs, s // tq),
        in_specs=[pl.BlockSpec((1, hs, tq, LANES), lambda bi, hp, i: (bi, hp, i, 0)),
                  pl.BlockSpec((1, hs, s, LANES), lambda bi, hp, i: (bi, hp, 0, 0),
                               pipeline_mode=pl.Buffered(1)),
                  pl.BlockSpec((1, hs, s, LANES), lambda bi, hp, i: (bi, hp, 0, 0),
                               pipeline_mode=pl.Buffered(1))],
        out_specs=pl.BlockSpec((1, tq, hs // 2 * LANES), lambda bi, hp, i: (bi, i, hp)),
        out_shape=jax.ShapeDtypeStruct((b, s, W_ATT), F32),
        scratch_shapes=[pltpu.VMEM((hs, tq, LANES), F32), pltpu.VMEM((hs, tq, LANES), F32)],
        compiler_params=_params("parallel", "parallel", "arbitrary"),
    )(qa, ka, va)


def _column(ref, rows, lane_is_seq):
    return jnp.sum(jnp.where(lane_is_seq, ref[rows, :], 0.0), axis=1, keepdims=True)


def _zero_refs(out_refs):
    for ref in out_refs:
        ref[...] = jnp.zeros(ref.shape, F32)


def _moba_dec_body(bi, _, in_refs, out_refs, *, n_pages, page):
    q_ref, kn_ref, vn_ref = in_refs[:3]
    k_refs, v_refs = in_refs[3:3 + n_pages], in_refs[3 + n_pages:]
    o_ref, = out_refs
    pages_per_block = MOBA_BLOCK // page
    n_past = n_pages // pages_per_block
    lane_is_seq = lax.broadcasted_iota(jnp.int32, (DH, LANES), 1) == bi
    for h in range(HEADS):
        rows = slice(h * DH, (h + 1) * DH)
        qc = _column(q_ref, rows, lane_is_seq)
        kc = _column(kn_ref, rows, lane_is_seq)
        vc = _column(vn_ref, rows, lane_is_seq)
        raw = [jnp.sum(k_refs[j][0, h] * qc, axis=0, keepdims=True) for j in range(n_pages)]
        gates = []
        for n in range(n_past):
            tot = raw[n * pages_per_block]
            for j in range(1, pages_per_block):
                tot = tot + raw[n * pages_per_block + j]
            gates.append(jnp.sum(tot, axis=1, keepdims=True) * (1.0 / MOBA_BLOCK))
        sel = [jnp.zeros((1, 1), F32)] * n_past
        for _ in range(min(MOBA_TOPK, n_past + 1)):
            best = functools.reduce(jnp.maximum, gates)
            free = jnp.ones((1, 1), F32)
            for n in range(n_past):
                hit = jnp.where(gates[n] == best, free, 0.0)
                free = free - hit
                sel[n] = jnp.maximum(sel[n], hit)
                gates[n] = jnp.where(hit > 0.5, LOWEST, gates[n])
        s_self = jnp.sum(qc * kc, axis=0, keepdims=True) * SCALE
        s = [jnp.where(sel[j // pages_per_block] > 0.5, raw[j] * SCALE, NEG_INF)
             for j in range(n_pages)]
        m = s_self
        for sj in s:
            m = jnp.maximum(m, jnp.max(sj, axis=1, keepdims=True))
        p_self = jnp.exp(s_self - m)
        l = p_self
        acc = jnp.zeros((DH, page), F32)
        for j in range(n_pages):
            pj = jnp.exp(s[j] - m)
            l = l + jnp.sum(pj, axis=1, keepdims=True)
            acc = acc + v_refs[j][0, h] * pj
        o = (jnp.sum(acc, axis=1, keepdims=True) + p_self * vc) / l
        o_ref[rows, :] = jnp.where(lane_is_seq, o, o_ref[rows, :])


def _moba_dec_guest(proj_t, cache_k, cache_v, page_table):
    b = proj_t.shape[1]
    n_pages = page_table.shape[1]
    page = cache_k.shape[3]
    assert b == LANES
    vec = lambda j: ((W_ATT, b), lambda bi, pt: (j, 0))
    pg = lambda j: ((1, HEADS, DH, page), lambda bi, pt: (pt[bi * n_pages + j], 0, 0, 0))
    pages = [pg(j) for j in range(n_pages)]
    return _Guest(
        in_specs=[vec(0), vec(1), vec(2)] + pages + pages,
        args=[proj_t] * 3 + [cache_k] * n_pages + [cache_v] * n_pages,
        out_specs=[((W_ATT, b), lambda bi, pt: (0, 0))],
        out_shape=[jax.ShapeDtypeStruct((W_ATT, b), F32)],
        init=_zero_refs,
        body=functools.partial(_moba_dec_body, n_pages=n_pages, page=page),
        prefetch=page_table.reshape(-1))


def _hgrn_seq_kernel(q_ref, f_ref, i_ref, g_ref, lb_ref, gn_ref, o_ref, s_ref, st_scr, *, tb):
    t = pl.program_id(1)

    @pl.when(t == 0)
    def _():
        st_scr[...] = jnp.zeros(st_scr.shape, F32)

    c = HGRN_CHUNK
    lb = lb_ref[...]
    gn = gn_ref[...]
    r_i = lax.broadcasted_iota(jnp.int32, (c, c), 0)
    c_i = lax.broadcasted_iota(jnp.int32, (c, c), 1)
    tril = c_i <= r_i
    ones_tril = jnp.where(tril, 1.0, 0.0)
    for ch in range(tb // c):
        rows = slice(ch * c, (ch + 1) * c)
        f = lb + (1.0 - lb) * _sigmoid(f_ref[0, rows, :])
        k = 1.0 - f
        q = _silu(q_ref[0, rows, :])
        v = i_ref[0, rows, :]
        b = jnp.dot(ones_tril, jnp.log(f), precision=HIGHEST, preferred_element_type=F32)
        b_mid = b[c // 2 - 1:c // 2, :]
        b_last = b[c - 1:c, :]
        q_mid = (q * jnp.exp(b - b_mid)).astype(BF16)
        k_mid = (k * jnp.exp(b_mid - b)).astype(BF16)
        k_end = (k * jnp.exp(b_last - b)).astype(BF16)
        q_in = (q * jnp.exp(b)).astype(BF16)
        decay = jnp.exp(b_last)
        vb = v.astype(BF16)
        outs = []
        for h in range(H_B):
            cols = slice(h * D_B, (h + 1) * D_B)
            a = jnp.where(tril, _dot_t(q_mid[:, cols], k_mid[:, cols]), 0.0)
            st = st_scr[h]
            o = _dot(a.astype(BF16), vb[:, cols]) + _dot_t(q_in[:, cols], st.astype(BF16))
            st_scr[h] = st * decay[:, cols] + _dot(v[:, cols].T.astype(BF16), k_end[:, cols])
            outs.append(_rms(o, gn[:, cols]))
        o_ref[0, rows, :] = jnp.concatenate(outs, axis=1) * _silu(g_ref[0, rows, :])

    @pl.when(t == pl.num_programs(1) - 1)
    def _():
        for h in range(H_B):
            s_ref[0, h] = st_scr[h].T


def _hgrn_seq(proj, lb, gn, first_chunk, tb):
    b, s, _ = proj.shape
    w = H_B * D_B
    blk = lambda j: pl.BlockSpec((1, tb, w), lambda bi, t: (bi, t, first_chunk + j))
    vec = pl.BlockSpec((1, w), lambda bi, t: (0, 0))
    return pl.pallas_call(
        functools.partial(_hgrn_seq_kernel, tb=tb),
        grid=(b, s // tb),
        in_specs=[blk(0), blk(1), blk(2), blk(3), vec, vec],
        out_specs=[pl.BlockSpec((1, tb, w), lambda bi, t: (bi, t, 0)),
                   pl.BlockSpec((1, H_B, D_B, D_B), lambda bi, t: (bi, 0, 0, 0))],
        out_shape=[jax.ShapeDtypeStruct((b, s, w), F32),
                   jax.ShapeDtypeStruct((b, H_B, D_B, D_B), F32)],
        scratch_shapes=[pltpu.VMEM((H_B, D_B, D_B), F32)],
        compiler_params=_params("parallel", "arbitrary"),
    )(proj, proj, proj, proj, lb.reshape(1, w), jnp.tile(gn.astype(F32), H_B).reshape(1, w))


def _hgrn_step_kernel(q_ref, f_ref, i_ref, g_ref, lb_ref, gn_ref, s0_ref, o_ref, s_ref):
    bi = pl.program_id(0)
    lb = lb_ref[...]
    gn = gn_ref[...]
    f = lb + (1.0 - lb) * _sigmoid(f_ref[pl.ds(bi, 1), :])
    q = _silu(q_ref[pl.ds(bi, 1), :])
    v = i_ref[pl.ds(bi, 1), :]
    eye = (lax.broadcasted_iota(jnp.int32, (D_B, D_B), 0)
           == lax.broadcasted_iota(jnp.int32, (D_B, D_B), 1))
    outs = []
    for h in range(H_B):
        cols = slice(h * D_B, (h + 1) * D_B)
        f_col = jnp.sum(jnp.where(eye, f[:, cols], 0.0), axis=1, keepdims=True)
        q_col = jnp.sum(jnp.where(eye, q[:, cols], 0.0), axis=1, keepdims=True)
        s_new = f_col * s0_ref[0, h] + (1.0 - f_col) * v[:, cols]
        s_ref[0, h] = s_new
        o = jnp.sum(q_col * s_new, axis=0, keepdims=True)
        outs.append(_rms(o, gn[:, cols]))
    o_ref[0] = jnp.concatenate(outs, axis=1) * _silu(g_ref[pl.ds(bi, 1), :])


def _hgrn_step(proj, lb, gn, s0, first_chunk):
    b = proj.shape[0]
    w = H_B * D_B
    blk = lambda j: pl.BlockSpec((b, w), lambda bi: (0, first_chunk + j))
    vec = pl.BlockSpec((1, w), lambda bi: (0, 0))
    st = pl.BlockSpec((1, H_B, D_B, D_B), lambda bi: (bi, 0, 0, 0))
    o, s = pl.pallas_call(
        _hgrn_step_kernel,
        grid=(b,),
        in_specs=[blk(0), blk(1), blk(2), blk(3), vec, vec, st],
        out_specs=[pl.BlockSpec((1, 1, w), lambda bi: (bi, 0, 0)), st],
        out_shape=[jax.ShapeDtypeStruct((b, 1, w), F32),
                   jax.ShapeDtypeStruct((b, H_B, D_B, D_B), F32)],
        compiler_params=_params("parallel"),
    )(proj, proj, proj, proj, lb.reshape(1, w),
      jnp.tile(gn.astype(F32), H_B).reshape(1, w), s0)
    return o.reshape(b, w), s


def _band_kernel(q_ref, kc_ref, kp_ref, vc_ref, vp_ref, o_ref, l_ref, *, tq, dil):
    first = pl.program_id(1) == 0
    r = pl.program_id(2)
    row = lax.broadcasted_iota(jnp.int32, (BAND, 2 * BAND), 0)
    col = lax.broadcasted_iota(jnp.int32, (BAND, 2 * BAND), 1)
    band = jnp.logical_and(col >= row, col <= row + BAND)
    lane = lax.broadcasted_iota(jnp.int32, (BAND, LANES), 1)
    low = lane < DH
    for i in range(tq // BAND):
        rows = slice(i * BAND, (i + 1) * BAND)
        if i == 0:
            k_prev, v_prev = kp_ref[0, 0], vp_ref[0, 0]
            mask = jnp.logical_and(band, jnp.logical_or(col >= BAND, jnp.logical_not(first)))
        else:
            prev = slice((i - 1) * BAND, i * BAND)
            k_prev, v_prev = kc_ref[0, 0, prev, :], vc_ref[0, 0, prev, :]
            mask = band
        kk = jnp.concatenate([k_prev, kc_ref[0, 0, rows, :]], axis=0).astype(BF16)
        vv = jnp.concatenate([v_prev, vc_ref[0, 0, rows, :]], axis=0).astype(BF16)
        out_rows = rows if dil == 1 else pl.ds(r + i * BAND * dil, BAND, stride=dil)
        for hp in range(HEADS // 2):
            cols = slice(hp * LANES, (hp + 1) * LANES)
            q2 = q_ref[0, 0, rows, cols] * SCALE
            res = []
            for e in range(2):
                mine = low if e == 0 else jnp.logical_not(low)
                s = _dot_t(jnp.where(mine, q2, 0.0).astype(BF16), kk[:, cols])
                s = jnp.where(mask, s, NEG_INF)
                m = jnp.max(s, axis=1, keepdims=True)
                p = jnp.exp(s - m)
                l = jnp.sum(p, axis=1, keepdims=True)
                res.append((_dot((p / l).astype(BF16), vv[:, cols]), m + jnp.log(l)))
            o_ref[0, hp, out_rows, :] = jnp.where(low, res[0][0], res[1][0])
            l_ref[0, hp, out_rows, :] = jnp.where(low, res[0][1], res[1][1])


def _band_attention(proj, tq):
    b, dil, ln, _ = proj.shape
    sub = tq // BAND
    cur = lambda comp: pl.BlockSpec((1, 1, tq, W_ATT), lambda bi, l, r: (bi, r, l, comp))
    prev = lambda comp: pl.BlockSpec((1, 1, BAND, W_ATT),
                                     lambda bi, l, r: (bi, r, jnp.maximum(l * sub - 1, 0), comp))
    out = pl.BlockSpec((1, HEADS // 2, tq * dil, LANES), lambda bi, l, r: (bi, 0, l, 0))
    shape = jax.ShapeDtypeStruct((b, HEADS // 2, ln * dil, LANES), F32)
    return pl.pallas_call(
        functools.partial(_band_kernel, tq=tq, dil=dil),
        grid=(b, ln // tq, dil),
        in_specs=[cur(0), cur(1), prev(1), cur(2), prev(2)],
        out_specs=[out, out],
        out_shape=[shape, shape],
        compiler_params=_params("parallel", "arbitrary", "arbitrary"),
    )(proj, proj, proj, proj, proj)


def _dil_cache_body(step, _, in_refs, out_refs, *, dils, hb, head_groups):
    bi, hg = step // head_groups, step % head_groups
    for g, dil in enumerate(dils):
        _dil_cache_group(bi, hg, *in_refs[5 * g:5 * g + 5], *out_refs[4 * g:4 * g + 4],
                         dil=dil, hb=hb)


def _dil_cache_init(out_refs):
    for g in range(len(out_refs) // 4):
        _zero_refs(out_refs[4 * g + 2:4 * g + 4])


def _dil_cache_kernel(*refs, dils, hb):
    n_in = 5 * len(dils)

    @pl.when(jnp.logical_and(pl.program_id(0) == 0, pl.program_id(1) == 0))
    def _():
        _dil_cache_init(refs[n_in:])

    _dil_cache_body(pl.program_id(0) * pl.num_programs(1) + pl.program_id(1), None,
                    refs[:n_in], refs[n_in:], dils=dils, hb=hb, head_groups=HEADS // hb)


def _dil_cache_group(bi, hg, q_ref, kn_ref, vn_ref, ck_ref, cv_ref, ok_ref, ov_ref, o_ref, l_ref,
                     *, dil, hb):
    lw = ck_ref.shape[3]
    nr = hb * DH
    rows = pl.ds(pl.multiple_of(hg * nr, nr), nr)
    lane_is_seq = lax.broadcasted_iota(jnp.int32, (nr, LANES), 1) == bi
    pos = lax.broadcasted_iota(jnp.int32, (hb, lw), 1)
    on_grid = (pos & (dil - 1)) == 0
    newest = lax.broadcasted_iota(jnp.int32, (DH, lw), 1) == lw - 1
    qc = _column(q_ref, rows, lane_is_seq) * SCALE
    kc = _column(kn_ref, rows, lane_is_seq)
    vc = _column(vn_ref, rows, lane_is_seq)
    head = lambda x, hh: x[hh * DH:(hh + 1) * DH]
    s = jnp.concatenate([jnp.sum(ck_ref[0, hh] * head(qc, hh), axis=0, keepdims=True)
                         for hh in range(hb)], axis=0)
    s = jnp.where(on_grid, s, NEG_INF)
    qk = qc * kc
    s_self = jnp.concatenate([jnp.sum(head(qk, hh), axis=0, keepdims=True)
                              for hh in range(hb)], axis=0)
    m = jnp.maximum(jnp.max(s, axis=1, keepdims=True), s_self)
    p = jnp.exp(s - m)
    p_self = jnp.exp(s_self - m)
    l = jnp.sum(p, axis=1, keepdims=True) + p_self
    pn = p / l
    w_self = p_self / l
    lse = m + jnp.log(l)
    os, ls = [], []
    for hh in range(hb):
        k = ck_ref[0, hh]
        v = cv_ref[0, hh]
        os.append(jnp.sum(v * pn[hh:hh + 1, :], axis=1, keepdims=True)
                  + w_self[hh:hh + 1, :] * head(vc, hh))
        ls.append(jnp.broadcast_to(lse[hh:hh + 1, :], (DH, 1)))
        ok_ref[0, hh] = jnp.where(newest, head(kc, hh), pltpu.roll(k, lw - 1, axis=1))
        ov_ref[0, hh] = jnp.where(newest, head(vc, hh), pltpu.roll(v, lw - 1, axis=1))
    o_ref[rows, :] = jnp.where(lane_is_seq, jnp.concatenate(os, axis=0), o_ref[rows, :])
    l_ref[rows, :] = jnp.where(lane_is_seq, jnp.concatenate(ls, axis=0), l_ref[rows, :])


def _dil_cache_guest(proj_t, n_groups, groups, dils, caches, hb):
    b, h, dh, _ = caches[0][0].shape
    hgs = h // hb
    assert b == LANES and all(dil & (dil - 1) == 0 for dil in dils)
    vec = lambda j: ((W_ATT, b), lambda step, *_: (j, 0))
    res = ((W_ATT, b), lambda step, *_: (0, 0))
    res_shape = jax.ShapeDtypeStruct((W_ATT, b), F32)
    in_specs, out_specs, out_shape, args = [], [], [], []
    for g, (ck, cv) in zip(groups, caches):
        cache = ((1, hb, dh, ck.shape[3]), lambda step, *_: (step // hgs, step % hgs, 0, 0))
        in_specs += [vec(g), vec(n_groups + g), vec(2 * n_groups + g), cache, cache]
        args += [proj_t, proj_t, proj_t, ck, cv]
        out_specs += [cache, cache, res, res]
        out_shape += [jax.ShapeDtypeStruct(ck.shape, F32), jax.ShapeDtypeStruct(cv.shape, F32),
                      res_shape, res_shape]
    return _Guest(in_specs, args, out_specs, out_shape, _dil_cache_init,
                  functools.partial(_dil_cache_body, dils=tuple(dils), hb=hb, head_groups=hgs))


def _dil_cache(guest, b, head_groups):
    by_step = lambda f: (lambda bi, hg: f(bi * head_groups + hg))
    return pl.pallas_call(
        functools.partial(_dil_cache_kernel, dils=guest.body.keywords["dils"],
                          hb=guest.body.keywords["hb"]),
        grid=(b, head_groups),
        in_specs=[pl.BlockSpec(shape, by_step(f)) for shape, f in guest.in_specs],
        out_specs=[pl.BlockSpec(shape, by_step(f)) for shape, f in guest.out_specs],
        out_shape=guest.out_shape,
        compiler_params=_params("arbitrary", "arbitrary"),
    )(*guest.args)


def kernel(x_prompt, x_sample, cache_k_a, cache_v_a, state_hgrn, cache_k_c0, cache_v_c0, cache_k_c1, cache_v_c1, cache_k_c2, cache_v_c2, state_conv, page_table, norm_mix, norm_ffn, w_in_ab, q_norm_a, k_norm_a, lb_logits, g_norm_b, w_out_ab, w_in_c, q_norm_c, k_norm_c, w_out_c, w_gate, w_up, conv_w, conv_b, w_down):
    bsz, seq, d = x_prompt.shape
    nb = x_sample.shape[0]
    n = bsz * seq
    dff = w_gate.shape[2]
    tm, tc = 512, dff // 2
    ng = len(C_PATTERNS)
    c_caches = ((cache_k_c0, cache_v_c0), (cache_k_c1, cache_v_c1), (cache_k_c2, cache_v_c2))

    lb_all = jnp.cumsum(jax.nn.softmax(lb_logits.astype(F32), axis=0), axis=0)
    bf = lambda w: w.astype(BF16)

    xp = x_prompt.reshape(n, d)
    xs = x_sample.reshape(nb, d)

    w_in = bf(w_in_ab[0])
    gains = _head_gains([q_norm_a[0], k_norm_a[0]])
    w_out = bf(w_out_ab[0])
    lb = lb_all[0]

    pp = _norm_proj(xp, norm_mix[0], w_in, gains, tm).reshape(bsz, seq, -1)
    qa, ka, va = _moba_prep(pp)
    oa = _moba_attn(qa, ka, va, 512, HEADS).reshape(n, W_ATT)
    ob, hg_p = _hgrn_seq(pp, lb, g_norm_b[0], 3, 256)
    xp = _out_ab(oa, ob.reshape(n, -1), w_out, xp, tm)
    k_a_p = pp[:, :, W_ATT:2 * W_ATT].reshape(1, bsz, seq, HEADS, DH)
    v_a_p = pp[:, :, 2 * W_ATT:3 * W_ATT].reshape(1, bsz, seq, HEADS, DH)

    pos_minor = lambda c: jnp.transpose(c, (0, 2, 3, 1))
    pos_major = lambda c: jnp.transpose(c, (0, 3, 1, 2))

    conv_p, conv_s = [], []
    tm_ffn, nc_ffn = 256, 2

    def ffn(l, xp, guest):
        wg, wu, wd = bf(w_gate[l]), bf(w_up[l]), bf(w_down[l])
        xp, cp, guest_out = _ffn_seq(xp, norm_ffn[l], wg, wu, conv_w[l], conv_b[l], wd,
                                     bsz, tm_ffn, nc_ffn, guest)

        def sample(xs):
            xs, a_s = _ffn_step(xs, norm_ffn[l], wg, wu, conv_w[l], conv_b[l], wd,
                                state_conv[l, :, 0, :], state_conv[l, :, 1, :], tc)
            conv_s.append(jnp.stack([state_conv[l, :, 1, :], a_s], axis=1))
            return xs

        conv_p.append(cp)
        return xp, sample, guest_out

    ps = _norm_proj(xs, norm_mix[0], w_in, gains, nb)
    assert n // tm_ffn * nc_ffn == nb
    xp, ffn_sample, (oas_t,) = ffn(0, xp, _moba_dec_guest(
        ps.T, pos_minor(cache_k_a[0]), pos_minor(cache_v_a[0]), page_table))
    obs, hg_s = _hgrn_step(ps, lb, g_norm_b[0], state_hgrn[0], 3)
    xs = ffn_sample(_out_ab_t(oas_t, obs, w_out, xs))
    k_a_s = ps[:, W_ATT:2 * W_ATT].reshape(1, nb, 1, HEADS, DH)
    v_a_s = ps[:, 2 * W_ATT:3 * W_ATT].reshape(1, nb, 1, HEADS, DH)

    w_in = bf(w_in_c[0])
    w_out = bf(w_out_c[0])
    gains = _head_gains([q_norm_c[0], k_norm_c[0]])

    os, ls, c_prompt = [], [], []
    for g, (win, dil) in enumerate(C_PATTERNS):
        ln = seq // dil
        w_g = jnp.concatenate([w_in[:, (c * ng + g) * W_ATT:(c * ng + g + 1) * W_ATT]
                               for c in range(3)], axis=1)
        xr = xp.reshape(bsz, ln, dil, d).transpose(0, 2, 1, 3).reshape(n, d)
        pg = _norm_proj(xr, norm_mix[1], w_g, gains, tm)
        o, lse = _band_attention(pg.reshape(bsz, dil, ln, 3 * W_ATT), min(512, 2048 // dil))
        os.append(o)
        ls.append(lse)
        keep = min(win, seq)
        kd = keep // dil
        tail = lax.optimization_barrier(lax.slice(
            pg.reshape(bsz * dil, ln, 3 * W_ATT), (0, ln - kd, W_ATT), (bsz * dil, ln, 3 * W_ATT)))
        tail = tail.reshape(bsz, dil, kd, 2, HEADS, DH).transpose(3, 0, 2, 1, 4, 5)
        tail = tail.reshape(2, 1, bsz, keep, HEADS, DH)
        c_prompt.append((tail[0], tail[1]))
    xp = _out_c(os, ls, w_out, xp, tm)

    gains = _head_gains([q_norm_c[0]] * ng + [k_norm_c[0]] * ng)
    ps_t = _norm_proj(xs, norm_mix[1], w_in, gains, nb).T
    dils = [dil for _, dil in C_PATTERNS]
    views = [(pos_minor(ck[0]), pos_minor(cv[0])) for ck, cv in c_caches]
    hosted = [g for g in range(ng) if C_PATTERNS[g][0] <= 512]
    alone = [g for g in range(ng) if g not in hosted]
    pick = lambda xs_, gs: [xs_[g] for g in gs]
    xp, ffn_sample, hosted_out = ffn(1, xp, _dil_cache_guest(
        ps_t, ng, hosted, pick(dils, hosted), pick(views, hosted), HEADS))
    alone_out = _dil_cache(_dil_cache_guest(ps_t, ng, alone, pick(dils, alone), pick(views, alone),
                                            HEADS // 2), nb, 2)
    decoded = dict(zip(hosted + alone,
                       [outs[4 * j:4 * j + 4] for outs in (hosted_out, alone_out)
                        for j in range(len(outs) // 4)]))
    c_out, os_t, ls_t = [], [], []
    for g in range(ng):
        nk, nv, o_t, l_t = decoded[g]
        c_out += list(c_prompt[g])
        c_out += [pos_major(nk)[None], pos_major(nv)[None]]
        os_t.append(o_t)
        ls_t.append(l_t)
    xs = ffn_sample(_out_c_t(os_t, ls_t, w_out, xs))

    return (xp.reshape(bsz, seq, d), xs.reshape(nb, 1, d),
            k_a_p, v_a_p, k_a_s, v_a_s, hg_p[None], hg_s[None],
            *c_out, jnp.stack(conv_p), jnp.stack(conv_s))
```

```python
import functools

import jax
import jax.numpy as jnp
from jax import lax
from jax.experimental import pallas as pl
from jax.experimental.pallas import tpu as pltpu

F32 = jnp.float32
BF16 = jnp.bfloat16
HIGHEST = lax.Precision.HIGHEST

RMS_EPS = 1e-6
NEG_INF = -1e30
LOWEST = -3e38

LANES = 128
DH = 64
HEADS = 8
W_ATT = HEADS * DH
SCALE = DH ** -0.5
MOBA_BLOCK = 256
MOBA_TOPK = 3
H_B = 4
D_B = 128
HGRN_CHUNK = 32
BAND = 128
C_PATTERNS = ((128, 1), (512, 4), (2048, 16))
VMEM_LIMIT = 48 * 1024 * 1024
VMEM_LIMIT_HOSTED = 56 * 1024 * 1024

_CONTRACT_LAST = (((1,), (1,)), ((), ()))


def _params(*sem, vmem=VMEM_LIMIT):
    return pltpu.CompilerParams(dimension_semantics=sem, vmem_limit_bytes=vmem)


def _sigmoid(x):
    return 1.0 / (1.0 + jnp.exp(-x))


def _silu(x):
    return x * _sigmoid(x)


def _rms(x, g):
    return x * lax.rsqrt(jnp.mean(x * x, axis=-1, keepdims=True) + RMS_EPS) * g


def _dot(a, b):
    return jnp.dot(a, b, preferred_element_type=F32)


def _dot_t(a, b, precision=None):
    return lax.dot_general(a, b, _CONTRACT_LAST, precision=precision, preferred_element_type=F32)


def _norm_proj_kernel(x_ref, g_ref, w_ref, hg_ref, p_ref, o_ref, *, n_norm):
    h = _rms(x_ref[...], g_ref[...]).astype(BF16)
    for j in range(w_ref.shape[1] // W_ATT):
        cols = slice(j * W_ATT, (j + 1) * W_ATT)
        y = _dot(h, w_ref[:, cols])
        if j < n_norm:
            ms = _dot((y * y).astype(BF16), p_ref[...])
            y = y * lax.rsqrt(ms + RMS_EPS) * hg_ref[j:j + 1, :]
        o_ref[:, cols] = y


def _norm_proj(x, g, w_bf, head_gains, tm):
    n, d = x.shape
    m = w_bf.shape[1]
    n_norm = head_gains.shape[0]
    blockdiag = (jnp.arange(W_ATT)[:, None] // DH == jnp.arange(W_ATT)[None, :] // DH)
    p = (blockdiag.astype(F32) / DH).astype(BF16)
    return pl.pallas_call(
        functools.partial(_norm_proj_kernel, n_norm=n_norm),
        grid=(n // tm,),
        in_specs=[
            pl.BlockSpec((tm, d), lambda i: (i, 0)),
            pl.BlockSpec((1, d), lambda i: (0, 0)),
            pl.BlockSpec((d, m), lambda i: (0, 0)),
            pl.BlockSpec((n_norm, W_ATT), lambda i: (0, 0)),
            pl.BlockSpec((W_ATT, W_ATT), lambda i: (0, 0)),
        ],
        out_specs=pl.BlockSpec((tm, m), lambda i: (i, 0)),
        out_shape=jax.ShapeDtypeStruct((n, m), F32),
        compiler_params=_params("parallel"),
    )(x, g.reshape(1, d), w_bf, head_gains, p)


def _head_gains(gains):
    return jnp.stack([jnp.tile(gv.astype(F32), HEADS) for gv in gains])


class _Guest:
    def __init__(self, in_specs, args, out_specs, out_shape, init, body, prefetch=None):
        self.in_specs, self.args = in_specs, args
        self.out_specs, self.out_shape = out_specs, out_shape
        self.init, self.body, self.prefetch = init, body, prefetch


def _ffn_seq_kernel(*refs, tm, blocks_per_seq, guest):
    refs = list(refs)
    pt_ref = refs.pop(0) if guest is not None and guest.prefetch is not None else None
    n_gi = len(guest.in_specs) if guest is not None else 0
    n_go = len(guest.out_specs) if guest is not None else 0
    x_ref, g_ref, wg_ref, wu_ref, cw_ref, cb_ref, wd_ref = refs[:7]
    guest_in = refs[7:7 + n_gi]
    o_ref, cs_ref = refs[7 + n_gi:9 + n_gi]
    guest_out = refs[9 + n_gi:9 + n_gi + n_go]
    a_scr, acc_scr = refs[9 + n_gi + n_go:]
    i = pl.program_id(0)
    c = pl.program_id(1)
    step = i * pl.num_programs(1) + c

    @pl.when(i % blocks_per_seq == 0)
    def _():
        a_scr[c, 0:8, :] = jnp.zeros((8, a_scr.shape[2]), F32)

    @pl.when(c == 0)
    def _():
        acc_scr[...] = jnp.zeros(acc_scr.shape, F32)

    if guest is not None:
        @pl.when(step == 0)
        def _():
            guest.init(guest_out)

    tc = a_scr.shape[2]
    cols = pl.ds(pl.multiple_of(c * tc, LANES), tc)
    h = _rms(x_ref[...], g_ref[...]).astype(BF16)
    a = _dot(h, wg_ref[:, cols])
    u = _dot(h, wu_ref[:, cols])
    a_scr[c, 8:8 + tm, :] = a
    a1 = a_scr[c, 7:7 + tm, :]
    a2 = a_scr[c, 6:6 + tm, :]
    cw = cw_ref[:, cols]
    conv = cb_ref[:, cols] + a2 * cw[0:1] + a1 * cw[1:2] + a * cw[2:3]
    y = (_silu(conv) * u).astype(BF16)
    acc = acc_scr[...] + _dot(y, wd_ref[c])
    acc_scr[...] = acc
    a_scr[c, 0:8, :] = a_scr[c, tm:tm + 8, :]
    cs_ref[0, c] = a_scr[c, tm + 6:tm + 8, :]
    o_ref[...] = x_ref[...] + acc
    if guest is not None:
        guest.body(step, pt_ref, guest_in, guest_out)


def _ffn_seq(x, g, wg, wu, cw, cb, wd, n_seq, tm, nc, guest=None):
    n, d = x.shape
    dff = wg.shape[1]
    tc = dff // nc
    bps = n // n_seq // tm
    resident = lambda shape: pl.BlockSpec(shape, lambda i, c, *_: (0,) * len(shape),
                                          pipeline_mode=pl.Buffered(1))
    in_specs = [
        pl.BlockSpec((tm, d), lambda i, c, *_: (i, 0)),
        resident((1, d)),
        resident((d, dff)), resident((d, dff)), resident((3, dff)), resident((1, dff)),
        resident((nc, tc, d)),
    ]
    args = [x, g.reshape(1, d), wg, wu, cw, cb.reshape(1, dff), wd.reshape(nc, tc, d)]
    out_specs = [pl.BlockSpec((tm, d), lambda i, c, *_: (i, 0)),
                 pl.BlockSpec((1, nc, 2, tc), lambda i, c, *_: (i // bps, 0, 0, 0))]
    out_shape = [jax.ShapeDtypeStruct((n, d), F32), jax.ShapeDtypeStruct((n_seq, nc, 2, tc), F32)]
    prefetch = []
    if guest is not None:
        by_step = lambda f: (lambda i, c, *pf: f(i * nc + c, *pf))
        in_specs += [pl.BlockSpec(shape, by_step(f)) for shape, f in guest.in_specs]
        out_specs += [pl.BlockSpec(shape, by_step(f)) for shape, f in guest.out_specs]
        args += guest.args
        out_shape += guest.out_shape
        prefetch = [] if guest.prefetch is None else [guest.prefetch]
    outs = pl.pallas_call(
        functools.partial(_ffn_seq_kernel, tm=tm, blocks_per_seq=bps, guest=guest),
        grid_spec=pltpu.PrefetchScalarGridSpec(
            num_scalar_prefetch=len(prefetch),
            grid=(n // tm, nc),
            in_specs=in_specs,
            out_specs=out_specs,
            scratch_shapes=[pltpu.VMEM((nc, tm + 8, tc), F32), pltpu.VMEM((tm, d), F32)],
        ),
        out_shape=out_shape,
        compiler_params=_params("arbitrary", "arbitrary", vmem=VMEM_LIMIT_HOSTED),
    )(*prefetch, *args)
    conv_state = outs[1].transpose(0, 2, 1, 3).reshape(n_seq, 2, dff)
    return outs[0], conv_state, outs[2:]


def _ffn_step_kernel(x_ref, g_ref, wg_ref, wu_ref, cw_ref, cb_ref, wd_ref, s0_ref, s1_ref,
                     o_ref, a_ref, h_scr, acc_scr):
    c = pl.program_id(0)

    @pl.when(c == 0)
    def _():
        h_scr[...] = _rms(x_ref[...], g_ref[...]).astype(BF16)

    h = h_scr[...]
    a = _dot(h, wg_ref[...])
    u = _dot(h, wu_ref[...])
    a_ref[...] = a
    cw = cw_ref[...]
    conv = cb_ref[...] + s0_ref[...] * cw[0:1] + s1_ref[...] * cw[1:2] + a * cw[2:3]
    y = (_silu(conv) * u).astype(BF16)
    part = _dot(y, wd_ref[...])

    @pl.when(c == 0)
    def _():
        acc_scr[...] = part

    @pl.when(c > 0)
    def _():
        acc_scr[...] = acc_scr[...] + part

    @pl.when(c == pl.num_programs(0) - 1)
    def _():
        o_ref[...] = x_ref[...] + acc_scr[...]


def _ffn_step(x, g, wg, wu, cw, cb, wd, s0, s1, tc):
    n, d = x.shape
    dff = wg.shape[1]
    return pl.pallas_call(
        _ffn_step_kernel,
        grid=(dff // tc,),
        in_specs=[
            pl.BlockSpec((n, d), lambda c: (0, 0)),
            pl.BlockSpec((1, d), lambda c: (0, 0)),
            pl.BlockSpec((d, tc), lambda c: (0, c)),
            pl.BlockSpec((d, tc), lambda c: (0, c)),
            pl.BlockSpec((3, tc), lambda c: (0, c)),
            pl.BlockSpec((1, tc), lambda c: (0, c)),
            pl.BlockSpec((tc, d), lambda c: (c, 0)),
            pl.BlockSpec((n, tc), lambda c: (0, c)),
            pl.BlockSpec((n, tc), lambda c: (0, c)),
        ],
        out_specs=[pl.BlockSpec((n, d), lambda c: (0, 0)),
                   pl.BlockSpec((n, tc), lambda c: (0, c))],
        out_shape=[jax.ShapeDtypeStruct((n, d), F32), jax.ShapeDtypeStruct((n, dff), F32)],
        scratch_shapes=[pltpu.VMEM((n, d), BF16), pltpu.VMEM((n, d), F32)],
        compiler_params=_params("arbitrary"),
    )(x, g.reshape(1, d), wg, wu, cw, cb.reshape(1, dff), wd, s0, s1)


def _out_ab_kernel(oa_ref, ob_ref, w_ref, x_ref, o_ref):
    ka = oa_ref.shape[1]
    o_ref[...] = (x_ref[...] + _dot(oa_ref[...].astype(BF16), w_ref[0:ka, :])
                  + _dot(ob_ref[...].astype(BF16), w_ref[ka:, :]))


def _out_ab(oa, ob, w_bf, x, tm):
    n, d = x.shape
    ka, kb = oa.shape[1], ob.shape[1]
    return pl.pallas_call(
        _out_ab_kernel,
        grid=(n // tm,),
        in_specs=[pl.BlockSpec((tm, ka), lambda i: (i, 0)),
                  pl.BlockSpec((tm, kb), lambda i: (i, 0)),
                  pl.BlockSpec((ka + kb, d), lambda i: (0, 0)),
                  pl.BlockSpec((tm, d), lambda i: (i, 0))],
        out_specs=pl.BlockSpec((tm, d), lambda i: (i, 0)),
        out_shape=jax.ShapeDtypeStruct((n, d), F32),
        compiler_params=_params("parallel"),
    )(oa, ob, w_bf, x)


def _merge_groups(os, ls):
    m = jnp.maximum(jnp.maximum(ls[0], ls[1]), ls[2])
    es = [jnp.exp(l - m) for l in ls]
    den = es[0] + es[1] + es[2]
    return (es[0] / den) * os[0] + (es[1] / den) * os[1] + (es[2] / den) * os[2]


def _out_c_kernel(o0, o1, o2, l0, l1, l2, w_ref, x_ref, o_ref):
    merged = jnp.concatenate(
        [_merge_groups([o0[0, hp], o1[0, hp], o2[0, hp]], [l0[0, hp], l1[0, hp], l2[0, hp]])
         for hp in range(HEADS // 2)], axis=1)
    o_ref[...] = x_ref[...] + _dot(merged.astype(BF16), w_ref[...])


def _out_c(os, ls, w_bf, x, tm):
    n, d = x.shape
    k = w_bf.shape[0]
    per_seq = os[0].shape[2] // tm
    act = pl.BlockSpec((1, HEADS // 2, tm, LANES), lambda i: (i // per_seq, 0, i % per_seq, 0))
    return pl.pallas_call(
        _out_c_kernel,
        grid=(n // tm,),
        in_specs=[act] * 6 + [pl.BlockSpec((k, d), lambda i: (0, 0)),
                              pl.BlockSpec((tm, d), lambda i: (i, 0))],
        out_specs=pl.BlockSpec((tm, d), lambda i: (i, 0)),
        out_shape=jax.ShapeDtypeStruct((n, d), F32),
        compiler_params=_params("parallel"),
    )(*os, *ls, w_bf, x)


def _out_ab_t_kernel(oat_ref, ob_ref, w_ref, x_ref, o_ref):
    ka = oat_ref.shape[0]
    o_ref[...] = (x_ref[...] + _dot(oat_ref[...].T.astype(BF16), w_ref[0:ka, :])
                  + _dot(ob_ref[...].astype(BF16), w_ref[ka:, :]))


def _out_ab_t(oa_t, ob, w_bf, x):
    return pl.pallas_call(
        _out_ab_t_kernel,
        out_shape=jax.ShapeDtypeStruct(x.shape, F32),
        compiler_params=pltpu.CompilerParams(vmem_limit_bytes=VMEM_LIMIT),
    )(oa_t, ob, w_bf, x)


def _out_c_t_kernel(o0, o1, o2, l0, l1, l2, w_ref, x_ref, o_ref):
    merged = _merge_groups([o0[...], o1[...], o2[...]], [l0[...], l1[...], l2[...]])
    o_ref[...] = x_ref[...] + _dot(merged.T.astype(BF16), w_ref[...])


def _out_c_t(os_t, ls_t, w_bf, x):
    return pl.pallas_call(
        _out_c_t_kernel,
        out_shape=jax.ShapeDtypeStruct(x.shape, F32),
        compiler_params=pltpu.CompilerParams(vmem_limit_bytes=VMEM_LIMIT),
    )(*os_t, *ls_t, w_bf, x)


def _top_k_mask(g, index, k, axis):
    sel = jnp.zeros(g.shape, F32)
    index_f = index.astype(F32)
    for _ in range(k):
        m = jnp.max(g, axis=axis, keepdims=True)
        first = jnp.min(jnp.where(g == m, index_f, float(g.shape[axis])), axis=axis, keepdims=True)
        hit = index_f == first
        sel = jnp.where(hit, 1.0, sel)
        g = jnp.where(hit, LOWEST, g)
    return sel


def _moba_prep_kernel(q_ref, k_ref, v_ref, qa_ref, ka_ref, va_ref, kmean_scr, *, nb_pad):
    i = pl.program_id(1)

    @pl.when(i == 0)
    def _():
        kmean_scr[...] = jnp.zeros(kmean_scr.shape, F32)

    rows = q_ref.shape[1]
    lane = lax.broadcasted_iota(jnp.int32, (rows, LANES), 1)
    low = lane < DH
    blk = lax.broadcasted_iota(jnp.int32, (nb_pad, rows), 0)
    past = blk < i
    for hp in range(HEADS // 2):
        cols = slice(hp * LANES, (hp + 1) * LANES)
        q2 = q_ref[0, :, cols]
        k2 = k_ref[0, :, cols]
        v2 = v_ref[0, :, cols]
        km2 = kmean_scr[0:nb_pad, cols]
        for e in range(2):
            mine = low if e == 0 else jnp.logical_not(low)
            gate = _dot_t(km2, jnp.where(mine, q2, 0.0), precision=HIGHEST)
            sel = _top_k_mask(jnp.where(past, gate, NEG_INF), blk, MOBA_TOPK, 0)
            keep = jnp.where(past, sel, 0.0) + jnp.where(blk == i, 1.0, 0.0)
            bias = jnp.where(keep > 0.5, 0.0, NEG_INF)
            bias = jnp.concatenate([bias, jnp.zeros((LANES - nb_pad, rows), F32)], axis=0).T
            onehot = jnp.where(lane == i, 1.0, 0.0)
            if e == 0:
                bias = pltpu.roll(bias, DH, axis=1)
                onehot = jnp.where(lane == i + DH, 1.0, 0.0)
                ones_col = jnp.where(lane == DH, 1.0, 0.0)
            else:
                ones_col = jnp.where(lane == 0, 1.0, 0.0)
            h = 2 * hp + e
            qa_ref[0, h] = jnp.where(mine, q2 * SCALE, bias).astype(BF16)
            ka_ref[0, h] = jnp.where(mine, k2, onehot).astype(BF16)
            va_ref[0, h] = jnp.where(mine, v2, ones_col).astype(BF16)

    kmean_scr[pl.ds(i, 1), :] = jnp.mean(k_ref[0], axis=0, keepdims=True)


def _moba_prep(proj):
    b, s, _ = proj.shape
    nb = s // MOBA_BLOCK
    nb_pad = -(-nb // 8) * 8
    assert nb_pad <= DH
    blk = lambda j: pl.BlockSpec((1, MOBA_BLOCK, W_ATT), lambda bi, i: (bi, i, j))
    aug = pl.BlockSpec((1, HEADS, MOBA_BLOCK, LANES), lambda bi, i: (bi, 0, i, 0))
    shape = jax.ShapeDtypeStruct((b, HEADS, s, LANES), BF16)
    return pl.pallas_call(
        functools.partial(_moba_prep_kernel, nb_pad=nb_pad),
        grid=(b, nb),
        in_specs=[blk(0), blk(1), blk(2)],
        out_specs=[aug, aug, aug],
        out_shape=[shape, shape, shape],
        scratch_shapes=[pltpu.VMEM((LANES, W_ATT), F32)],
        compiler_params=_params("arbitrary", "arbitrary"),
    )(proj, proj, proj)


def _moba_attn_kernel(q_ref, k_ref, v_ref, o_ref, m_scr, acc_scr, *, tq):
    qi = pl.program_id(2)
    tk = MOBA_BLOCK
    n_diag = tq // tk
    row = lax.broadcasted_iota(jnp.int32, (tq, tk), 0)
    col = lax.broadcasted_iota(jnp.int32, (tq, tk), 1)
    lane = lax.broadcasted_iota(jnp.int32, (tq, LANES), 1)
    hs = q_ref.shape[1]

    def step(n, causal_offset=None):
        st = pl.multiple_of(n * tk, tk)
        for e in range(hs):
            s = _dot_t(q_ref[0, e], k_ref[0, e, pl.ds(st, tk), :])
            if causal_offset is not None:
                s = jnp.where(col + causal_offset <= row, s, NEG_INF)
            m = m_scr[e]
            m_new = jnp.maximum(m, jnp.max(s, axis=1, keepdims=True))
            m_scr[e] = m_new
            p = jnp.exp(s - jnp.concatenate([m_new] * (tk // LANES), axis=1)).astype(BF16)
            acc_scr[e] = (jnp.exp(m - m_new) * acc_scr[e]
                          + _dot(p, v_ref[0, e, pl.ds(st, tk), :]))

    m_scr[...] = jnp.full(m_scr.shape, LOWEST, F32)
    acc_scr[...] = jnp.zeros(acc_scr.shape, F32)
    for j in range(n_diag):
        step(qi * n_diag + j, causal_offset=j * tk)

    def body(n, carry):
        step(n)
        return carry

    lax.fori_loop(0, qi * n_diag, body, 0)
    for e in range(0, hs, 2):
        even, odd = acc_scr[e], acc_scr[e + 1]
        o_ref[0, :, e // 2 * LANES:(e // 2 + 1) * LANES] = jnp.where(
            lane < DH, even / even[:, DH:DH + 1], odd / odd[:, 0:1])


def _moba_attn(qa, ka, va, tq, hs):
    b, h, s, _ = qa.shape
    return pl.pallas_call(
        functools.partial(_moba_attn_kernel, tq=tq),
        grid=(b, h // hs, s // tq),
        in_specs=[pl.BlockSpec((1, hs, tq, LANES), lambda bi, hp, i: (bi, hp, i, 0)),
                  pl.BlockSpec((1, hs, s, LANES), lambda bi, hp, i: (bi, hp, 0, 0),
                               pipeline_mode=pl.Buffered(1)),
                  pl.BlockSpec((1, hs, s, LANES), lambda bi, hp, i: (bi, hp, 0, 0),
                               pipeline_mode=pl.Buffered(1))],
        out_specs=pl.BlockSpec((1, tq, hs // 2 * LANES), lambda bi, hp, i: (bi, i, hp)),
        out_shape=jax.ShapeDtypeStruct((b, s, W_ATT), F32),
        scratch_shapes=[pltpu.VMEM((hs, tq, LANES), F32), pltpu.VMEM((hs, tq, LANES), F32)],
        compiler_params=_params("parallel", "parallel", "arbitrary"),
    )(qa, ka, va)


def _column(ref, rows, lane_is_seq):
    return jnp.sum(jnp.where(lane_is_seq, ref[rows, :], 0.0), axis=1, keepdims=True)


def _zero_refs(out_refs):
    for ref in out_refs:
        ref[...] = jnp.zeros(ref.shape, F32)


def _moba_dec_body(bi, _, in_refs, out_refs, *, n_pages, page):
    q_ref, kn_ref, vn_ref = in_refs[:3]
    k_refs, v_refs = in_refs[3:3 + n_pages], in_refs[3 + n_pages:]
    o_ref, = out_refs
    pages_per_block = MOBA_BLOCK // page
    n_past = n_pages // pages_per_block
    lane_is_seq = lax.broadcasted_iota(jnp.int32, (DH, LANES), 1) == bi
    for h in range(HEADS):
        rows = slice(h * DH, (h + 1) * DH)
        qc = _column(q_ref, rows, lane_is_seq)
        kc = _column(kn_ref, rows, lane_is_seq)
        vc = _column(vn_ref, rows, lane_is_seq)
        raw = [jnp.sum(k_refs[j][0, h] * qc, axis=0, keepdims=True) for j in range(n_pages)]
        gates = []
        for n in range(n_past):
            tot = raw[n * pages_per_block]
            for j in range(1, pages_per_block):
                tot = tot + raw[n * pages_per_block + j]
            gates.append(jnp.sum(tot, axis=1, keepdims=True) * (1.0 / MOBA_BLOCK))
        sel = [jnp.zeros((1, 1), F32)] * n_past
        for _ in range(min(MOBA_TOPK, n_past + 1)):
            best = functools.reduce(jnp.maximum, gates)
            free = jnp.ones((1, 1), F32)
            for n in range(n_past):
                hit = jnp.where(gates[n] == best, free, 0.0)
                free = free - hit
                sel[n] = jnp.maximum(sel[n], hit)
                gates[n] = jnp.where(hit > 0.5, LOWEST, gates[n])
        s_self = jnp.sum(qc * kc, axis=0, keepdims=True) * SCALE
        s = [jnp.where(sel[j // pages_per_block] > 0.5, raw[j] * SCALE, NEG_INF)
             for j in range(n_pages)]
        m = s_self
        for sj in s:
            m = jnp.maximum(m, jnp.max(sj, axis=1, keepdims=True))
        p_self = jnp.exp(s_self - m)
        l = p_self
        acc = jnp.zeros((DH, page), F32)
        for j in range(n_pages):
            pj = jnp.exp(s[j] - m)
            l = l + jnp.sum(pj, axis=1, keepdims=True)
            acc = acc + v_refs[j][0, h] * pj
        o = (jnp.sum(acc, axis=1, keepdims=True) + p_self * vc) / l
        o_ref[rows, :] = jnp.where(lane_is_seq, o, o_ref[rows, :])


def _moba_dec_guest(proj_t, cache_k, cache_v, page_table):
    b = proj_t.shape[1]
    n_pages = page_table.shape[1]
    page = cache_k.shape[3]
    assert b == LANES
    vec = lambda j: ((W_ATT, b), lambda bi, pt: (j, 0))
    pg = lambda j: ((1, HEADS, DH, page), lambda bi, pt: (pt[bi * n_pages + j], 0, 0, 0))
    pages = [pg(j) for j in range(n_pages)]
    return _Guest(
        in_specs=[vec(0), vec(1), vec(2)] + pages + pages,
        args=[proj_t] * 3 + [cache_k] * n_pages + [cache_v] * n_pages,
        out_specs=[((W_ATT, b), lambda bi, pt: (0, 0))],
        out_shape=[jax.ShapeDtypeStruct((W_ATT, b), F32)],
        init=_zero_refs,
        body=functools.partial(_moba_dec_body, n_pages=n_pages, page=page),
        prefetch=page_table.reshape(-1))


def _hgrn_seq_kernel(q_ref, f_ref, i_ref, g_ref, lb_ref, gn_ref, o_ref, s_ref, st_scr, *, tb):
    t = pl.program_id(0)

    @pl.when(t == 0)
    def _():
        st_scr[...] = jnp.zeros(st_scr.shape, F32)

    c = HGRN_CHUNK
    nch = tb // c
    w = q_ref.shape[2]
    lb = lb_ref[...]
    gn = gn_ref[...]
    r_i = lax.broadcasted_iota(jnp.int32, (tb, tb), 0)
    c_i = lax.broadcasted_iota(jnp.int32, (tb, tb), 1)
    tril = jnp.logical_and(c_i <= r_i, c_i // c == r_i // c)
    ones_tril = jnp.where(tril, 1.0, 0.0)
    per_chunk = lambda x, row: jnp.broadcast_to(
        x.reshape(nch, c, w)[:, row:row + 1, :], (nch, c, w)).reshape(tb, w)
    for bi in range(q_ref.shape[0]):
        f = lb + (1.0 - lb) * _sigmoid(f_ref[bi])
        k = 1.0 - f
        q = _silu(q_ref[bi])
        v = i_ref[bi]
        b = jnp.dot(ones_tril, jnp.log(f), precision=HIGHEST, preferred_element_type=F32)
        b_mid = per_chunk(b, c // 2 - 1)
        b_last = per_chunk(b, c - 1)
        q_mid = (q * jnp.exp(b - b_mid)).astype(BF16)
        k_mid = (k * jnp.exp(b_mid - b)).astype(BF16)
        k_end = (k * jnp.exp(b_last - b)).astype(BF16)
        q_in = (q * jnp.exp(b)).astype(BF16)
        decay = jnp.exp(b_last)
        vb = v.astype(BF16)
        outs = []
        for h in range(H_B):
            cols = slice(h * D_B, (h + 1) * D_B)
            a = jnp.where(tril, _dot_t(q_mid[:, cols], k_mid[:, cols]), 0.0)
            o_intra = _dot(a.astype(BF16), vb[:, cols])
            o_inter = []
            st = st_scr[bi, h]
            for ch in range(nch):
                rows = slice(ch * c, (ch + 1) * c)
                o_inter.append(_dot_t(q_in[rows, cols], st.astype(BF16)))
                st = (st * decay[ch * c:ch * c + 1, cols]
                      + _dot(v[rows, cols].T.astype(BF16), k_end[rows, cols]))
            st_scr[bi, h] = st
            outs.append(_rms(o_intra + jnp.concatenate(o_inter, axis=0), gn[:, cols]))
        o_ref[bi] = jnp.concatenate(outs, axis=1) * _silu(g_ref[bi])

    @pl.when(t == pl.num_programs(0) - 1)
    def _():
        for bi in range(q_ref.shape[0]):
            for h in range(H_B):
                s_ref[bi, h] = st_scr[bi, h].T


def _hgrn_seq(proj, lb, gn, first_chunk, tb):
    b, s, _ = proj.shape
    w = H_B * D_B
    blk = lambda j: pl.BlockSpec((b, tb, w), lambda t: (0, t, first_chunk + j))
    vec = pl.BlockSpec((1, w), lambda t: (0, 0))
    return pl.pallas_call(
        functools.partial(_hgrn_seq_kernel, tb=tb),
        grid=(s // tb,),
        in_specs=[blk(0), blk(1), blk(2), blk(3), vec, vec],
        out_specs=[pl.BlockSpec((b, tb, w), lambda t: (0, t, 0)),
                   pl.BlockSpec((b, H_B, D_B, D_B), lambda t: (0, 0, 0, 0))],
        out_shape=[jax.ShapeDtypeStruct((b, s, w), F32),
                   jax.ShapeDtypeStruct((b, H_B, D_B, D_B), F32)],
        scratch_shapes=[pltpu.VMEM((b, H_B, D_B, D_B), F32)],
        compiler_params=_params("arbitrary"),
    )(proj, proj, proj, proj, lb.reshape(1, w), jnp.tile(gn.astype(F32), H_B).reshape(1, w))


def _hgrn_step_kernel(q_ref, f_ref, i_ref, g_ref, lb_ref, gn_ref, s0_ref, o_ref, s_ref):
    bi = pl.program_id(0)
    lb = lb_ref[...]
    gn = gn_ref[...]
    f = lb + (1.0 - lb) * _sigmoid(f_ref[pl.ds(bi, 1), :])
    q = _silu(q_ref[pl.ds(bi, 1), :])
    v = i_ref[pl.ds(bi, 1), :]
    eye = (lax.broadcasted_iota(jnp.int32, (D_B, D_B), 0)
           == lax.broadcasted_iota(jnp.int32, (D_B, D_B), 1))
    outs = []
    for h in range(H_B):
        cols = slice(h * D_B, (h + 1) * D_B)
        f_col = jnp.sum(jnp.where(eye, f[:, cols], 0.0), axis=1, keepdims=True)
        q_col = jnp.sum(jnp.where(eye, q[:, cols], 0.0), axis=1, keepdims=True)
        s_new = f_col * s0_ref[0, h] + (1.0 - f_col) * v[:, cols]
        s_ref[0, h] = s_new
        o = jnp.sum(q_col * s_new, axis=0, keepdims=True)
        outs.append(_rms(o, gn[:, cols]))
    o_ref[0] = jnp.concatenate(outs, axis=1) * _silu(g_ref[pl.ds(bi, 1), :])


def _hgrn_step(proj, lb, gn, s0, first_chunk):
    b = proj.shape[0]
    w = H_B * D_B
    blk = lambda j: pl.BlockSpec((b, w), lambda bi: (0, first_chunk + j))
    vec = pl.BlockSpec((1, w), lambda bi: (0, 0))
    st = pl.BlockSpec((1, H_B, D_B, D_B), lambda bi: (bi, 0, 0, 0))
    o, s = pl.pallas_call(
        _hgrn_step_kernel,
        grid=(b,),
        in_specs=[blk(0), blk(1), blk(2), blk(3), vec, vec, st],
        out_specs=[pl.BlockSpec((1, 1, w), lambda bi: (bi, 0, 0)), st],
        out_shape=[jax.ShapeDtypeStruct((b, 1, w), F32),
                   jax.ShapeDtypeStruct((b, H_B, D_B, D_B), F32)],
        compiler_params=_params("parallel"),
    )(proj, proj, proj, proj, lb.reshape(1, w),
      jnp.tile(gn.astype(F32), H_B).reshape(1, w), s0)
    return o.reshape(b, w), s


def _band_kernel(q_ref, kc_ref, kp_ref, vc_ref, vp_ref, o_ref, l_ref, *, tq, dil):
    first = pl.program_id(1) == 0
    r = pl.program_id(2)
    row = lax.broadcasted_iota(jnp.int32, (BAND, 2 * BAND), 0)
    col = lax.broadcasted_iota(jnp.int32, (BAND, 2 * BAND), 1)
    band = jnp.logical_and(col >= row, col <= row + BAND)
    lane = lax.broadcasted_iota(jnp.int32, (BAND, LANES), 1)
    low = lane < DH
    for i in range(tq // BAND):
        rows = slice(i * BAND, (i + 1) * BAND)
        if i == 0:
            k_prev, v_prev = kp_ref[0, 0], vp_ref[0, 0]
            mask = jnp.logical_and(band, jnp.logical_or(col >= BAND, jnp.logical_not(first)))
        else:
            prev = slice((i - 1) * BAND, i * BAND)
            k_prev, v_prev = kc_ref[0, 0, prev, :], vc_ref[0, 0, prev, :]
            mask = band
        kk = jnp.concatenate([k_prev, kc_ref[0, 0, rows, :]], axis=0).astype(BF16)
        vv = jnp.concatenate([v_prev, vc_ref[0, 0, rows, :]], axis=0).astype(BF16)
        out_rows = rows if dil == 1 else pl.ds(r + i * BAND * dil, BAND, stride=dil)
        for hp in range(HEADS // 2):
            cols = slice(hp * LANES, (hp + 1) * LANES)
            q2 = q_ref[0, 0, rows, cols] * SCALE
            res = []
            for e in range(2):
                mine = low if e == 0 else jnp.logical_not(low)
                s = _dot_t(jnp.where(mine, q2, 0.0).astype(BF16), kk[:, cols])
                s = jnp.where(mask, s, NEG_INF)
                m = jnp.max(s, axis=1, keepdims=True)
                p = jnp.exp(s - m)
                l = jnp.sum(p, axis=1, keepdims=True)
                res.append((_dot((p / l).astype(BF16), vv[:, cols]), m + jnp.log(l)))
            o_ref[0, hp, out_rows, :] = jnp.where(low, res[0][0], res[1][0])
            l_ref[0, hp, out_rows, :] = jnp.where(low, res[0][1], res[1][1])


def _band_attention(proj, tq):
    b, dil, ln, _ = proj.shape
    sub = tq // BAND
    cur = lambda comp: pl.BlockSpec((1, 1, tq, W_ATT), lambda bi, l, r: (bi, r, l, comp))
    prev = lambda comp: pl.BlockSpec((1, 1, BAND, W_ATT),
                                     lambda bi, l, r: (bi, r, jnp.maximum(l * sub - 1, 0), comp))
    out = pl.BlockSpec((1, HEADS // 2, tq * dil, LANES), lambda bi, l, r: (bi, 0, l, 0))
    shape = jax.ShapeDtypeStruct((b, HEADS // 2, ln * dil, LANES), F32)
    return pl.pallas_call(
        functools.partial(_band_kernel, tq=tq, dil=dil),
        grid=(b, ln // tq, dil),
        in_specs=[cur(0), cur(1), prev(1), cur(2), prev(2)],
        out_specs=[out, out],
        out_shape=[shape, shape],
        compiler_params=_params("parallel", "arbitrary", "arbitrary"),
    )(proj, proj, proj, proj, proj)


def _dil_cache_body(step, _, in_refs, out_refs, *, dils, hb, head_groups):
    bi, hg = step // head_groups, step % head_groups
    for g, dil in enumerate(dils):
        _dil_cache_group(bi, hg, *in_refs[5 * g:5 * g + 5], *out_refs[4 * g:4 * g + 4],
                         dil=dil, hb=hb)


def _dil_cache_init(out_refs):
    for g in range(len(out_refs) // 4):
        _zero_refs(out_refs[4 * g + 2:4 * g + 4])


def _dil_cache_kernel(*refs, dils, hb):
    n_in = 5 * len(dils)

    @pl.when(jnp.logical_and(pl.program_id(0) == 0, pl.program_id(1) == 0))
    def _():
        _dil_cache_init(refs[n_in:])

    _dil_cache_body(pl.program_id(0) * pl.num_programs(1) + pl.program_id(1), None,
                    refs[:n_in], refs[n_in:], dils=dils, hb=hb, head_groups=HEADS // hb)


def _dil_cache_group(bi, hg, q_ref, kn_ref, vn_ref, ck_ref, cv_ref, ok_ref, ov_ref, o_ref, l_ref,
                     *, dil, hb):
    lw = ck_ref.shape[3]
    nr = hb * DH
    rows = pl.ds(pl.multiple_of(hg * nr, nr), nr)
    lane_is_seq = lax.broadcasted_iota(jnp.int32, (nr, LANES), 1) == bi
    pos = lax.broadcasted_iota(jnp.int32, (hb, lw), 1)
    on_grid = (pos & (dil - 1)) == 0
    newest = lax.broadcasted_iota(jnp.int32, (DH, lw), 1) == lw - 1
    qc = _column(q_ref, rows, lane_is_seq) * SCALE
    kc = _column(kn_ref, rows, lane_is_seq)
    vc = _column(vn_ref, rows, lane_is_seq)
    head = lambda x, hh: x[hh * DH:(hh + 1) * DH]
    s = jnp.concatenate([jnp.sum(ck_ref[0, hh] * head(qc, hh), axis=0, keepdims=True)
                         for hh in range(hb)], axis=0)
    s = jnp.where(on_grid, s, NEG_INF)
    qk = qc * kc
    s_self = jnp.concatenate([jnp.sum(head(qk, hh), axis=0, keepdims=True)
                              for hh in range(hb)], axis=0)
    m = jnp.maximum(jnp.max(s, axis=1, keepdims=True), s_self)
    p = jnp.exp(s - m)
    p_self = jnp.exp(s_self - m)
    l = jnp.sum(p, axis=1, keepdims=True) + p_self
    pn = p / l
    w_self = p_self / l
    lse = m + jnp.log(l)
    os, ls = [], []
    for hh in range(hb):
        k = ck_ref[0, hh]
        v = cv_ref[0, hh]
        os.append(jnp.sum(v * pn[hh:hh + 1, :], axis=1, keepdims=True)
                  + w_self[hh:hh + 1, :] * head(vc, hh))
        ls.append(jnp.broadcast_to(lse[hh:hh + 1, :], (DH, 1)))
        ok_ref[0, hh] = jnp.where(newest, head(kc, hh), pltpu.roll(k, lw - 1, axis=1))
        ov_ref[0, hh] = jnp.where(newest, head(vc, hh), pltpu.roll(v, lw - 1, axis=1))
    o_ref[rows, :] = jnp.where(lane_is_seq, jnp.concatenate(os, axis=0), o_ref[rows, :])
    l_ref[rows, :] = jnp.where(lane_is_seq, jnp.concatenate(ls, axis=0), l_ref[rows, :])


def _dil_cache_guest(proj_t, n_groups, groups, dils, caches, hb):
    b, h, dh, _ = caches[0][0].shape
    hgs = h // hb
    assert b == LANES and all(dil & (dil - 1) == 0 for dil in dils)
    vec = lambda j: ((W_ATT, b), lambda step, *_: (j, 0))
    res = ((W_ATT, b), lambda step, *_: (0, 0))
    res_shape = jax.ShapeDtypeStruct((W_ATT, b), F32)
    in_specs, out_specs, out_shape, args = [], [], [], []
    for g, (ck, cv) in zip(groups, caches):
        cache = ((1, hb, dh, ck.shape[3]), lambda step, *_: (step // hgs, step % hgs, 0, 0))
        in_specs += [vec(g), vec(n_groups + g), vec(2 * n_groups + g), cache, cache]
        args += [proj_t, proj_t, proj_t, ck, cv]
        out_specs += [cache, cache, res, res]
        out_shape += [jax.ShapeDtypeStruct(ck.shape, F32), jax.ShapeDtypeStruct(cv.shape, F32),
                      res_shape, res_shape]
    return _Guest(in_specs, args, out_specs, out_shape, _dil_cache_init,
                  functools.partial(_dil_cache_body, dils=tuple(dils), hb=hb, head_groups=hgs))


def _dil_cache(guest, b, head_groups):
    by_step = lambda f: (lambda bi, hg: f(bi * head_groups + hg))
    return pl.pallas_call(
        functools.partial(_dil_cache_kernel, dils=guest.body.keywords["dils"],
                          hb=guest.body.keywords["hb"]),
        grid=(b, head_groups),
        in_specs=[pl.BlockSpec(shape, by_step(f)) for shape, f in guest.in_specs],
        out_specs=[pl.BlockSpec(shape, by_step(f)) for shape, f in guest.out_specs],
        out_shape=guest.out_shape,
        compiler_params=_params("arbitrary", "arbitrary"),
    )(*guest.args)


def kernel(x_prompt, x_sample, cache_k_a, cache_v_a, state_hgrn, cache_k_c0, cache_v_c0, cache_k_c1, cache_v_c1, cache_k_c2, cache_v_c2, state_conv, page_table, norm_mix, norm_ffn, w_in_ab, q_norm_a, k_norm_a, lb_logits, g_norm_b, w_out_ab, w_in_c, q_norm_c, k_norm_c, w_out_c, w_gate, w_up, conv_w, conv_b, w_down):
    bsz, seq, d = x_prompt.shape
    nb = x_sample.shape[0]
    n = bsz * seq
    dff = w_gate.shape[2]
    tm, tc = 512, dff // 2
    ng = len(C_PATTERNS)
    c_caches = ((cache_k_c0, cache_v_c0), (cache_k_c1, cache_v_c1), (cache_k_c2, cache_v_c2))

    lb_all = jnp.cumsum(jax.nn.softmax(lb_logits.astype(F32), axis=0), axis=0)
    bf = lambda w: w.astype(BF16)

    xp = x_prompt.reshape(n, d)
    xs = x_sample.reshape(nb, d)

    w_in = bf(w_in_ab[0])
    gains = _head_gains([q_norm_a[0], k_norm_a[0]])
    w_out = bf(w_out_ab[0])
    lb = lb_all[0]

    pp = _norm_proj(xp, norm_mix[0], w_in, gains, tm).reshape(bsz, seq, -1)
    qa, ka, va = _moba_prep(pp)
    oa = _moba_attn(qa, ka, va, 512, HEADS).reshape(n, W_ATT)
    ob, hg_p = _hgrn_seq(pp, lb, g_norm_b[0], 3, 256)
    xp = _out_ab(oa, ob.reshape(n, -1), w_out, xp, tm)
    k_a_p = pp[:, :, W_ATT:2 * W_ATT].reshape(1, bsz, seq, HEADS, DH)
    v_a_p = pp[:, :, 2 * W_ATT:3 * W_ATT].reshape(1, bsz, seq, HEADS, DH)

    pos_minor = lambda c: jnp.transpose(c, (0, 2, 3, 1))
    pos_major = lambda c: jnp.transpose(c, (0, 3, 1, 2))

    conv_p, conv_s = [], []
    tm_ffn, nc_ffn = 256, 2

    def ffn(l, xp, guest):
        wg, wu, wd = bf(w_gate[l]), bf(w_up[l]), bf(w_down[l])
        xp, cp, guest_out = _ffn_seq(xp, norm_ffn[l], wg, wu, conv_w[l], conv_b[l], wd,
                                     bsz, tm_ffn, nc_ffn, guest)

        def sample(xs):
            xs, a_s = _ffn_step(xs, norm_ffn[l], wg, wu, conv_w[l], conv_b[l], wd,
                                state_conv[l, :, 0, :], state_conv[l, :, 1, :], tc)
            conv_s.append(jnp.stack([state_conv[l, :, 1, :], a_s], axis=1))
            return xs

        conv_p.append(cp)
        return xp, sample, guest_out

    ps = _norm_proj(xs, norm_mix[0], w_in, gains, nb)
    assert n // tm_ffn * nc_ffn == nb
    xp, ffn_sample, (oas_t,) = ffn(0, xp, _moba_dec_guest(
        ps.T, pos_minor(cache_k_a[0]), pos_minor(cache_v_a[0]), page_table))
    obs, hg_s = _hgrn_step(ps, lb, g_norm_b[0], state_hgrn[0], 3)
    xs = ffn_sample(_out_ab_t(oas_t, obs, w_out, xs))
    k_a_s = ps[:, W_ATT:2 * W_ATT].reshape(1, nb, 1, HEADS, DH)
    v_a_s = ps[:, 2 * W_ATT:3 * W_ATT].reshape(1, nb, 1, HEADS, DH)

    w_in = bf(w_in_c[0])
    w_out = bf(w_out_c[0])
    gains = _head_gains([q_norm_c[0], k_norm_c[0]])

    os, ls, c_prompt = [], [], []
    for g, (win, dil) in enumerate(C_PATTERNS):
        ln = seq // dil
        w_g = jnp.concatenate([w_in[:, (c * ng + g) * W_ATT:(c * ng + g + 1) * W_ATT]
                               for c in range(3)], axis=1)
        xr = xp.reshape(bsz, ln, dil, d).transpose(0, 2, 1, 3).reshape(n, d)
        pg = _norm_proj(xr, norm_mix[1], w_g, gains, tm)
        o, lse = _band_attention(pg.reshape(bsz, dil, ln, 3 * W_ATT), min(512, 4096 // dil))
        os.append(o)
        ls.append(lse)
        keep = min(win, seq)
        kd = keep // dil
        tail = lax.optimization_barrier(lax.slice(
            pg.reshape(bsz * dil, ln, 3 * W_ATT), (0, ln - kd, W_ATT), (bsz * dil, ln, 3 * W_ATT)))
        tail = tail.reshape(bsz, dil, kd, 2, HEADS, DH).transpose(3, 0, 2, 1, 4, 5)
        tail = tail.reshape(2, 1, bsz, keep, HEADS, DH)
        c_prompt.append((tail[0], tail[1]))
    xp = _out_c(os, ls, w_out, xp, tm)

    gains = _head_gains([q_norm_c[0]] * ng + [k_norm_c[0]] * ng)
    ps_t = _norm_proj(xs, norm_mix[1], w_in, gains, nb).T
    dils = [dil for _, dil in C_PATTERNS]
    views = [(pos_minor(ck[0]), pos_minor(cv[0])) for ck, cv in c_caches]
    hosted = [g for g in range(ng) if C_PATTERNS[g][0] <= 512]
    alone = [g for g in range(ng) if g not in hosted]
    pick = lambda xs_, gs: [xs_[g] for g in gs]
    xp, ffn_sample, hosted_out = ffn(1, xp, _dil_cache_guest(
        ps_t, ng, hosted, pick(dils, hosted), pick(views, hosted), HEADS))
    alone_out = _dil_cache(_dil_cache_guest(ps_t, ng, alone, pick(dils, alone), pick(views, alone),
                                            HEADS // 2), nb, 2)
    decoded = dict(zip(hosted + alone,
                       [outs[4 * j:4 * j + 4] for outs in (hosted_out, alone_out)
                        for j in range(len(outs) // 4)]))
    c_out, os_t, ls_t = [], [], []
    for g in range(ng):
        nk, nv, o_t, l_t = decoded[g]
        c_out += list(c_prompt[g])
        c_out += [pos_major(nk)[None], pos_major(nv)[None]]
        os_t.append(o_t)
        ls_t.append(l_t)
    xs = ffn_sample(_out_c_t(os_t, ls_t, w_out, xs))

    return (xp.reshape(bsz, seq, d), xs.reshape(nb, 1, d),
            k_a_p, v_a_p, k_a_s, v_a_s, hg_p[None], hg_s[None],
            *c_out, jnp.stack(conv_p), jnp.stack(conv_s))
```

```python
import functools

import jax
import jax.numpy as jnp
from jax import lax
from jax.experimental import pallas as pl
from jax.experimental.pallas import tpu as pltpu

F32 = jnp.float32
BF16 = jnp.bfloat16
HIGHEST = lax.Precision.HIGHEST

RMS_EPS = 1e-6
NEG_INF = -1e30
LOWEST = -3e38

LANES = 128
DH = 64
HEADS = 8
W_ATT = HEADS * DH
SCALE = DH ** -0.5
MOBA_BLOCK = 256
MOBA_TOPK = 3
H_B = 4
D_B = 128
HGRN_CHUNK = 32
BAND = 128
C_PATTERNS = ((128, 1), (512, 4), (2048, 16))
VMEM_LIMIT = 48 * 1024 * 1024
VMEM_LIMIT_HOSTED = 56 * 1024 * 1024

_CONTRACT_LAST = (((1,), (1,)), ((), ()))


def _params(*sem, vmem=VMEM_LIMIT):
    return pltpu.CompilerParams(dimension_semantics=sem, vmem_limit_bytes=vmem)


def _sigmoid(x):
    return 1.0 / (1.0 + jnp.exp(-x))


def _silu(x):
    return x * _sigmoid(x)


def _rms(x, g):
    return x * lax.rsqrt(jnp.mean(x * x, axis=-1, keepdims=True) + RMS_EPS) * g


def _dot(a, b):
    return jnp.dot(a, b, preferred_element_type=F32)


def _dot_t(a, b, precision=None):
    return lax.dot_general(a, b, _CONTRACT_LAST, precision=precision, preferred_element_type=F32)


def _norm_proj_kernel(x_ref, g_ref, w_ref, hg_ref, p_ref, o_ref, *t_refs, n_norm, transposed):
    h = _rms(x_ref[...], g_ref[...]).astype(BF16)
    for j in range(w_ref.shape[1] // W_ATT):
        cols = slice(j * W_ATT, (j + 1) * W_ATT)
        y = _dot(h, w_ref[:, cols])
        if j < n_norm:
            ms = _dot((y * y).astype(BF16), p_ref[...])
            y = y * lax.rsqrt(ms + RMS_EPS) * hg_ref[j:j + 1, :]
        o_ref[:, cols] = y
        if j in transposed:
            t_refs[transposed.index(j)][0] = y.T


def _norm_proj(x, g, w_bf, head_gains, tm, transposed=(), n_seq=1):
    n, d = x.shape
    m = w_bf.shape[1]
    n_norm = head_gains.shape[0]
    per_seq = n // n_seq // tm
    blockdiag = (jnp.arange(W_ATT)[:, None] // DH == jnp.arange(W_ATT)[None, :] // DH)
    p = (blockdiag.astype(F32) / DH).astype(BF16)
    t_spec = pl.BlockSpec((1, W_ATT, tm), lambda i: (i // per_seq, 0, i % per_seq))
    t_shape = jax.ShapeDtypeStruct((n_seq, W_ATT, n // n_seq), F32)
    outs = pl.pallas_call(
        functools.partial(_norm_proj_kernel, n_norm=n_norm, transposed=tuple(transposed)),
        grid=(n // tm,),
        in_specs=[
            pl.BlockSpec((tm, d), lambda i: (i, 0)),
            pl.BlockSpec((1, d), lambda i: (0, 0)),
            pl.BlockSpec((d, m), lambda i: (0, 0)),
            pl.BlockSpec((n_norm, W_ATT), lambda i: (0, 0)),
            pl.BlockSpec((W_ATT, W_ATT), lambda i: (0, 0)),
        ],
        out_specs=[pl.BlockSpec((tm, m), lambda i: (i, 0))] + [t_spec] * len(transposed),
        out_shape=[jax.ShapeDtypeStruct((n, m), F32)] + [t_shape] * len(transposed),
        compiler_params=_params("parallel"),
    )(x, g.reshape(1, d), w_bf, head_gains, p)
    return outs if transposed else outs[0]


def _head_gains(gains):
    return jnp.stack([jnp.tile(gv.astype(F32), HEADS) for gv in gains])


class _Guest:
    def __init__(self, in_specs, args, out_specs, out_shape, init, body, prefetch=None):
        self.in_specs, self.args = in_specs, args
        self.out_specs, self.out_shape = out_specs, out_shape
        self.init, self.body, self.prefetch = init, body, prefetch


def _ffn_seq_kernel(*refs, tm, blocks_per_seq, guest):
    refs = list(refs)
    pt_ref = refs.pop(0) if guest is not None and guest.prefetch is not None else None
    n_gi = len(guest.in_specs) if guest is not None else 0
    n_go = len(guest.out_specs) if guest is not None else 0
    x_ref, g_ref, wg_ref, wu_ref, cw_ref, cb_ref, wd_ref = refs[:7]
    guest_in = refs[7:7 + n_gi]
    o_ref, cs_ref = refs[7 + n_gi:9 + n_gi]
    guest_out = refs[9 + n_gi:9 + n_gi + n_go]
    a_scr, acc_scr = refs[9 + n_gi + n_go:]
    i = pl.program_id(0)
    c = pl.program_id(1)
    step = i * pl.num_programs(1) + c

    @pl.when(i % blocks_per_seq == 0)
    def _():
        a_scr[c, 0:8, :] = jnp.zeros((8, a_scr.shape[2]), F32)

    @pl.when(c == 0)
    def _():
        acc_scr[...] = jnp.zeros(acc_scr.shape, F32)

    if guest is not None:
        @pl.when(step == 0)
        def _():
            guest.init(guest_out)

    tc = a_scr.shape[2]
    cols = pl.ds(pl.multiple_of(c * tc, LANES), tc)
    h = _rms(x_ref[...], g_ref[...]).astype(BF16)
    a = _dot(h, wg_ref[:, cols])
    u = _dot(h, wu_ref[:, cols])
    a_scr[c, 8:8 + tm, :] = a
    a1 = a_scr[c, 7:7 + tm, :]
    a2 = a_scr[c, 6:6 + tm, :]
    cw = cw_ref[:, cols]
    conv = cb_ref[:, cols] + a2 * cw[0:1] + a1 * cw[1:2] + a * cw[2:3]
    y = (_silu(conv) * u).astype(BF16)
    acc = acc_scr[...] + _dot(y, wd_ref[c])
    acc_scr[...] = acc
    a_scr[c, 0:8, :] = a_scr[c, tm:tm + 8, :]
    cs_ref[0, c] = a_scr[c, tm + 6:tm + 8, :]
    o_ref[...] = x_ref[...] + acc
    if guest is not None:
        guest.body(step, pt_ref, guest_in, guest_out)


def _ffn_seq(x, g, wg, wu, cw, cb, wd, n_seq, tm, nc, guest=None):
    n, d = x.shape
    dff = wg.shape[1]
    tc = dff // nc
    bps = n // n_seq // tm
    resident = lambda shape: pl.BlockSpec(shape, lambda i, c, *_: (0,) * len(shape),
                                          pipeline_mode=pl.Buffered(1))
    in_specs = [
        pl.BlockSpec((tm, d), lambda i, c, *_: (i, 0)),
        resident((1, d)),
        resident((d, dff)), resident((d, dff)), resident((3, dff)), resident((1, dff)),
        resident((nc, tc, d)),
    ]
    args = [x, g.reshape(1, d), wg, wu, cw, cb.reshape(1, dff), wd.reshape(nc, tc, d)]
    out_specs = [pl.BlockSpec((tm, d), lambda i, c, *_: (i, 0)),
                 pl.BlockSpec((1, nc, 2, tc), lambda i, c, *_: (i // bps, 0, 0, 0))]
    out_shape = [jax.ShapeDtypeStruct((n, d), F32), jax.ShapeDtypeStruct((n_seq, nc, 2, tc), F32)]
    prefetch = []
    if guest is not None:
        by_step = lambda f: (lambda i, c, *pf: f(i * nc + c, *pf))
        in_specs += [pl.BlockSpec(shape, by_step(f)) for shape, f in guest.in_specs]
        out_specs += [pl.BlockSpec(shape, by_step(f)) for shape, f in guest.out_specs]
        args += guest.args
        out_shape += guest.out_shape
        prefetch = [] if guest.prefetch is None else [guest.prefetch]
    outs = pl.pallas_call(
        functools.partial(_ffn_seq_kernel, tm=tm, blocks_per_seq=bps, guest=guest),
        grid_spec=pltpu.PrefetchScalarGridSpec(
            num_scalar_prefetch=len(prefetch),
            grid=(n // tm, nc),
            in_specs=in_specs,
            out_specs=out_specs,
            scratch_shapes=[pltpu.VMEM((nc, tm + 8, tc), F32), pltpu.VMEM((tm, d), F32)],
        ),
        out_shape=out_shape,
        compiler_params=_params("arbitrary", "arbitrary", vmem=VMEM_LIMIT_HOSTED),
    )(*prefetch, *args)
    conv_state = outs[1].transpose(0, 2, 1, 3).reshape(n_seq, 2, dff)
    return outs[0], conv_state, outs[2:]


def _ffn_step_kernel(x_ref, g_ref, wg_ref, wu_ref, cw_ref, cb_ref, wd_ref, s0_ref, s1_ref,
                     o_ref, a_ref, h_scr, acc_scr):
    c = pl.program_id(0)

    @pl.when(c == 0)
    def _():
        h_scr[...] = _rms(x_ref[...], g_ref[...]).astype(BF16)

    h = h_scr[...]
    a = _dot(h, wg_ref[...])
    u = _dot(h, wu_ref[...])
    a_ref[...] = a
    cw = cw_ref[...]
    conv = cb_ref[...] + s0_ref[...] * cw[0:1] + s1_ref[...] * cw[1:2] + a * cw[2:3]
    y = (_silu(conv) * u).astype(BF16)
    part = _dot(y, wd_ref[...])

    @pl.when(c == 0)
    def _():
        acc_scr[...] = part

    @pl.when(c > 0)
    def _():
        acc_scr[...] = acc_scr[...] + part

    @pl.when(c == pl.num_programs(0) - 1)
    def _():
        o_ref[...] = x_ref[...] + acc_scr[...]


def _ffn_step(x, g, wg, wu, cw, cb, wd, s0, s1, tc):
    n, d = x.shape
    dff = wg.shape[1]
    return pl.pallas_call(
        _ffn_step_kernel,
        grid=(dff // tc,),
        in_specs=[
            pl.BlockSpec((n, d), lambda c: (0, 0)),
            pl.BlockSpec((1, d), lambda c: (0, 0)),
            pl.BlockSpec((d, tc), lambda c: (0, c)),
            pl.BlockSpec((d, tc), lambda c: (0, c)),
            pl.BlockSpec((3, tc), lambda c: (0, c)),
            pl.BlockSpec((1, tc), lambda c: (0, c)),
            pl.BlockSpec((tc, d), lambda c: (c, 0)),
            pl.BlockSpec((n, tc), lambda c: (0, c)),
            pl.BlockSpec((n, tc), lambda c: (0, c)),
        ],
        out_specs=[pl.BlockSpec((n, d), lambda c: (0, 0)),
                   pl.BlockSpec((n, tc), lambda c: (0, c))],
        out_shape=[jax.ShapeDtypeStruct((n, d), F32), jax.ShapeDtypeStruct((n, dff), F32)],
        scratch_shapes=[pltpu.VMEM((n, d), BF16), pltpu.VMEM((n, d), F32)],
        compiler_params=_params("arbitrary"),
    )(x, g.reshape(1, d), wg, wu, cw, cb.reshape(1, dff), wd, s0, s1)


def _out_ab_kernel(oa_ref, ob_ref, w_ref, x_ref, o_ref):
    ka = oa_ref.shape[1]
    o_ref[...] = (x_ref[...] + _dot(oa_ref[...].astype(BF16), w_ref[0:ka, :])
                  + _dot(ob_ref[...].astype(BF16), w_ref[ka:, :]))


def _out_ab(oa, ob, w_bf, x, tm):
    n, d = x.shape
    ka, kb = oa.shape[1], ob.shape[1]
    return pl.pallas_call(
        _out_ab_kernel,
        grid=(n // tm,),
        in_specs=[pl.BlockSpec((tm, ka), lambda i: (i, 0)),
                  pl.BlockSpec((tm, kb), lambda i: (i, 0)),
                  pl.BlockSpec((ka + kb, d), lambda i: (0, 0)),
                  pl.BlockSpec((tm, d), lambda i: (i, 0))],
        out_specs=pl.BlockSpec((tm, d), lambda i: (i, 0)),
        out_shape=jax.ShapeDtypeStruct((n, d), F32),
        compiler_params=_params("parallel"),
    )(oa, ob, w_bf, x)


def _merge_groups(os, ls):
    m = jnp.maximum(jnp.maximum(ls[0], ls[1]), ls[2])
    es = [jnp.exp(l - m) for l in ls]
    den = es[0] + es[1] + es[2]
    return (es[0] / den) * os[0] + (es[1] / den) * os[1] + (es[2] / den) * os[2]


def _out_c_kernel(o0, o1, o2, l0, l1, l2, w_ref, x_ref, o_ref):
    merged = jnp.concatenate(
        [_merge_groups([o0[0, hp], o1[0, hp], o2[0, hp]], [l0[0, hp], l1[0, hp], l2[0, hp]])
         for hp in range(HEADS // 2)], axis=1)
    o_ref[...] = x_ref[...] + _dot(merged.astype(BF16), w_ref[...])


def _out_c(os, ls, w_bf, x, tm):
    n, d = x.shape
    k = w_bf.shape[0]
    per_seq = os[0].shape[2] // tm
    act = pl.BlockSpec((1, HEADS // 2, tm, LANES), lambda i: (i // per_seq, 0, i % per_seq, 0))
    return pl.pallas_call(
        _out_c_kernel,
        grid=(n // tm,),
        in_specs=[act] * 6 + [pl.BlockSpec((k, d), lambda i: (0, 0)),
                              pl.BlockSpec((tm, d), lambda i: (i, 0))],
        out_specs=pl.BlockSpec((tm, d), lambda i: (i, 0)),
        out_shape=jax.ShapeDtypeStruct((n, d), F32),
        compiler_params=_params("parallel"),
    )(*os, *ls, w_bf, x)


def _out_ab_t_kernel(oat_ref, ob_ref, w_ref, x_ref, o_ref):
    ka = oat_ref.shape[0]
    o_ref[...] = (x_ref[...] + _dot(oat_ref[...].T.astype(BF16), w_ref[0:ka, :])
                  + _dot(ob_ref[...].astype(BF16), w_ref[ka:, :]))


def _out_ab_t(oa_t, ob, w_bf, x):
    return pl.pallas_call(
        _out_ab_t_kernel,
        out_shape=jax.ShapeDtypeStruct(x.shape, F32),
        compiler_params=pltpu.CompilerParams(vmem_limit_bytes=VMEM_LIMIT),
    )(oa_t, ob, w_bf, x)


def _out_c_t_kernel(o0, o1, o2, l0, l1, l2, w_ref, x_ref, o_ref):
    merged = _merge_groups([o0[...], o1[...], o2[...]], [l0[...], l1[...], l2[...]])
    o_ref[...] = x_ref[...] + _dot(merged.T.astype(BF16), w_ref[...])


def _out_c_t(os_t, ls_t, w_bf, x):
    return pl.pallas_call(
        _out_c_t_kernel,
        out_shape=jax.ShapeDtypeStruct(x.shape, F32),
        compiler_params=pltpu.CompilerParams(vmem_limit_bytes=VMEM_LIMIT),
    )(*os_t, *ls_t, w_bf, x)


def _top_k_mask(g, index, k, axis):
    sel = jnp.zeros(g.shape, F32)
    index_f = index.astype(F32)
    for _ in range(k):
        m = jnp.max(g, axis=axis, keepdims=True)
        first = jnp.min(jnp.where(g == m, index_f, float(g.shape[axis])), axis=axis, keepdims=True)
        hit = index_f == first
        sel = jnp.where(hit, 1.0, sel)
        g = jnp.where(hit, LOWEST, g)
    return sel


def _moba_prep_kernel(q_ref, k_ref, v_ref, qa_ref, ka_ref, va_ref, kmean_scr, *, nb_pad):
    i = pl.program_id(1)

    @pl.when(i == 0)
    def _():
        kmean_scr[...] = jnp.zeros(kmean_scr.shape, F32)

    rows = q_ref.shape[1]
    lane = lax.broadcasted_iota(jnp.int32, (rows, LANES), 1)
    low = lane < DH
    blk = lax.broadcasted_iota(jnp.int32, (nb_pad, rows), 0)
    past = blk < i
    for hp in range(HEADS // 2):
        cols = slice(hp * LANES, (hp + 1) * LANES)
        q2 = q_ref[0, :, cols]
        k2 = k_ref[0, :, cols]
        v2 = v_ref[0, :, cols]
        km2 = kmean_scr[0:nb_pad, cols]
        for e in range(2):
            mine = low if e == 0 else jnp.logical_not(low)
            gate = _dot_t(km2, jnp.where(mine, q2, 0.0), precision=HIGHEST)
            sel = _top_k_mask(jnp.where(past, gate, NEG_INF), blk, MOBA_TOPK, 0)
            keep = jnp.where(past, sel, 0.0) + jnp.where(blk == i, 1.0, 0.0)
            bias = jnp.where(keep > 0.5, 0.0, NEG_INF)
            bias = jnp.concatenate([bias, jnp.zeros((LANES - nb_pad, rows), F32)], axis=0).T
            onehot = jnp.where(lane == i, 1.0, 0.0)
            if e == 0:
                bias = pltpu.roll(bias, DH, axis=1)
                onehot = jnp.where(lane == i + DH, 1.0, 0.0)
                ones_col = jnp.where(lane == DH, 1.0, 0.0)
            else:
                ones_col = jnp.where(lane == 0, 1.0, 0.0)
            h = 2 * hp + e
            qa_ref[0, h] = jnp.where(mine, q2 * SCALE, bias).astype(BF16)
            ka_ref[0, h] = jnp.where(mine, k2, onehot).astype(BF16)
            va_ref[0, h] = jnp.where(mine, v2, ones_col).astype(BF16)

    kmean_scr[pl.ds(i, 1), :] = jnp.mean(k_ref[0], axis=0, keepdims=True)


def _moba_prep(proj):
    b, s, _ = proj.shape
    nb = s // MOBA_BLOCK
    nb_pad = -(-nb // 8) * 8
    assert nb_pad <= DH
    blk = lambda j: pl.BlockSpec((1, MOBA_BLOCK, W_ATT), lambda bi, i: (bi, i, j))
    aug = pl.BlockSpec((1, HEADS, MOBA_BLOCK, LANES), lambda bi, i: (bi, 0, i, 0))
    shape = jax.ShapeDtypeStruct((b, HEADS, s, LANES), BF16)
    return pl.pallas_call(
        functools.partial(_moba_prep_kernel, nb_pad=nb_pad),
        grid=(b, nb),
        in_specs=[blk(0), blk(1), blk(2)],
        out_specs=[aug, aug, aug],
        out_shape=[shape, shape, shape],
        scratch_shapes=[pltpu.VMEM((LANES, W_ATT), F32)],
        compiler_params=_params("arbitrary", "arbitrary"),
    )(proj, proj, proj)


def _moba_attn_kernel(q_ref, k_ref, v_ref, o_ref, m_scr, acc_scr, *, tq):
    qi = pl.program_id(2)
    tk = MOBA_BLOCK
    n_diag = tq // tk
    row = lax.broadcasted_iota(jnp.int32, (tq, tk), 0)
    col = lax.broadcasted_iota(jnp.int32, (tq, tk), 1)
    lane = lax.broadcasted_iota(jnp.int32, (tq, LANES), 1)
    hs = q_ref.shape[1]

    def step(n, causal_offset=None):
        st = pl.multiple_of(n * tk, tk)
        for e in range(hs):
            s = _dot_t(q_ref[0, e], k_ref[0, e, pl.ds(st, tk), :])
            if causal_offset is not None:
                s = jnp.where(col + causal_offset <= row, s, NEG_INF)
            m = m_scr[e]
            m_new = jnp.maximum(m, jnp.max(s, axis=1, keepdims=True))
            m_scr[e] = m_new
            p = jnp.exp(s - jnp.concatenate([m_new] * (tk // LANES), axis=1)).astype(BF16)
            acc_scr[e] = (jnp.exp(m - m_new) * acc_scr[e]
                          + _dot(p, v_ref[0, e, pl.ds(st, tk), :]))

    m_scr[...] = jnp.full(m_scr.shape, LOWEST, F32)
    acc_scr[...] = jnp.zeros(acc_scr.shape, F32)
    for j in range(n_diag):
        step(qi * n_diag + j, causal_offset=j * tk)

    def body(n, carry):
        step(n)
        return carry

    lax.fori_loop(0, qi * n_diag, body, 0)
    for e in range(0, hs, 2):
        even, odd = acc_scr[e], acc_scr[e + 1]
        o_ref[0, :, e // 2 * LANES:(e // 2 + 1) * LANES] = jnp.where(
            lane < DH, even / even[:, DH:DH + 1], odd / odd[:, 0:1])


def _moba_attn(qa, ka, va, tq, hs):
    b, h, s, _ = qa.shape
    return pl.pallas_call(
        functools.partial(_moba_attn_kernel, tq=tq),
        grid=(b, h // hs, s // tq),
        in_specs=[pl.BlockSpec((1, hs, tq, LANES), lambda bi, hp, i: (bi, hp, i, 0)),
                  pl.BlockSpec((1, hs, s, LANES), lambda bi, hp, i: (bi, hp, 0, 0),
                               pipeline_mode=pl.Buffered(1)),
                  pl.BlockSpec((1, hs, s, LANES), lambda bi, hp, i: (bi, hp, 0, 0),
                               pipeline_mode=pl.Buffered(1))],
        out_specs=pl.BlockSpec((1, tq, hs // 2 * LANES), lambda bi, hp, i: (bi, i, hp)),
        out_shape=jax.ShapeDtypeStruct((b, s, W_ATT), F32),
        scratch_shapes=[pltpu.VMEM((hs, tq, LANES), F32), pltpu.VMEM((hs, tq, LANES), F32)],
        compiler_params=_params("parallel", "parallel", "arbitrary"),
    )(qa, ka, va)


def _column(ref, rows, lane_is_seq):
    return jnp.sum(jnp.where(lane_is_seq, ref[rows, :], 0.0), axis=1, keepdims=True)


def _zero_refs(out_refs):
    for ref in out_refs:
        ref[...] = jnp.zeros(ref.shape, F32)


def _moba_dec_body(bi, _, in_refs, out_refs, *, n_pages, page):
    q_ref, kn_ref, vn_ref = in_refs[:3]
    k_refs, v_refs = in_refs[3:3 + n_pages], in_refs[3 + n_pages:]
    o_ref, = out_refs
    pages_per_block = MOBA_BLOCK // page
    n_past = n_pages // pages_per_block
    lane_is_seq = lax.broadcasted_iota(jnp.int32, (DH, LANES), 1) == bi
    for h in range(HEADS):
        rows = slice(h * DH, (h + 1) * DH)
        qc = _column(q_ref, rows, lane_is_seq)
        kc = _column(kn_ref, rows, lane_is_seq)
        vc = _column(vn_ref, rows, lane_is_seq)
        raw = [jnp.sum(k_refs[j][0, h] * qc, axis=0, keepdims=True) for j in range(n_pages)]
        gates = []
        for n in range(n_past):
            tot = raw[n * pages_per_block]
            for j in range(1, pages_per_block):
                tot = tot + raw[n * pages_per_block + j]
            gates.append(jnp.sum(tot, axis=1, keepdims=True) * (1.0 / MOBA_BLOCK))
        sel = [jnp.zeros((1, 1), F32)] * n_past
        for _ in range(min(MOBA_TOPK, n_past + 1)):
            best = functools.reduce(jnp.maximum, gates)
            free = jnp.ones((1, 1), F32)
            for n in range(n_past):
                hit = jnp.where(gates[n] == best, free, 0.0)
                free = free - hit
                sel[n] = jnp.maximum(sel[n], hit)
                gates[n] = jnp.where(hit > 0.5, LOWEST, gates[n])
        s_self = jnp.sum(qc * kc, axis=0, keepdims=True) * SCALE
        s = [jnp.where(sel[j // pages_per_block] > 0.5, raw[j] * SCALE, NEG_INF)
             for j in range(n_pages)]
        m = s_self
        for sj in s:
            m = jnp.maximum(m, jnp.max(sj, axis=1, keepdims=True))
        p_self = jnp.exp(s_self - m)
        l = p_self
        acc = jnp.zeros((DH, page), F32)
        for j in range(n_pages):
            pj = jnp.exp(s[j] - m)
            l = l + jnp.sum(pj, axis=1, keepdims=True)
            acc = acc + v_refs[j][0, h] * pj
        o = (jnp.sum(acc, axis=1, keepdims=True) + p_self * vc) / l
        o_ref[rows, :] = jnp.where(lane_is_seq, o, o_ref[rows, :])


def _moba_dec_guest(proj_t, cache_k, cache_v, page_table):
    b = proj_t.shape[1]
    n_pages = page_table.shape[1]
    page = cache_k.shape[3]
    assert b == LANES
    vec = lambda j: ((W_ATT, b), lambda bi, pt: (j, 0))
    pg = lambda j: ((1, HEADS, DH, page), lambda bi, pt: (pt[bi * n_pages + j], 0, 0, 0))
    pages = [pg(j) for j in range(n_pages)]
    return _Guest(
        in_specs=[vec(0), vec(1), vec(2)] + pages + pages,
        args=[proj_t] * 3 + [cache_k] * n_pages + [cache_v] * n_pages,
        out_specs=[((W_ATT, b), lambda bi, pt: (0, 0))],
        out_shape=[jax.ShapeDtypeStruct((W_ATT, b), F32)],
        init=_zero_refs,
        body=functools.partial(_moba_dec_body, n_pages=n_pages, page=page),
        prefetch=page_table.reshape(-1))


def _hgrn_seq_kernel(q_ref, f_ref, i_ref, g_ref, lb_ref, gn_ref, o_ref, s_ref, st_scr, *, tb):
    t = pl.program_id(0)

    @pl.when(t == 0)
    def _():
        st_scr[...] = jnp.zeros(st_scr.shape, F32)

    c = HGRN_CHUNK
    nch = tb // c
    w = q_ref.shape[2]
    lb = lb_ref[...]
    gn = gn_ref[...]
    r_i = lax.broadcasted_iota(jnp.int32, (tb, tb), 0)
    c_i = lax.broadcasted_iota(jnp.int32, (tb, tb), 1)
    tril = jnp.logical_and(c_i <= r_i, c_i // c == r_i // c)
    ones_tril = jnp.where(tril, 1.0, 0.0)
    per_chunk = lambda x, row: jnp.broadcast_to(
        x.reshape(nch, c, w)[:, row:row + 1, :], (nch, c, w)).reshape(tb, w)
    for bi in range(q_ref.shape[0]):
        f = lb + (1.0 - lb) * _sigmoid(f_ref[bi])
        k = 1.0 - f
        q = _silu(q_ref[bi])
        v = i_ref[bi]
        b = jnp.dot(ones_tril, jnp.log(f), precision=HIGHEST, preferred_element_type=F32)
        b_mid = per_chunk(b, c // 2 - 1)
        b_last = per_chunk(b, c - 1)
        q_mid = (q * jnp.exp(b - b_mid)).astype(BF16)
        k_mid = (k * jnp.exp(b_mid - b)).astype(BF16)
        k_end = (k * jnp.exp(b_last - b)).astype(BF16)
        q_in = (q * jnp.exp(b)).astype(BF16)
        decay = jnp.exp(b_last)
        vb = v.astype(BF16)
        outs = []
        for h in range(H_B):
            cols = slice(h * D_B, (h + 1) * D_B)
            a = jnp.where(tril, _dot_t(q_mid[:, cols], k_mid[:, cols]), 0.0)
            o_intra = _dot(a.astype(BF16), vb[:, cols])
            o_inter = []
            st = st_scr[bi, h]
            for ch in range(nch):
                rows = slice(ch * c, (ch + 1) * c)
                o_inter.append(_dot_t(q_in[rows, cols], st.astype(BF16)))
                st = (st * decay[ch * c:ch * c + 1, cols]
                      + _dot(v[rows, cols].T.astype(BF16), k_end[rows, cols]))
            st_scr[bi, h] = st
            outs.append(_rms(o_intra + jnp.concatenate(o_inter, axis=0), gn[:, cols]))
        o_ref[bi] = jnp.concatenate(outs, axis=1) * _silu(g_ref[bi])

    @pl.when(t == pl.num_programs(0) - 1)
    def _():
        for bi in range(q_ref.shape[0]):
            for h in range(H_B):
                s_ref[bi, h] = st_scr[bi, h].T


def _hgrn_seq(proj, lb, gn, first_chunk, tb):
    b, s, _ = proj.shape
    w = H_B * D_B
    blk = lambda j: pl.BlockSpec((b, tb, w), lambda t: (0, t, first_chunk + j))
    vec = pl.BlockSpec((1, w), lambda t: (0, 0))
    return pl.pallas_call(
        functools.partial(_hgrn_seq_kernel, tb=tb),
        grid=(s // tb,),
        in_specs=[blk(0), blk(1), blk(2), blk(3), vec, vec],
        out_specs=[pl.BlockSpec((b, tb, w), lambda t: (0, t, 0)),
                   pl.BlockSpec((b, H_B, D_B, D_B), lambda t: (0, 0, 0, 0))],
        out_shape=[jax.ShapeDtypeStruct((b, s, w), F32),
                   jax.ShapeDtypeStruct((b, H_B, D_B, D_B), F32)],
        scratch_shapes=[pltpu.VMEM((b, H_B, D_B, D_B), F32)],
        compiler_params=_params("arbitrary"),
    )(proj, proj, proj, proj, lb.reshape(1, w), jnp.tile(gn.astype(F32), H_B).reshape(1, w))


def _hgrn_step_kernel(q_ref, f_ref, i_ref, g_ref, lb_ref, gn_ref, s0_ref, o_ref, s_ref):
    bt = s0_ref.shape[0]
    rows = pl.ds(pl.multiple_of(pl.program_id(0) * bt, bt), bt)
    lb = lb_ref[...]
    gn = gn_ref[...]
    f_all = lb + (1.0 - lb) * _sigmoid(f_ref[rows, :])
    q_all = _silu(q_ref[rows, :])
    v_all = i_ref[rows, :]
    gate_all = _silu(g_ref[rows, :])
    eye = (lax.broadcasted_iota(jnp.int32, (D_B, D_B), 0)
           == lax.broadcasted_iota(jnp.int32, (D_B, D_B), 1))
    for j in range(bt):
        f, q, v = f_all[j:j + 1], q_all[j:j + 1], v_all[j:j + 1]
        outs = []
        for h in range(H_B):
            cols = slice(h * D_B, (h + 1) * D_B)
            f_col = jnp.sum(jnp.where(eye, f[:, cols], 0.0), axis=1, keepdims=True)
            q_col = jnp.sum(jnp.where(eye, q[:, cols], 0.0), axis=1, keepdims=True)
            s_new = f_col * s0_ref[j, h] + (1.0 - f_col) * v[:, cols]
            s_ref[j, h] = s_new
            o = jnp.sum(q_col * s_new, axis=0, keepdims=True)
            outs.append(_rms(o, gn[:, cols]))
        o_ref[j:j + 1, :] = jnp.concatenate(outs, axis=1) * gate_all[j:j + 1]


def _hgrn_step(proj, lb, gn, s0, first_chunk, bt):
    b = proj.shape[0]
    w = H_B * D_B
    blk = lambda j: pl.BlockSpec((b, w), lambda bi: (0, first_chunk + j))
    vec = pl.BlockSpec((1, w), lambda bi: (0, 0))
    st = pl.BlockSpec((bt, H_B, D_B, D_B), lambda bi: (bi, 0, 0, 0))
    return pl.pallas_call(
        _hgrn_step_kernel,
        grid=(b // bt,),
        in_specs=[blk(0), blk(1), blk(2), blk(3), vec, vec, st],
        out_specs=[pl.BlockSpec((bt, w), lambda bi: (bi, 0)), st],
        out_shape=[jax.ShapeDtypeStruct((b, w), F32),
                   jax.ShapeDtypeStruct((b, H_B, D_B, D_B), F32)],
        compiler_params=_params("parallel"),
    )(proj, proj, proj, proj, lb.reshape(1, w),
      jnp.tile(gn.astype(F32), H_B).reshape(1, w), s0)


def _band_kernel(q_ref, kc_ref, kp_ref, vc_ref, vp_ref, o_ref, l_ref, *, tq, dil):
    first = pl.program_id(1) == 0
    r = pl.program_id(2)
    row = lax.broadcasted_iota(jnp.int32, (BAND, 2 * BAND), 0)
    col = lax.broadcasted_iota(jnp.int32, (BAND, 2 * BAND), 1)
    band = jnp.logical_and(col >= row, col <= row + BAND)
    lane = lax.broadcasted_iota(jnp.int32, (BAND, LANES), 1)
    low = lane < DH
    for i in range(tq // BAND):
        rows = slice(i * BAND, (i + 1) * BAND)
        if i == 0:
            k_prev, v_prev = kp_ref[0, 0], vp_ref[0, 0]
            mask = jnp.logical_and(band, jnp.logical_or(col >= BAND, jnp.logical_not(first)))
        else:
            prev = slice((i - 1) * BAND, i * BAND)
            k_prev, v_prev = kc_ref[0, 0, prev, :], vc_ref[0, 0, prev, :]
            mask = band
        kk = jnp.concatenate([k_prev, kc_ref[0, 0, rows, :]], axis=0).astype(BF16)
        vv = jnp.concatenate([v_prev, vc_ref[0, 0, rows, :]], axis=0).astype(BF16)
        out_rows = rows if dil == 1 else pl.ds(r + i * BAND * dil, BAND, stride=dil)
        for hp in range(HEADS // 2):
            cols = slice(hp * LANES, (hp + 1) * LANES)
            q2 = q_ref[0, 0, rows, cols] * SCALE
            res = []
            for e in range(2):
                mine = low if e == 0 else jnp.logical_not(low)
                s = _dot_t(jnp.where(mine, q2, 0.0).astype(BF16), kk[:, cols])
                s = jnp.where(mask, s, NEG_INF)
                m = jnp.max(s, axis=1, keepdims=True)
                p = jnp.exp(s - m)
                l = jnp.sum(p, axis=1, keepdims=True)
                res.append((_dot((p / l).astype(BF16), vv[:, cols]), m + jnp.log(l)))
            o_ref[0, hp, out_rows, :] = jnp.where(low, res[0][0], res[1][0])
            l_ref[0, hp, out_rows, :] = jnp.where(low, res[0][1], res[1][1])


def _band_attention(proj, tq):
    b, dil, ln, _ = proj.shape
    sub = tq // BAND
    cur = lambda comp: pl.BlockSpec((1, 1, tq, W_ATT), lambda bi, l, r: (bi, r, l, comp))
    prev = lambda comp: pl.BlockSpec((1, 1, BAND, W_ATT),
                                     lambda bi, l, r: (bi, r, jnp.maximum(l * sub - 1, 0), comp))
    out = pl.BlockSpec((1, HEADS // 2, tq * dil, LANES), lambda bi, l, r: (bi, 0, l, 0))
    shape = jax.ShapeDtypeStruct((b, HEADS // 2, ln * dil, LANES), F32)
    return pl.pallas_call(
        functools.partial(_band_kernel, tq=tq, dil=dil),
        grid=(b, ln // tq, dil),
        in_specs=[cur(0), cur(1), prev(1), cur(2), prev(2)],
        out_specs=[out, out],
        out_shape=[shape, shape],
        compiler_params=_params("parallel", "arbitrary", "arbitrary"),
    )(proj, proj, proj, proj, proj)


def _dil_cache_body(step, _, in_refs, out_refs, *, dils, hb, head_groups):
    bi, hg = step // head_groups, step % head_groups
    for g, dil in enumerate(dils):
        _dil_cache_group(bi, hg, *in_refs[5 * g:5 * g + 5], *out_refs[4 * g:4 * g + 4],
                         dil=dil, hb=hb)


def _dil_cache_init(out_refs):
    for g in range(len(out_refs) // 4):
        _zero_refs(out_refs[4 * g + 2:4 * g + 4])


def _dil_cache_kernel(*refs, dils, hb):
    n_in = 5 * len(dils)

    @pl.when(jnp.logical_and(pl.program_id(0) == 0, pl.program_id(1) == 0))
    def _():
        _dil_cache_init(refs[n_in:])

    _dil_cache_body(pl.program_id(0) * pl.num_programs(1) + pl.program_id(1), None,
                    refs[:n_in], refs[n_in:], dils=dils, hb=hb, head_groups=HEADS // hb)


def _dil_cache_group(bi, hg, q_ref, kn_ref, vn_ref, ck_ref, cv_ref, ok_ref, ov_ref, o_ref, l_ref,
                     *, dil, hb):
    lw = ck_ref.shape[3]
    nr = hb * DH
    rows = pl.ds(pl.multiple_of(hg * nr, nr), nr)
    lane_is_seq = lax.broadcasted_iota(jnp.int32, (nr, LANES), 1) == bi
    pos = lax.broadcasted_iota(jnp.int32, (hb, lw), 1)
    on_grid = (pos & (dil - 1)) == 0
    newest = lax.broadcasted_iota(jnp.int32, (DH, lw), 1) == lw - 1
    qc = _column(q_ref, rows, lane_is_seq) * SCALE
    kc = _column(kn_ref, rows, lane_is_seq)
    vc = _column(vn_ref, rows, lane_is_seq)
    head = lambda x, hh: x[hh * DH:(hh + 1) * DH]
    s = jnp.concatenate([jnp.sum(ck_ref[0, hh] * head(qc, hh), axis=0, keepdims=True)
                         for hh in range(hb)], axis=0)
    s = jnp.where(on_grid, s, NEG_INF)
    qk = qc * kc
    s_self = jnp.concatenate([jnp.sum(head(qk, hh), axis=0, keepdims=True)
                              for hh in range(hb)], axis=0)
    m = jnp.maximum(jnp.max(s, axis=1, keepdims=True), s_self)
    p = jnp.exp(s - m)
    p_self = jnp.exp(s_self - m)
    l = jnp.sum(p, axis=1, keepdims=True) + p_self
    pn = p / l
    w_self = p_self / l
    lse = m + jnp.log(l)
    os, ls = [], []
    for hh in range(hb):
        k = ck_ref[0, hh]
        v = cv_ref[0, hh]
        os.append(jnp.sum(v * pn[hh:hh + 1, :], axis=1, keepdims=True)
                  + w_self[hh:hh + 1, :] * head(vc, hh))
        ls.append(jnp.broadcast_to(lse[hh:hh + 1, :], (DH, 1)))
        ok_ref[0, hh] = jnp.where(newest, head(kc, hh), pltpu.roll(k, lw - 1, axis=1))
        ov_ref[0, hh] = jnp.where(newest, head(vc, hh), pltpu.roll(v, lw - 1, axis=1))
    o_ref[rows, :] = jnp.where(lane_is_seq, jnp.concatenate(os, axis=0), o_ref[rows, :])
    l_ref[rows, :] = jnp.where(lane_is_seq, jnp.concatenate(ls, axis=0), l_ref[rows, :])


def _dil_cache_guest(proj_t, n_groups, groups, dils, caches, hb):
    b, h, dh, _ = caches[0][0].shape
    hgs = h // hb
    assert b == LANES and all(dil & (dil - 1) == 0 for dil in dils)
    vec = lambda j: ((W_ATT, b), lambda step, *_: (j, 0))
    res = ((W_ATT, b), lambda step, *_: (0, 0))
    res_shape = jax.ShapeDtypeStruct((W_ATT, b), F32)
    in_specs, out_specs, out_shape, args = [], [], [], []
    for g, (ck, cv) in zip(groups, caches):
        cache = ((1, hb, dh, ck.shape[3]), lambda step, *_: (step // hgs, step % hgs, 0, 0))
        in_specs += [vec(g), vec(n_groups + g), vec(2 * n_groups + g), cache, cache]
        args += [proj_t, proj_t, proj_t, ck, cv]
        out_specs += [cache, cache, res, res]
        out_shape += [jax.ShapeDtypeStruct(ck.shape, F32), jax.ShapeDtypeStruct(cv.shape, F32),
                      res_shape, res_shape]
    return _Guest(in_specs, args, out_specs, out_shape, _dil_cache_init,
                  functools.partial(_dil_cache_body, dils=tuple(dils), hb=hb, head_groups=hgs))


def _dil_cache(guest, b, head_groups):
    by_step = lambda f: (lambda bi, hg: f(bi * head_groups + hg))
    return pl.pallas_call(
        functools.partial(_dil_cache_kernel, dils=guest.body.keywords["dils"],
                          hb=guest.body.keywords["hb"]),
        grid=(b, head_groups),
        in_specs=[pl.BlockSpec(shape, by_step(f)) for shape, f in guest.in_specs],
        out_specs=[pl.BlockSpec(shape, by_step(f)) for shape, f in guest.out_specs],
        out_shape=guest.out_shape,
        compiler_params=_params("arbitrary", "arbitrary"),
    )(*guest.args)


def kernel(x_prompt, x_sample, cache_k_a, cache_v_a, state_hgrn, cache_k_c0, cache_v_c0, cache_k_c1, cache_v_c1, cache_k_c2, cache_v_c2, state_conv, page_table, norm_mix, norm_ffn, w_in_ab, q_norm_a, k_norm_a, lb_logits, g_norm_b, w_out_ab, w_in_c, q_norm_c, k_norm_c, w_out_c, w_gate, w_up, conv_w, conv_b, w_down):
    bsz, seq, d = x_prompt.shape
    nb = x_sample.shape[0]
    n = bsz * seq
    dff = w_gate.shape[2]
    tm, tc = 512, dff // 2
    ng = len(C_PATTERNS)
    c_caches = ((cache_k_c0, cache_v_c0), (cache_k_c1, cache_v_c1), (cache_k_c2, cache_v_c2))

    lb_all = jnp.cumsum(jax.nn.softmax(lb_logits.astype(F32), axis=0), axis=0)
    bf = lambda w: w.astype(BF16)

    xp = x_prompt.reshape(n, d)
    xs = x_sample.reshape(nb, d)

    w_in = bf(w_in_ab[0])
    gains = _head_gains([q_norm_a[0], k_norm_a[0]])
    w_out = bf(w_out_ab[0])
    lb = lb_all[0]

    pp, k_t, v_t = _norm_proj(xp, norm_mix[0], w_in, gains, tm, transposed=(1, 2), n_seq=bsz)
    pp = pp.reshape(bsz, seq, -1)
    qa, ka, va = _moba_prep(pp)
    oa = _moba_attn(qa, ka, va, 512, HEADS).reshape(n, W_ATT)
    ob, hg_p = _hgrn_seq(pp, lb, g_norm_b[0], 3, 256)
    xp = _out_ab(oa, ob.reshape(n, -1), w_out, xp, tm)

    pos_minor = lambda c: jnp.transpose(c, (0, 2, 3, 1))
    pos_major = lambda c: jnp.transpose(c, (0, 3, 1, 2))
    k_a_p = pos_major(k_t.reshape(bsz, HEADS, DH, seq))[None]
    v_a_p = pos_major(v_t.reshape(bsz, HEADS, DH, seq))[None]

    conv_p, conv_s = [], []
    tm_ffn, nc_ffn = 256, 2

    def ffn(l, xp, guest):
        wg, wu, wd = bf(w_gate[l]), bf(w_up[l]), bf(w_down[l])
        xp, cp, guest_out = _ffn_seq(xp, norm_ffn[l], wg, wu, conv_w[l], conv_b[l], wd,
                                     bsz, tm_ffn, nc_ffn, guest)

        def sample(xs):
            xs, a_s = _ffn_step(xs, norm_ffn[l], wg, wu, conv_w[l], conv_b[l], wd,
                                state_conv[l, :, 0, :], state_conv[l, :, 1, :], tc)
            conv_s.append(jnp.stack([state_conv[l, :, 1, :], a_s], axis=1))
            return xs

        conv_p.append(cp)
        return xp, sample, guest_out

    ps = _norm_proj(xs, norm_mix[0], w_in, gains, nb)
    assert n // tm_ffn * nc_ffn == nb
    xp, ffn_sample, (oas_t,) = ffn(0, xp, _moba_dec_guest(
        ps.T, pos_minor(cache_k_a[0]), pos_minor(cache_v_a[0]), page_table))
    obs, hg_s = _hgrn_step(ps, lb, g_norm_b[0], state_hgrn[0], 3, 8)
    xs = ffn_sample(_out_ab_t(oas_t, obs, w_out, xs))
    k_a_s = ps[:, W_ATT:2 * W_ATT].reshape(1, nb, 1, HEADS, DH)
    v_a_s = ps[:, 2 * W_ATT:3 * W_ATT].reshape(1, nb, 1, HEADS, DH)

    w_in = bf(w_in_c[0])
    w_out = bf(w_out_c[0])
    gains = _head_gains([q_norm_c[0], k_norm_c[0]])

    os, ls, c_prompt = [], [], []
    for g, (win, dil) in enumerate(C_PATTERNS):
        ln = seq // dil
        w_g = jnp.concatenate([w_in[:, (c * ng + g) * W_ATT:(c * ng + g + 1) * W_ATT]
                               for c in range(3)], axis=1)
        xr = xp.reshape(bsz, ln, dil, d).transpose(0, 2, 1, 3).reshape(n, d)
        pg = _norm_proj(xr, norm_mix[1], w_g, gains, tm)
        o, lse = _band_attention(pg.reshape(bsz, dil, ln, 3 * W_ATT), min(512, 4096 // dil))
        os.append(o)
        ls.append(lse)
        keep = min(win, seq)
        kd = keep // dil
        tail = lax.optimization_barrier(lax.slice(
            pg.reshape(bsz * dil, ln, 3 * W_ATT), (0, ln - kd, W_ATT), (bsz * dil, ln, 3 * W_ATT)))
        tail = tail.reshape(bsz, dil, kd, 2, HEADS, DH).transpose(3, 0, 2, 1, 4, 5)
        tail = tail.reshape(2, 1, bsz, keep, HEADS, DH)
        c_prompt.append((tail[0], tail[1]))
    xp = _out_c(os, ls, w_out, xp, tm)

    gains = _head_gains([q_norm_c[0]] * ng + [k_norm_c[0]] * ng)
    ps_t = _norm_proj(xs, norm_mix[1], w_in, gains, nb).T
    dils = [dil for _, dil in C_PATTERNS]
    views = [(pos_minor(ck[0]), pos_minor(cv[0])) for ck, cv in c_caches]
    hosted = [g for g in range(ng) if C_PATTERNS[g][0] <= 512]
    alone = [g for g in range(ng) if g not in hosted]
    pick = lambda xs_, gs: [xs_[g] for g in gs]
    xp, ffn_sample, hosted_out = ffn(1, xp, _dil_cache_guest(
        ps_t, ng, hosted, pick(dils, hosted), pick(views, hosted), HEADS))
    alone_out = _dil_cache(_dil_cache_guest(ps_t, ng, alone, pick(dils, alone), pick(views, alone),
                                            HEADS // 2), nb, 2)
    decoded = dict(zip(hosted + alone,
                       [outs[4 * j:4 * j + 4] for outs in (hosted_out, alone_out)
                        for j in range(len(outs) // 4)]))
    c_out, os_t, ls_t = [], [], []
    for g in range(ng):
        nk, nv, o_t, l_t = decoded[g]
        c_out += list(c_prompt[g])
        c_out += [pos_major(nk)[None], pos_major(nv)[None]]
        os_t.append(o_t)
        ls_t.append(l_t)
    xs = ffn_sample(_out_c_t(os_t, ls_t, w_out, xs))

    return (xp.reshape(bsz, seq, d), xs.reshape(nb, 1, d),
            k_a_p, v_a_p, k_a_s, v_a_s, hg_p[None], hg_s[None],
            *c_out, jnp.stack(conv_p), jnp.stack(conv_s))
```

```python
import functools

import jax
import jax.numpy as jnp
from jax import lax
from jax.experimental import pallas as pl
from jax.experimental.pallas import tpu as pltpu

F32 = jnp.float32
BF16 = jnp.bfloat16
HIGHEST = lax.Precision.HIGHEST

RMS_EPS = 1e-6
NEG_INF = -1e30
LOWEST = -3e38

LANES = 128
DH = 64
HEADS = 8
W_ATT = HEADS * DH
SCALE = DH ** -0.5
MOBA_BLOCK = 256
MOBA_TOPK = 3
H_B = 4
D_B = 128
HGRN_CHUNK = 32
BAND = 128
C_PATTERNS = ((128, 1), (512, 4), (2048, 16))
VMEM_LIMIT = 48 * 1024 * 1024
VMEM_LIMIT_HOSTED = 56 * 1024 * 1024

_CONTRACT_LAST = (((1,), (1,)), ((), ()))


def _params(*sem, vmem=VMEM_LIMIT):
    return pltpu.CompilerParams(dimension_semantics=sem, vmem_limit_bytes=vmem)


def _sigmoid(x):
    return 1.0 / (1.0 + jnp.exp(-x))


def _silu(x):
    return x * _sigmoid(x)


def _rms(x, g):
    return x * lax.rsqrt(jnp.mean(x * x, axis=-1, keepdims=True) + RMS_EPS) * g


def _dot(a, b):
    return jnp.dot(a, b, preferred_element_type=F32)


def _dot_t(a, b, precision=None):
    return lax.dot_general(a, b, _CONTRACT_LAST, precision=precision, preferred_element_type=F32)


def _norm_proj_kernel(x_ref, g_ref, w_ref, hg_ref, p_ref, o_ref, *t_refs, n_norm, transposed):
    h = _rms(x_ref[...], g_ref[...]).astype(BF16)
    for j in range(w_ref.shape[1] // W_ATT):
        cols = slice(j * W_ATT, (j + 1) * W_ATT)
        y = _dot(h, w_ref[:, cols])
        if j < n_norm:
            ms = _dot((y * y).astype(BF16), p_ref[...])
            y = y * lax.rsqrt(ms + RMS_EPS) * hg_ref[j:j + 1, :]
        o_ref[:, cols] = y
        if j in transposed:
            t_refs[transposed.index(j)][0] = y.T


def _norm_proj(x, g, w_bf, head_gains, tm, transposed=(), n_seq=1):
    n, d = x.shape
    m = w_bf.shape[1]
    n_norm = head_gains.shape[0]
    per_seq = n // n_seq // tm
    blockdiag = (jnp.arange(W_ATT)[:, None] // DH == jnp.arange(W_ATT)[None, :] // DH)
    p = (blockdiag.astype(F32) / DH).astype(BF16)
    t_spec = pl.BlockSpec((1, W_ATT, tm), lambda i: (i // per_seq, 0, i % per_seq))
    t_shape = jax.ShapeDtypeStruct((n_seq, W_ATT, n // n_seq), F32)
    outs = pl.pallas_call(
        functools.partial(_norm_proj_kernel, n_norm=n_norm, transposed=tuple(transposed)),
        grid=(n // tm,),
        in_specs=[
            pl.BlockSpec((tm, d), lambda i: (i, 0)),
            pl.BlockSpec((1, d), lambda i: (0, 0)),
            pl.BlockSpec((d, m), lambda i: (0, 0)),
            pl.BlockSpec((n_norm, W_ATT), lambda i: (0, 0)),
            pl.BlockSpec((W_ATT, W_ATT), lambda i: (0, 0)),
        ],
        out_specs=[pl.BlockSpec((tm, m), lambda i: (i, 0))] + [t_spec] * len(transposed),
        out_shape=[jax.ShapeDtypeStruct((n, m), F32)] + [t_shape] * len(transposed),
        compiler_params=_params("parallel"),
    )(x, g.reshape(1, d), w_bf, head_gains, p)
    return outs if transposed else outs[0]


def _head_gains(gains):
    return jnp.stack([jnp.tile(gv.astype(F32), HEADS) for gv in gains])


class _Guest:
    def __init__(self, in_specs, args, out_specs, out_shape, init, body, prefetch=None):
        self.in_specs, self.args = in_specs, args
        self.out_specs, self.out_shape = out_specs, out_shape
        self.init, self.body, self.prefetch = init, body, prefetch


def _ffn_seq_kernel(*refs, tm, blocks_per_seq, guest):
    refs = list(refs)
    pt_ref = refs.pop(0) if guest is not None and guest.prefetch is not None else None
    n_gi = len(guest.in_specs) if guest is not None else 0
    n_go = len(guest.out_specs) if guest is not None else 0
    x_ref, g_ref, wg_ref, wu_ref, cw_ref, cb_ref, wd_ref = refs[:7]
    guest_in = refs[7:7 + n_gi]
    o_ref, cs_ref = refs[7 + n_gi:9 + n_gi]
    guest_out = refs[9 + n_gi:9 + n_gi + n_go]
    a_scr, acc_scr = refs[9 + n_gi + n_go:]
    i = pl.program_id(0)
    c = pl.program_id(1)
    step = i * pl.num_programs(1) + c

    @pl.when(i % blocks_per_seq == 0)
    def _():
        a_scr[c, 0:8, :] = jnp.zeros((8, a_scr.shape[2]), F32)

    @pl.when(c == 0)
    def _():
        acc_scr[...] = jnp.zeros(acc_scr.shape, F32)

    if guest is not None:
        @pl.when(step == 0)
        def _():
            guest.init(guest_out)

    tc = a_scr.shape[2]
    cols = pl.ds(pl.multiple_of(c * tc, LANES), tc)
    h = _rms(x_ref[...], g_ref[...]).astype(BF16)
    a = _dot(h, wg_ref[:, cols])
    u = _dot(h, wu_ref[:, cols])
    a_scr[c, 8:8 + tm, :] = a
    a1 = a_scr[c, 7:7 + tm, :]
    a2 = a_scr[c, 6:6 + tm, :]
    cw = cw_ref[:, cols]
    conv = cb_ref[:, cols] + a2 * cw[0:1] + a1 * cw[1:2] + a * cw[2:3]
    y = (_silu(conv) * u).astype(BF16)
    acc = acc_scr[...] + _dot(y, wd_ref[c])
    acc_scr[...] = acc
    a_scr[c, 0:8, :] = a_scr[c, tm:tm + 8, :]
    cs_ref[0, c] = a_scr[c, tm + 6:tm + 8, :]
    o_ref[...] = x_ref[...] + acc
    if guest is not None:
        guest.body(step, pt_ref, guest_in, guest_out)


def _ffn_seq(x, g, wg, wu, cw, cb, wd, n_seq, tm, nc, guest=None):
    n, d = x.shape
    dff = wg.shape[1]
    tc = dff // nc
    bps = n // n_seq // tm
    resident = lambda shape: pl.BlockSpec(shape, lambda i, c, *_: (0,) * len(shape),
                                          pipeline_mode=pl.Buffered(1))
    in_specs = [
        pl.BlockSpec((tm, d), lambda i, c, *_: (i, 0)),
        resident((1, d)),
        resident((d, dff)), resident((d, dff)), resident((3, dff)), resident((1, dff)),
        resident((nc, tc, d)),
    ]
    args = [x, g.reshape(1, d), wg, wu, cw, cb.reshape(1, dff), wd.reshape(nc, tc, d)]
    out_specs = [pl.BlockSpec((tm, d), lambda i, c, *_: (i, 0)),
                 pl.BlockSpec((1, nc, 2, tc), lambda i, c, *_: (i // bps, 0, 0, 0))]
    out_shape = [jax.ShapeDtypeStruct((n, d), F32), jax.ShapeDtypeStruct((n_seq, nc, 2, tc), F32)]
    prefetch = []
    if guest is not None:
        by_step = lambda f: (lambda i, c, *pf: f(i * nc + c, *pf))
        in_specs += [pl.BlockSpec(shape, by_step(f)) for shape, f in guest.in_specs]
        out_specs += [pl.BlockSpec(shape, by_step(f)) for shape, f in guest.out_specs]
        args += guest.args
        out_shape += guest.out_shape
        prefetch = [] if guest.prefetch is None else [guest.prefetch]
    outs = pl.pallas_call(
        functools.partial(_ffn_seq_kernel, tm=tm, blocks_per_seq=bps, guest=guest),
        grid_spec=pltpu.PrefetchScalarGridSpec(
            num_scalar_prefetch=len(prefetch),
            grid=(n // tm, nc),
            in_specs=in_specs,
            out_specs=out_specs,
            scratch_shapes=[pltpu.VMEM((nc, tm + 8, tc), F32), pltpu.VMEM((tm, d), F32)],
        ),
        out_shape=out_shape,
        compiler_params=_params("arbitrary", "arbitrary", vmem=VMEM_LIMIT_HOSTED),
    )(*prefetch, *args)
    conv_state = outs[1].transpose(0, 2, 1, 3).reshape(n_seq, 2, dff)
    return outs[0], conv_state, outs[2:]


def _ffn_step_kernel(x_ref, g_ref, wg_ref, wu_ref, cw_ref, cb_ref, wd_ref, s0_ref, s1_ref,
                     o_ref, a_ref, h_scr, acc_scr):
    c = pl.program_id(0)

    @pl.when(c == 0)
    def _():
        h_scr[...] = _rms(x_ref[...], g_ref[...]).astype(BF16)

    h = h_scr[...]
    a = _dot(h, wg_ref[...])
    u = _dot(h, wu_ref[...])
    a_ref[...] = a
    cw = cw_ref[...]
    conv = cb_ref[...] + s0_ref[...] * cw[0:1] + s1_ref[...] * cw[1:2] + a * cw[2:3]
    y = (_silu(conv) * u).astype(BF16)
    part = _dot(y, wd_ref[...])

    @pl.when(c == 0)
    def _():
        acc_scr[...] = part

    @pl.when(c > 0)
    def _():
        acc_scr[...] = acc_scr[...] + part

    @pl.when(c == pl.num_programs(0) - 1)
    def _():
        o_ref[...] = x_ref[...] + acc_scr[...]


def _ffn_step(x, g, wg, wu, cw, cb, wd, s0, s1, tc):
    n, d = x.shape
    dff = wg.shape[1]
    return pl.pallas_call(
        _ffn_step_kernel,
        grid=(dff // tc,),
        in_specs=[
            pl.BlockSpec((n, d), lambda c: (0, 0)),
            pl.BlockSpec((1, d), lambda c: (0, 0)),
            pl.BlockSpec((d, tc), lambda c: (0, c)),
            pl.BlockSpec((d, tc), lambda c: (0, c)),
            pl.BlockSpec((3, tc), lambda c: (0, c)),
            pl.BlockSpec((1, tc), lambda c: (0, c)),
            pl.BlockSpec((tc, d), lambda c: (c, 0)),
            pl.BlockSpec((n, tc), lambda c: (0, c)),
            pl.BlockSpec((n, tc), lambda c: (0, c)),
        ],
        out_specs=[pl.BlockSpec((n, d), lambda c: (0, 0)),
                   pl.BlockSpec((n, tc), lambda c: (0, c))],
        out_shape=[jax.ShapeDtypeStruct((n, d), F32), jax.ShapeDtypeStruct((n, dff), F32)],
        scratch_shapes=[pltpu.VMEM((n, d), BF16), pltpu.VMEM((n, d), F32)],
        compiler_params=_params("arbitrary"),
    )(x, g.reshape(1, d), wg, wu, cw, cb.reshape(1, dff), wd, s0, s1)


def _out_ab_kernel(oa_ref, ob_ref, w_ref, x_ref, o_ref):
    ka = oa_ref.shape[1]
    o_ref[...] = (x_ref[...] + _dot(oa_ref[...].astype(BF16), w_ref[0:ka, :])
                  + _dot(ob_ref[...].astype(BF16), w_ref[ka:, :]))


def _out_ab(oa, ob, w_bf, x, tm):
    n, d = x.shape
    ka, kb = oa.shape[1], ob.shape[1]
    return pl.pallas_call(
        _out_ab_kernel,
        grid=(n // tm,),
        in_specs=[pl.BlockSpec((tm, ka), lambda i: (i, 0)),
                  pl.BlockSpec((tm, kb), lambda i: (i, 0)),
                  pl.BlockSpec((ka + kb, d), lambda i: (0, 0)),
                  pl.BlockSpec((tm, d), lambda i: (i, 0))],
        out_specs=pl.BlockSpec((tm, d), lambda i: (i, 0)),
        out_shape=jax.ShapeDtypeStruct((n, d), F32),
        compiler_params=_params("parallel"),
    )(oa, ob, w_bf, x)


def _merge_groups(os, ls):
    m = jnp.maximum(jnp.maximum(ls[0], ls[1]), ls[2])
    es = [jnp.exp(l - m) for l in ls]
    den = es[0] + es[1] + es[2]
    return (es[0] / den) * os[0] + (es[1] / den) * os[1] + (es[2] / den) * os[2]


def _out_c_kernel(o0, o1, o2, l0, l1, l2, w_ref, x_ref, o_ref):
    merged = jnp.concatenate(
        [_merge_groups([o0[0, hp], o1[0, hp], o2[0, hp]], [l0[0, hp], l1[0, hp], l2[0, hp]])
         for hp in range(HEADS // 2)], axis=1)
    o_ref[...] = x_ref[...] + _dot(merged.astype(BF16), w_ref[...])


def _out_c(os, ls, w_bf, x, tm):
    n, d = x.shape
    k = w_bf.shape[0]
    per_seq = os[0].shape[2] // tm
    act = pl.BlockSpec((1, HEADS // 2, tm, LANES), lambda i: (i // per_seq, 0, i % per_seq, 0))
    return pl.pallas_call(
        _out_c_kernel,
        grid=(n // tm,),
        in_specs=[act] * 6 + [pl.BlockSpec((k, d), lambda i: (0, 0)),
                              pl.BlockSpec((tm, d), lambda i: (i, 0))],
        out_specs=pl.BlockSpec((tm, d), lambda i: (i, 0)),
        out_shape=jax.ShapeDtypeStruct((n, d), F32),
        compiler_params=_params("parallel"),
    )(*os, *ls, w_bf, x)


def _out_ab_t_kernel(oat_ref, ob_ref, w_ref, x_ref, o_ref):
    ka = oat_ref.shape[0]
    o_ref[...] = (x_ref[...] + _dot(oat_ref[...].T.astype(BF16), w_ref[0:ka, :])
                  + _dot(ob_ref[...].astype(BF16), w_ref[ka:, :]))


def _out_ab_t(oa_t, ob, w_bf, x):
    return pl.pallas_call(
        _out_ab_t_kernel,
        out_shape=jax.ShapeDtypeStruct(x.shape, F32),
        compiler_params=pltpu.CompilerParams(vmem_limit_bytes=VMEM_LIMIT),
    )(oa_t, ob, w_bf, x)


def _out_c_t_kernel(o0, o1, o2, l0, l1, l2, w_ref, x_ref, o_ref):
    merged = _merge_groups([o0[...], o1[...], o2[...]], [l0[...], l1[...], l2[...]])
    o_ref[...] = x_ref[...] + _dot(merged.T.astype(BF16), w_ref[...])


def _out_c_t(os_t, ls_t, w_bf, x):
    return pl.pallas_call(
        _out_c_t_kernel,
        out_shape=jax.ShapeDtypeStruct(x.shape, F32),
        compiler_params=pltpu.CompilerParams(vmem_limit_bytes=VMEM_LIMIT),
    )(*os_t, *ls_t, w_bf, x)


def _top_k_mask(g, index, k, axis):
    sel = jnp.zeros(g.shape, F32)
    index_f = index.astype(F32)
    for _ in range(k):
        m = jnp.max(g, axis=axis, keepdims=True)
        first = jnp.min(jnp.where(g == m, index_f, float(g.shape[axis])), axis=axis, keepdims=True)
        hit = index_f == first
        sel = jnp.where(hit, 1.0, sel)
        g = jnp.where(hit, LOWEST, g)
    return sel


def _moba_prep_kernel(q_ref, k_ref, v_ref, qa_ref, ka_ref, va_ref, kmean_scr, *, nb_pad):
    i = pl.program_id(1)

    @pl.when(i == 0)
    def _():
        kmean_scr[...] = jnp.zeros(kmean_scr.shape, F32)

    rows = q_ref.shape[1]
    lane = lax.broadcasted_iota(jnp.int32, (rows, LANES), 1)
    low = lane < DH
    blk = lax.broadcasted_iota(jnp.int32, (nb_pad, rows), 0)
    past = blk < i
    for hp in range(HEADS // 2):
        cols = slice(hp * LANES, (hp + 1) * LANES)
        q2 = q_ref[0, :, cols]
        k2 = k_ref[0, :, cols]
        v2 = v_ref[0, :, cols]
        km2 = kmean_scr[0:nb_pad, cols]
        for e in range(2):
            mine = low if e == 0 else jnp.logical_not(low)
            gate = _dot_t(km2, jnp.where(mine, q2, 0.0), precision=HIGHEST)
            sel = _top_k_mask(jnp.where(past, gate, NEG_INF), blk, MOBA_TOPK, 0)
            keep = jnp.where(past, sel, 0.0) + jnp.where(blk == i, 1.0, 0.0)
            bias = jnp.where(keep > 0.5, 0.0, NEG_INF)
            bias = jnp.concatenate([bias, jnp.zeros((LANES - nb_pad, rows), F32)], axis=0).T
            onehot = jnp.where(lane == i, 1.0, 0.0)
            if e == 0:
                bias = pltpu.roll(bias, DH, axis=1)
                onehot = jnp.where(lane == i + DH, 1.0, 0.0)
                ones_col = jnp.where(lane == DH, 1.0, 0.0)
            else:
                ones_col = jnp.where(lane == 0, 1.0, 0.0)
            h = 2 * hp + e
            qa_ref[0, h] = jnp.where(mine, q2 * SCALE, bias).astype(BF16)
            ka_ref[0, h] = jnp.where(mine, k2, onehot).astype(BF16)
            va_ref[0, h] = jnp.where(mine, v2, ones_col).astype(BF16)

    kmean_scr[pl.ds(i, 1), :] = jnp.mean(k_ref[0], axis=0, keepdims=True)


def _moba_prep(proj):
    b, s, _ = proj.shape
    nb = s // MOBA_BLOCK
    nb_pad = -(-nb // 8) * 8
    assert nb_pad <= DH
    blk = lambda j: pl.BlockSpec((1, MOBA_BLOCK, W_ATT), lambda bi, i: (bi, i, j))
    aug = pl.BlockSpec((1, HEADS, MOBA_BLOCK, LANES), lambda bi, i: (bi, 0, i, 0))
    shape = jax.ShapeDtypeStruct((b, HEADS, s, LANES), BF16)
    return pl.pallas_call(
        functools.partial(_moba_prep_kernel, nb_pad=nb_pad),
        grid=(b, nb),
        in_specs=[blk(0), blk(1), blk(2)],
        out_specs=[aug, aug, aug],
        out_shape=[shape, shape, shape],
        scratch_shapes=[pltpu.VMEM((LANES, W_ATT), F32)],
        compiler_params=_params("arbitrary", "arbitrary"),
    )(proj, proj, proj)


def _moba_attn_kernel(q_ref, k_ref, v_ref, o_ref, m_scr, acc_scr, *, tq):
    qi = pl.program_id(2)
    tk = MOBA_BLOCK
    n_diag = tq // tk
    lane = lax.broadcasted_iota(jnp.int32, (tq, LANES), 1)
    hs = q_ref.shape[1]

    def step(n, causal_offset=None):
        st = pl.multiple_of(n * tk, tk)
        rows = slice(causal_offset or 0, tq)
        for e in range(hs):
            s = _dot_t(q_ref[0, e, rows, :], k_ref[0, e, pl.ds(st, tk), :])
            if causal_offset is not None:
                s = jnp.where(lax.broadcasted_iota(jnp.int32, s.shape, 1)
                              <= lax.broadcasted_iota(jnp.int32, s.shape, 0), s, NEG_INF)
            m = m_scr[e, rows, :]
            m_new = jnp.maximum(m, jnp.max(s, axis=1, keepdims=True))
            m_scr[e, rows, :] = m_new
            p = jnp.exp(s - jnp.concatenate([m_new] * (tk // LANES), axis=1)).astype(BF16)
            acc_scr[e, rows, :] = (jnp.exp(m - m_new) * acc_scr[e, rows, :]
                                   + _dot(p, v_ref[0, e, pl.ds(st, tk), :]))

    m_scr[...] = jnp.full(m_scr.shape, LOWEST, F32)
    acc_scr[...] = jnp.zeros(acc_scr.shape, F32)
    for j in range(n_diag):
        step(qi * n_diag + j, causal_offset=j * tk)

    def body(n, carry):
        step(n)
        return carry

    lax.fori_loop(0, qi * n_diag, body, 0)
    for e in range(0, hs, 2):
        even, odd = acc_scr[e], acc_scr[e + 1]
        o_ref[0, :, e // 2 * LANES:(e // 2 + 1) * LANES] = jnp.where(
            lane < DH, even / even[:, DH:DH + 1], odd / odd[:, 0:1])


def _moba_attn(qa, ka, va, tq, hs):
    b, h, s, _ = qa.shape
    return pl.pallas_call(
        functools.partial(_moba_attn_kernel, tq=tq),
        grid=(b, h // hs, s // tq),
        in_specs=[pl.BlockSpec((1, hs, tq, LANES), lambda bi, hp, i: (bi, hp, i, 0)),
                  pl.BlockSpec((1, hs, s, LANES), lambda bi, hp, i: (bi, hp, 0, 0),
                               pipeline_mode=pl.Buffered(1)),
                  pl.BlockSpec((1, hs, s, LANES), lambda bi, hp, i: (bi, hp, 0, 0),
                               pipeline_mode=pl.Buffered(1))],
        out_specs=pl.BlockSpec((1, tq, hs // 2 * LANES), lambda bi, hp, i: (bi, i, hp)),
        out_shape=jax.ShapeDtypeStruct((b, s, W_ATT), F32),
        scratch_shapes=[pltpu.VMEM((hs, tq, LANES), F32), pltpu.VMEM((hs, tq, LANES), F32)],
        compiler_params=_params("parallel", "parallel", "arbitrary"),
    )(qa, ka, va)


def _column(ref, rows, lane_is_seq):
    return jnp.sum(jnp.where(lane_is_seq, ref[rows, :], 0.0), axis=1, keepdims=True)


def _zero_refs(out_refs):
    for ref in out_refs:
        ref[...] = jnp.zeros(ref.shape, F32)


def _moba_dec_body(bi, _, in_refs, out_refs, *, n_pages, page):
    q_ref, kn_ref, vn_ref = in_refs[:3]
    k_refs, v_refs = in_refs[3:3 + n_pages], in_refs[3 + n_pages:]
    o_ref, = out_refs
    pages_per_block = MOBA_BLOCK // page
    n_past = n_pages // pages_per_block
    lane_is_seq = lax.broadcasted_iota(jnp.int32, (W_ATT, LANES), 1) == bi
    rows = slice(0, W_ATT)
    qc = _column(q_ref, rows, lane_is_seq)
    kc = _column(kn_ref, rows, lane_is_seq)
    vc = _column(vn_ref, rows, lane_is_seq)
    head = lambda x, h: x[h * DH:(h + 1) * DH]
    by_head = lambda f: jnp.concatenate([f(h) for h in range(HEADS)], axis=0)
    raw = [by_head(lambda h, j=j: jnp.sum(k_refs[j][0, h] * head(qc, h), axis=0, keepdims=True))
           for j in range(n_pages)]
    gates = []
    for n in range(n_past):
        tot = raw[n * pages_per_block]
        for j in range(1, pages_per_block):
            tot = tot + raw[n * pages_per_block + j]
        gates.append(jnp.sum(tot, axis=1, keepdims=True) * (1.0 / MOBA_BLOCK))
    sel = [jnp.zeros((HEADS, 1), F32)] * n_past
    for _ in range(min(MOBA_TOPK, n_past + 1)):
        best = functools.reduce(jnp.maximum, gates)
        free = jnp.ones((HEADS, 1), F32)
        for n in range(n_past):
            hit = jnp.where(gates[n] == best, free, 0.0)
            free = free - hit
            sel[n] = jnp.maximum(sel[n], hit)
            gates[n] = jnp.where(hit > 0.5, LOWEST, gates[n])
    qk = qc * kc
    s_self = by_head(lambda h: jnp.sum(head(qk, h), axis=0, keepdims=True)) * SCALE
    s = [jnp.where(sel[j // pages_per_block] > 0.5, raw[j] * SCALE, NEG_INF)
         for j in range(n_pages)]
    m = s_self
    for sj in s:
        m = jnp.maximum(m, jnp.max(sj, axis=1, keepdims=True))
    p_self = jnp.exp(s_self - m)
    p = [jnp.exp(sj - m) for sj in s]
    l = p_self
    for pj in p:
        l = l + jnp.sum(pj, axis=1, keepdims=True)

    def attend(h):
        acc = v_refs[0][0, h] * p[0][h:h + 1, :]
        for j in range(1, n_pages):
            acc = acc + v_refs[j][0, h] * p[j][h:h + 1, :]
        return ((jnp.sum(acc, axis=1, keepdims=True) + p_self[h:h + 1, :] * head(vc, h))
                / l[h:h + 1, :])

    o_ref[...] = jnp.where(lane_is_seq, by_head(attend), o_ref[...])


def _moba_dec_guest(proj_t, cache_k, cache_v, page_table):
    b = proj_t.shape[1]
    n_pages = page_table.shape[1]
    page = cache_k.shape[3]
    assert b == LANES
    vec = lambda j: ((W_ATT, b), lambda bi, pt: (j, 0))
    pg = lambda j: ((1, HEADS, DH, page), lambda bi, pt: (pt[bi * n_pages + j], 0, 0, 0))
    pages = [pg(j) for j in range(n_pages)]
    return _Guest(
        in_specs=[vec(0), vec(1), vec(2)] + pages + pages,
        args=[proj_t] * 3 + [cache_k] * n_pages + [cache_v] * n_pages,
        out_specs=[((W_ATT, b), lambda bi, pt: (0, 0))],
        out_shape=[jax.ShapeDtypeStruct((W_ATT, b), F32)],
        init=_zero_refs,
        body=functools.partial(_moba_dec_body, n_pages=n_pages, page=page),
        prefetch=page_table.reshape(-1))


def _hgrn_seq_kernel(q_ref, f_ref, i_ref, g_ref, lb_ref, gn_ref, o_ref, s_ref, st_scr, *, tb):
    t = pl.program_id(0)

    @pl.when(t == 0)
    def _():
        st_scr[...] = jnp.zeros(st_scr.shape, F32)

    c = HGRN_CHUNK
    nch = tb // c
    w = q_ref.shape[2]
    lb = lb_ref[...]
    gn = gn_ref[...]
    r_i = lax.broadcasted_iota(jnp.int32, (tb, tb), 0)
    c_i = lax.broadcasted_iota(jnp.int32, (tb, tb), 1)
    tril = jnp.logical_and(c_i <= r_i, c_i // c == r_i // c)
    ones_tril = jnp.where(tril, 1.0, 0.0)
    per_chunk = lambda x, row: jnp.broadcast_to(
        x.reshape(nch, c, w)[:, row:row + 1, :], (nch, c, w)).reshape(tb, w)
    for bi in range(q_ref.shape[0]):
        f = lb + (1.0 - lb) * _sigmoid(f_ref[bi])
        k = 1.0 - f
        q = _silu(q_ref[bi])
        v = i_ref[bi]
        b = jnp.dot(ones_tril, jnp.log(f), precision=HIGHEST, preferred_element_type=F32)
        b_mid = per_chunk(b, c // 2 - 1)
        b_last = per_chunk(b, c - 1)
        q_mid = (q * jnp.exp(b - b_mid)).astype(BF16)
        k_mid = (k * jnp.exp(b_mid - b)).astype(BF16)
        k_end = (k * jnp.exp(b_last - b)).astype(BF16)
        q_in = (q * jnp.exp(b)).astype(BF16)
        decay = jnp.exp(b_last)
        vb = v.astype(BF16)
        outs = []
        for h in range(H_B):
            cols = slice(h * D_B, (h + 1) * D_B)
            a = jnp.where(tril, _dot_t(q_mid[:, cols], k_mid[:, cols]), 0.0)
            o_intra = _dot(a.astype(BF16), vb[:, cols])
            o_inter = []
            st = st_scr[bi, h]
            for ch in range(nch):
                rows = slice(ch * c, (ch + 1) * c)
                o_inter.append(_dot_t(q_in[rows, cols], st.astype(BF16)))
                st = (st * decay[ch * c:ch * c + 1, cols]
                      + _dot(v[rows, cols].T.astype(BF16), k_end[rows, cols]))
            st_scr[bi, h] = st
            outs.append(_rms(o_intra + jnp.concatenate(o_inter, axis=0), gn[:, cols]))
        o_ref[bi] = jnp.concatenate(outs, axis=1) * _silu(g_ref[bi])

    @pl.when(t == pl.num_programs(0) - 1)
    def _():
        for bi in range(q_ref.shape[0]):
            for h in range(H_B):
                s_ref[bi, h] = st_scr[bi, h].T


def _hgrn_seq(proj, lb, gn, first_chunk, tb):
    b, s, _ = proj.shape
    w = H_B * D_B
    blk = lambda j: pl.BlockSpec((b, tb, w), lambda t: (0, t, first_chunk + j))
    vec = pl.BlockSpec((1, w), lambda t: (0, 0))
    return pl.pallas_call(
        functools.partial(_hgrn_seq_kernel, tb=tb),
        grid=(s // tb,),
        in_specs=[blk(0), blk(1), blk(2), blk(3), vec, vec],
        out_specs=[pl.BlockSpec((b, tb, w), lambda t: (0, t, 0)),
                   pl.BlockSpec((b, H_B, D_B, D_B), lambda t: (0, 0, 0, 0))],
        out_shape=[jax.ShapeDtypeStruct((b, s, w), F32),
                   jax.ShapeDtypeStruct((b, H_B, D_B, D_B), F32)],
        scratch_shapes=[pltpu.VMEM((b, H_B, D_B, D_B), F32)],
        compiler_params=_params("arbitrary"),
    )(proj, proj, proj, proj, lb.reshape(1, w), jnp.tile(gn.astype(F32), H_B).reshape(1, w))


def _hgrn_step_kernel(q_ref, f_ref, i_ref, g_ref, lb_ref, gn_ref, s0_ref, o_ref, s_ref):
    bt = s0_ref.shape[0]
    rows = pl.ds(pl.multiple_of(pl.program_id(0) * bt, bt), bt)
    lb = lb_ref[...]
    gn = gn_ref[...]
    f_all = lb + (1.0 - lb) * _sigmoid(f_ref[rows, :])
    q_all = _silu(q_ref[rows, :])
    v_all = i_ref[rows, :]
    gate_all = _silu(g_ref[rows, :])
    eye = (lax.broadcasted_iota(jnp.int32, (D_B, D_B), 0)
           == lax.broadcasted_iota(jnp.int32, (D_B, D_B), 1))
    for j in range(bt):
        f, q, v = f_all[j:j + 1], q_all[j:j + 1], v_all[j:j + 1]
        outs = []
        for h in range(H_B):
            cols = slice(h * D_B, (h + 1) * D_B)
            f_col = jnp.sum(jnp.where(eye, f[:, cols], 0.0), axis=1, keepdims=True)
            q_col = jnp.sum(jnp.where(eye, q[:, cols], 0.0), axis=1, keepdims=True)
            s_new = f_col * s0_ref[j, h] + (1.0 - f_col) * v[:, cols]
            s_ref[j, h] = s_new
            o = jnp.sum(q_col * s_new, axis=0, keepdims=True)
            outs.append(_rms(o, gn[:, cols]))
        o_ref[j:j + 1, :] = jnp.concatenate(outs, axis=1) * gate_all[j:j + 1]


def _hgrn_step(proj, lb, gn, s0, first_chunk, bt):
    b = proj.shape[0]
    w = H_B * D_B
    blk = lambda j: pl.BlockSpec((b, w), lambda bi: (0, first_chunk + j))
    vec = pl.BlockSpec((1, w), lambda bi: (0, 0))
    st = pl.BlockSpec((bt, H_B, D_B, D_B), lambda bi: (bi, 0, 0, 0))
    return pl.pallas_call(
        _hgrn_step_kernel,
        grid=(b // bt,),
        in_specs=[blk(0), blk(1), blk(2), blk(3), vec, vec, st],
        out_specs=[pl.BlockSpec((bt, w), lambda bi: (bi, 0)), st],
        out_shape=[jax.ShapeDtypeStruct((b, w), F32),
                   jax.ShapeDtypeStruct((b, H_B, D_B, D_B), F32)],
        compiler_params=_params("parallel"),
    )(proj, proj, proj, proj, lb.reshape(1, w),
      jnp.tile(gn.astype(F32), H_B).reshape(1, w), s0)


def _band_kernel(q_ref, kc_ref, kp_ref, vc_ref, vp_ref, o_ref, l_ref, *, tq, dil):
    first = pl.program_id(1) == 0
    r = pl.program_id(2)
    row = lax.broadcasted_iota(jnp.int32, (BAND, 2 * BAND), 0)
    col = lax.broadcasted_iota(jnp.int32, (BAND, 2 * BAND), 1)
    band = jnp.logical_and(col >= row, col <= row + BAND)
    lane = lax.broadcasted_iota(jnp.int32, (BAND, LANES), 1)
    low = lane < DH
    for i in range(tq // BAND):
        rows = slice(i * BAND, (i + 1) * BAND)
        if i == 0:
            k_prev, v_prev = kp_ref[0, 0], vp_ref[0, 0]
            mask = jnp.logical_and(band, jnp.logical_or(col >= BAND, jnp.logical_not(first)))
        else:
            prev = slice((i - 1) * BAND, i * BAND)
            k_prev, v_prev = kc_ref[0, 0, prev, :], vc_ref[0, 0, prev, :]
            mask = band
        kk = jnp.concatenate([k_prev, kc_ref[0, 0, rows, :]], axis=0).astype(BF16)
        vv = jnp.concatenate([v_prev, vc_ref[0, 0, rows, :]], axis=0).astype(BF16)
        out_rows = rows if dil == 1 else pl.ds(r + i * BAND * dil, BAND, stride=dil)
        for hp in range(HEADS // 2):
            cols = slice(hp * LANES, (hp + 1) * LANES)
            q2 = q_ref[0, 0, rows, cols] * SCALE
            res = []
            for e in range(2):
                mine = low if e == 0 else jnp.logical_not(low)
                s = _dot_t(jnp.where(mine, q2, 0.0).astype(BF16), kk[:, cols])
                s = jnp.where(mask, s, NEG_INF)
                m = jnp.max(s, axis=1, keepdims=True)
                p = jnp.exp(s - m)
                l = jnp.sum(p, axis=1, keepdims=True)
                res.append((_dot((p / l).astype(BF16), vv[:, cols]), m + jnp.log(l)))
            o_ref[0, hp, out_rows, :] = jnp.where(low, res[0][0], res[1][0])
            l_ref[0, hp, out_rows, :] = jnp.where(low, res[0][1], res[1][1])


def _band_attention(proj, tq):
    b, dil, ln, _ = proj.shape
    sub = tq // BAND
    cur = lambda comp: pl.BlockSpec((1, 1, tq, W_ATT), lambda bi, l, r: (bi, r, l, comp))
    prev = lambda comp: pl.BlockSpec((1, 1, BAND, W_ATT),
                                     lambda bi, l, r: (bi, r, jnp.maximum(l * sub - 1, 0), comp))
    out = pl.BlockSpec((1, HEADS // 2, tq * dil, LANES), lambda bi, l, r: (bi, 0, l, 0))
    shape = jax.ShapeDtypeStruct((b, HEADS // 2, ln * dil, LANES), F32)
    return pl.pallas_call(
        functools.partial(_band_kernel, tq=tq, dil=dil),
        grid=(b, ln // tq, dil),
        in_specs=[cur(0), cur(1), prev(1), cur(2), prev(2)],
        out_specs=[out, out],
        out_shape=[shape, shape],
        compiler_params=_params("parallel", "arbitrary", "arbitrary"),
    )(proj, proj, proj, proj, proj)


def _dil_cache_body(step, _, in_refs, out_refs, *, dils, hb, head_groups):
    bi, hg = step // head_groups, step % head_groups
    for g, dil in enumerate(dils):
        _dil_cache_group(bi, hg, *in_refs[5 * g:5 * g + 5], *out_refs[4 * g:4 * g + 4],
                         dil=dil, hb=hb)


def _dil_cache_init(out_refs):
    for g in range(len(out_refs) // 4):
        _zero_refs(out_refs[4 * g + 2:4 * g + 4])


def _dil_cache_kernel(*refs, dils, hb):
    n_in = 5 * len(dils)

    @pl.when(jnp.logical_and(pl.program_id(0) == 0, pl.program_id(1) == 0))
    def _():
        _dil_cache_init(refs[n_in:])

    _dil_cache_body(pl.program_id(0) * pl.num_programs(1) + pl.program_id(1), None,
                    refs[:n_in], refs[n_in:], dils=dils, hb=hb, head_groups=HEADS // hb)


def _dil_cache_group(bi, hg, q_ref, kn_ref, vn_ref, ck_ref, cv_ref, ok_ref, ov_ref, o_ref, l_ref,
                     *, dil, hb):
    lw = ck_ref.shape[3]
    nr = hb * DH
    rows = pl.ds(pl.multiple_of(hg * nr, nr), nr)
    lane_is_seq = lax.broadcasted_iota(jnp.int32, (nr, LANES), 1) == bi
    pos = lax.broadcasted_iota(jnp.int32, (hb, lw), 1)
    on_grid = (pos & (dil - 1)) == 0
    newest = lax.broadcasted_iota(jnp.int32, (DH, lw), 1) == lw - 1
    qc = _column(q_ref, rows, lane_is_seq) * SCALE
    kc = _column(kn_ref, rows, lane_is_seq)
    vc = _column(vn_ref, rows, lane_is_seq)
    head = lambda x, hh: x[hh * DH:(hh + 1) * DH]
    s = jnp.concatenate([jnp.sum(ck_ref[0, hh] * head(qc, hh), axis=0, keepdims=True)
                         for hh in range(hb)], axis=0)
    s = jnp.where(on_grid, s, NEG_INF)
    qk = qc * kc
    s_self = jnp.concatenate([jnp.sum(head(qk, hh), axis=0, keepdims=True)
                              for hh in range(hb)], axis=0)
    m = jnp.maximum(jnp.max(s, axis=1, keepdims=True), s_self)
    p = jnp.exp(s - m)
    p_self = jnp.exp(s_self - m)
    l = jnp.sum(p, axis=1, keepdims=True) + p_self
    pn = p / l
    w_self = p_self / l
    lse = m + jnp.log(l)
    os, ls = [], []
    for hh in range(hb):
        k = ck_ref[0, hh]
        v = cv_ref[0, hh]
        os.append(jnp.sum(v * pn[hh:hh + 1, :], axis=1, keepdims=True)
                  + w_self[hh:hh + 1, :] * head(vc, hh))
        ls.append(jnp.broadcast_to(lse[hh:hh + 1, :], (DH, 1)))
        ok_ref[0, hh] = jnp.where(newest, head(kc, hh), pltpu.roll(k, lw - 1, axis=1))
        ov_ref[0, hh] = jnp.where(newest, head(vc, hh), pltpu.roll(v, lw - 1, axis=1))
    o_ref[rows, :] = jnp.where(lane_is_seq, jnp.concatenate(os, axis=0), o_ref[rows, :])
    l_ref[rows, :] = jnp.where(lane_is_seq, jnp.concatenate(ls, axis=0), l_ref[rows, :])


def _dil_cache_guest(proj_t, n_groups, groups, dils, caches, hb):
    b, h, dh, _ = caches[0][0].shape
    hgs = h // hb
    assert b == LANES and all(dil & (dil - 1) == 0 for dil in dils)
    vec = lambda j: ((W_ATT, b), lambda step, *_: (j, 0))
    res = ((W_ATT, b), lambda step, *_: (0, 0))
    res_shape = jax.ShapeDtypeStruct((W_ATT, b), F32)
    in_specs, out_specs, out_shape, args = [], [], [], []
    for g, (ck, cv) in zip(groups, caches):
        cache = ((1, hb, dh, ck.shape[3]), lambda step, *_: (step // hgs, step % hgs, 0, 0))
        in_specs += [vec(g), vec(n_groups + g), vec(2 * n_groups + g), cache, cache]
        args += [proj_t, proj_t, proj_t, ck, cv]
        out_specs += [cache, cache, res, res]
        out_shape += [jax.ShapeDtypeStruct(ck.shape, F32), jax.ShapeDtypeStruct(cv.shape, F32),
                      res_shape, res_shape]
    return _Guest(in_specs, args, out_specs, out_shape, _dil_cache_init,
                  functools.partial(_dil_cache_body, dils=tuple(dils), hb=hb, head_groups=hgs))


def _dil_cache(guest, b, head_groups):
    by_step = lambda f: (lambda bi, hg: f(bi * head_groups + hg))
    return pl.pallas_call(
        functools.partial(_dil_cache_kernel, dils=guest.body.keywords["dils"],
                          hb=guest.body.keywords["hb"]),
        grid=(b, head_groups),
        in_specs=[pl.BlockSpec(shape, by_step(f)) for shape, f in guest.in_specs],
        out_specs=[pl.BlockSpec(shape, by_step(f)) for shape, f in guest.out_specs],
        out_shape=guest.out_shape,
        compiler_params=_params("arbitrary", "arbitrary"),
    )(*guest.args)


def kernel(x_prompt, x_sample, cache_k_a, cache_v_a, state_hgrn, cache_k_c0, cache_v_c0, cache_k_c1, cache_v_c1, cache_k_c2, cache_v_c2, state_conv, page_table, norm_mix, norm_ffn, w_in_ab, q_norm_a, k_norm_a, lb_logits, g_norm_b, w_out_ab, w_in_c, q_norm_c, k_norm_c, w_out_c, w_gate, w_up, conv_w, conv_b, w_down):
    bsz, seq, d = x_prompt.shape
    nb = x_sample.shape[0]
    n = bsz * seq
    dff = w_gate.shape[2]
    tm, tc = 512, dff // 2
    ng = len(C_PATTERNS)
    c_caches = ((cache_k_c0, cache_v_c0), (cache_k_c1, cache_v_c1), (cache_k_c2, cache_v_c2))

    lb_all = jnp.cumsum(jax.nn.softmax(lb_logits.astype(F32), axis=0), axis=0)
    bf = lambda w: w.astype(BF16)

    xp = x_prompt.reshape(n, d)
    xs = x_sample.reshape(nb, d)

    w_in = bf(w_in_ab[0])
    gains = _head_gains([q_norm_a[0], k_norm_a[0]])
    w_out = bf(w_out_ab[0])
    lb = lb_all[0]

    pp, k_t, v_t = _norm_proj(xp, norm_mix[0], w_in, gains, tm, transposed=(1, 2), n_seq=bsz)
    pp = pp.reshape(bsz, seq, -1)
    qa, ka, va = _moba_prep(pp)
    oa = _moba_attn(qa, ka, va, 512, HEADS).reshape(n, W_ATT)
    ob, hg_p = _hgrn_seq(pp, lb, g_norm_b[0], 3, 256)
    xp = _out_ab(oa, ob.reshape(n, -1), w_out, xp, tm)

    pos_minor = lambda c: jnp.transpose(c, (0, 2, 3, 1))
    pos_major = lambda c: jnp.transpose(c, (0, 3, 1, 2))
    k_a_p = pos_major(k_t.reshape(bsz, HEADS, DH, seq))[None]
    v_a_p = pos_major(v_t.reshape(bsz, HEADS, DH, seq))[None]

    conv_p, conv_s = [], []
    tm_ffn, nc_ffn = 256, 2

    def ffn(l, xp, guest):
        wg, wu, wd = bf(w_gate[l]), bf(w_up[l]), bf(w_down[l])
        xp, cp, guest_out = _ffn_seq(xp, norm_ffn[l], wg, wu, conv_w[l], conv_b[l], wd,
                                     bsz, tm_ffn, nc_ffn, guest)

        def sample(xs):
            xs, a_s = _ffn_step(xs, norm_ffn[l], wg, wu, conv_w[l], conv_b[l], wd,
                                state_conv[l, :, 0, :], state_conv[l, :, 1, :], tc)
            conv_s.append(jnp.stack([state_conv[l, :, 1, :], a_s], axis=1))
            return xs

        conv_p.append(cp)
        return xp, sample, guest_out

    ps = _norm_proj(xs, norm_mix[0], w_in, gains, nb)
    assert n // tm_ffn * nc_ffn == nb
    xp, ffn_sample, (oas_t,) = ffn(0, xp, _moba_dec_guest(
        ps.T, pos_minor(cache_k_a[0]), pos_minor(cache_v_a[0]), page_table))
    obs, hg_s = _hgrn_step(ps, lb, g_norm_b[0], state_hgrn[0], 3, 8)
    xs = ffn_sample(_out_ab_t(oas_t, obs, w_out, xs))
    k_a_s = ps[:, W_ATT:2 * W_ATT].reshape(1, nb, 1, HEADS, DH)
    v_a_s = ps[:, 2 * W_ATT:3 * W_ATT].reshape(1, nb, 1, HEADS, DH)

    w_in = bf(w_in_c[0])
    w_out = bf(w_out_c[0])
    gains = _head_gains([q_norm_c[0], k_norm_c[0]])

    os, ls, c_prompt = [], [], []
    for g, (win, dil) in enumerate(C_PATTERNS):
        ln = seq // dil
        w_g = jnp.concatenate([w_in[:, (c * ng + g) * W_ATT:(c * ng + g + 1) * W_ATT]
                               for c in range(3)], axis=1)
        xr = xp.reshape(bsz, ln, dil, d).transpose(0, 2, 1, 3).reshape(n, d)
        pg = _norm_proj(xr, norm_mix[1], w_g, gains, tm)
        o, lse = _band_attention(pg.reshape(bsz, dil, ln, 3 * W_ATT), min(512, 4096 // dil))
        os.append(o)
        ls.append(lse)
        keep = min(win, seq)
        kd = keep // dil
        tail = lax.optimization_barrier(lax.slice(
            pg.reshape(bsz * dil, ln, 3 * W_ATT), (0, ln - kd, W_ATT), (bsz * dil, ln, 3 * W_ATT)))
        tail = tail.reshape(bsz, dil, kd, 2, HEADS, DH).transpose(3, 0, 2, 1, 4, 5)
        tail = tail.reshape(2, 1, bsz, keep, HEADS, DH)
        c_prompt.append((tail[0], tail[1]))
    xp = _out_c(os, ls, w_out, xp, tm)

    gains = _head_gains([q_norm_c[0]] * ng + [k_norm_c[0]] * ng)
    ps_t = _norm_proj(xs, norm_mix[1], w_in, gains, nb).T
    dils = [dil for _, dil in C_PATTERNS]
    views = [(pos_minor(ck[0]), pos_minor(cv[0])) for ck, cv in c_caches]
    hosted = [g for g in range(ng) if C_PATTERNS[g][0] <= 512]
    alone = [g for g in range(ng) if g not in hosted]
    pick = lambda xs_, gs: [xs_[g] for g in gs]
    xp, ffn_sample, hosted_out = ffn(1, xp, _dil_cache_guest(
        ps_t, ng, hosted, pick(dils, hosted), pick(views, hosted), HEADS))
    alone_out = _dil_cache(_dil_cache_guest(ps_t, ng, alone, pick(dils, alone), pick(views, alone),
                                            HEADS // 2), nb, 2)
    decoded = dict(zip(hosted + alone,
                       [outs[4 * j:4 * j + 4] for outs in (hosted_out, alone_out)
                        for j in range(len(outs) // 4)]))
    c_out, os_t, ls_t = [], [], []
    for g in range(ng):
        nk, nv, o_t, l_t = decoded[g]
        c_out += list(c_prompt[g])
        c_out += [pos_major(nk)[None], pos_major(nv)[None]]
        os_t.append(o_t)
        ls_t.append(l_t)
    xs = ffn_sample(_out_c_t(os_t, ls_t, w_out, xs))

    return (xp.reshape(bsz, seq, d), xs.reshape(nb, 1, d),
            k_a_p, v_a_p, k_a_s, v_a_s, hg_p[None], hg_s[None],
            *c_out, jnp.stack(conv_p), jnp.stack(conv_s))
```

```python
import functools

import jax
import jax.numpy as jnp
from jax import lax
from jax.experimental import pallas as pl
from jax.experimental.pallas import tpu as pltpu

F32 = jnp.float32
BF16 = jnp.bfloat16
HIGHEST = lax.Precision.HIGHEST

RMS_EPS = 1e-6
NEG_INF = -1e30
LOWEST = -3e38

LANES = 128
DH = 64
HEADS = 8
W_ATT = HEADS * DH
SCALE = DH ** -0.5
MOBA_BLOCK = 256
MOBA_TOPK = 3
H_B = 4
D_B = 128
HGRN_CHUNK = 32
BAND = 128
C_PATTERNS = ((128, 1), (512, 4), (2048, 16))
VMEM_LIMIT = 48 * 1024 * 1024
VMEM_LIMIT_HOSTED = 56 * 1024 * 1024

_CONTRACT_LAST = (((1,), (1,)), ((), ()))


def _params(*sem, vmem=VMEM_LIMIT):
    return pltpu.CompilerParams(dimension_semantics=sem, vmem_limit_bytes=vmem)


def _sigmoid(x):
    return 1.0 / (1.0 + jnp.exp(-x))


def _silu(x):
    return x * _sigmoid(x)


def _rms(x, g):
    return x * lax.rsqrt(jnp.mean(x * x, axis=-1, keepdims=True) + RMS_EPS) * g


def _dot(a, b):
    return jnp.dot(a, b, preferred_element_type=F32)


def _dot_t(a, b, precision=None):
    return lax.dot_general(a, b, _CONTRACT_LAST, precision=precision, preferred_element_type=F32)


def _norm_proj_kernel(x_ref, g_ref, w_ref, hg_ref, p_ref, o_ref, *t_refs, n_norm, transposed):
    h = _rms(x_ref[...], g_ref[...]).astype(BF16)
    for j in range(w_ref.shape[1] // W_ATT):
        cols = slice(j * W_ATT, (j + 1) * W_ATT)
        y = _dot(h, w_ref[:, cols])
        if j < n_norm:
            ms = _dot((y * y).astype(BF16), p_ref[...])
            y = y * lax.rsqrt(ms + RMS_EPS) * hg_ref[j:j + 1, :]
        o_ref[:, cols] = y
        if j in transposed:
            t_refs[transposed.index(j)][0] = y.T


def _norm_proj(x, g, w_bf, head_gains, tm, transposed=(), n_seq=1):
    n, d = x.shape
    m = w_bf.shape[1]
    n_norm = head_gains.shape[0]
    per_seq = n // n_seq // tm
    blockdiag = (jnp.arange(W_ATT)[:, None] // DH == jnp.arange(W_ATT)[None, :] // DH)
    p = (blockdiag.astype(F32) / DH).astype(BF16)
    t_spec = pl.BlockSpec((1, W_ATT, tm), lambda i: (i // per_seq, 0, i % per_seq))
    t_shape = jax.ShapeDtypeStruct((n_seq, W_ATT, n // n_seq), F32)
    outs = pl.pallas_call(
        functools.partial(_norm_proj_kernel, n_norm=n_norm, transposed=tuple(transposed)),
        grid=(n // tm,),
        in_specs=[
            pl.BlockSpec((tm, d), lambda i: (i, 0)),
            pl.BlockSpec((1, d), lambda i: (0, 0)),
            pl.BlockSpec((d, m), lambda i: (0, 0)),
            pl.BlockSpec((n_norm, W_ATT), lambda i: (0, 0)),
            pl.BlockSpec((W_ATT, W_ATT), lambda i: (0, 0)),
        ],
        out_specs=[pl.BlockSpec((tm, m), lambda i: (i, 0))] + [t_spec] * len(transposed),
        out_shape=[jax.ShapeDtypeStruct((n, m), F32)] + [t_shape] * len(transposed),
        compiler_params=_params("parallel"),
    )(x, g.reshape(1, d), w_bf, head_gains, p)
    return outs if transposed else outs[0]


def _head_gains(gains):
    return jnp.stack([jnp.tile(gv.astype(F32), HEADS) for gv in gains])


class _Guest:
    def __init__(self, in_specs, args, out_specs, out_shape, init, body, prefetch=None):
        self.in_specs, self.args = in_specs, args
        self.out_specs, self.out_shape = out_specs, out_shape
        self.init, self.body, self.prefetch = init, body, prefetch


def _ffn_seq_kernel(*refs, tm, blocks_per_seq, guest):
    refs = list(refs)
    pt_ref = refs.pop(0) if guest is not None and guest.prefetch is not None else None
    n_gi = len(guest.in_specs) if guest is not None else 0
    n_go = len(guest.out_specs) if guest is not None else 0
    x_ref, g_ref, wg_ref, wu_ref, cw_ref, cb_ref, wd_ref = refs[:7]
    guest_in = refs[7:7 + n_gi]
    o_ref, cs_ref = refs[7 + n_gi:9 + n_gi]
    guest_out = refs[9 + n_gi:9 + n_gi + n_go]
    a_scr, acc_scr = refs[9 + n_gi + n_go:]
    i = pl.program_id(0)
    c = pl.program_id(1)
    step = i * pl.num_programs(1) + c

    @pl.when(i % blocks_per_seq == 0)
    def _():
        a_scr[c, 0:8, :] = jnp.zeros((8, a_scr.shape[2]), F32)

    @pl.when(c == 0)
    def _():
        acc_scr[...] = jnp.zeros(acc_scr.shape, F32)

    if guest is not None:
        @pl.when(step == 0)
        def _():
            guest.init(guest_out)

    tc = a_scr.shape[2]
    cols = pl.ds(pl.multiple_of(c * tc, LANES), tc)
    h = _rms(x_ref[...], g_ref[...]).astype(BF16)
    a = _dot(h, wg_ref[:, cols])
    u = _dot(h, wu_ref[:, cols])
    a_scr[c, 8:8 + tm, :] = a
    a1 = a_scr[c, 7:7 + tm, :]
    a2 = a_scr[c, 6:6 + tm, :]
    cw = cw_ref[:, cols]
    conv = cb_ref[:, cols] + a2 * cw[0:1] + a1 * cw[1:2] + a * cw[2:3]
    y = (_silu(conv) * u).astype(BF16)
    acc = acc_scr[...] + _dot(y, wd_ref[c])
    acc_scr[...] = acc
    a_scr[c, 0:8, :] = a_scr[c, tm:tm + 8, :]
    cs_ref[0, c] = a_scr[c, tm + 6:tm + 8, :]
    o_ref[...] = x_ref[...] + acc
    if guest is not None:
        guest.body(step, pt_ref, guest_in, guest_out)


def _ffn_seq(x, g, wg, wu, cw, cb, wd, n_seq, tm, nc, guest=None):
    n, d = x.shape
    dff = wg.shape[1]
    tc = dff // nc
    bps = n // n_seq // tm
    resident = lambda shape: pl.BlockSpec(shape, lambda i, c, *_: (0,) * len(shape),
                                          pipeline_mode=pl.Buffered(1))
    in_specs = [
        pl.BlockSpec((tm, d), lambda i, c, *_: (i, 0)),
        resident((1, d)),
        resident((d, dff)), resident((d, dff)), resident((3, dff)), resident((1, dff)),
        resident((nc, tc, d)),
    ]
    args = [x, g.reshape(1, d), wg, wu, cw, cb.reshape(1, dff), wd.reshape(nc, tc, d)]
    out_specs = [pl.BlockSpec((tm, d), lambda i, c, *_: (i, 0)),
                 pl.BlockSpec((1, nc, 2, tc), lambda i, c, *_: (i // bps, 0, 0, 0))]
    out_shape = [jax.ShapeDtypeStruct((n, d), F32), jax.ShapeDtypeStruct((n_seq, nc, 2, tc), F32)]
    prefetch = []
    if guest is not None:
        by_step = lambda f: (lambda i, c, *pf: f(i * nc + c, *pf))
        in_specs += [pl.BlockSpec(shape, by_step(f)) for shape, f in guest.in_specs]
        out_specs += [pl.BlockSpec(shape, by_step(f)) for shape, f in guest.out_specs]
        args += guest.args
        out_shape += guest.out_shape
        prefetch = [] if guest.prefetch is None else [guest.prefetch]
    outs = pl.pallas_call(
        functools.partial(_ffn_seq_kernel, tm=tm, blocks_per_seq=bps, guest=guest),
        grid_spec=pltpu.PrefetchScalarGridSpec(
            num_scalar_prefetch=len(prefetch),
            grid=(n // tm, nc),
            in_specs=in_specs,
            out_specs=out_specs,
            scratch_shapes=[pltpu.VMEM((nc, tm + 8, tc), F32), pltpu.VMEM((tm, d), F32)],
        ),
        out_shape=out_shape,
        compiler_params=_params("arbitrary", "arbitrary", vmem=VMEM_LIMIT_HOSTED),
    )(*prefetch, *args)
    conv_state = outs[1].transpose(0, 2, 1, 3).reshape(n_seq, 2, dff)
    return outs[0], conv_state, outs[2:]


def _ffn_step_kernel(x_ref, g_ref, wg_ref, wu_ref, cw_ref, cb_ref, wd_ref, s0_ref, s1_ref,
                     o_ref, a_ref, h_scr, acc_scr):
    c = pl.program_id(0)

    @pl.when(c == 0)
    def _():
        h_scr[...] = _rms(x_ref[...], g_ref[...]).astype(BF16)

    h = h_scr[...]
    a = _dot(h, wg_ref[...])
    u = _dot(h, wu_ref[...])
    a_ref[...] = a
    cw = cw_ref[...]
    conv = cb_ref[...] + s0_ref[...] * cw[0:1] + s1_ref[...] * cw[1:2] + a * cw[2:3]
    y = (_silu(conv) * u).astype(BF16)
    part = _dot(y, wd_ref[...])

    @pl.when(c == 0)
    def _():
        acc_scr[...] = part

    @pl.when(c > 0)
    def _():
        acc_scr[...] = acc_scr[...] + part

    @pl.when(c == pl.num_programs(0) - 1)
    def _():
        o_ref[...] = x_ref[...] + acc_scr[...]


def _ffn_step(x, g, wg, wu, cw, cb, wd, s0, s1, tc):
    n, d = x.shape
    dff = wg.shape[1]
    return pl.pallas_call(
        _ffn_step_kernel,
        grid=(dff // tc,),
        in_specs=[
            pl.BlockSpec((n, d), lambda c: (0, 0)),
            pl.BlockSpec((1, d), lambda c: (0, 0)),
            pl.BlockSpec((d, tc), lambda c: (0, c)),
            pl.BlockSpec((d, tc), lambda c: (0, c)),
            pl.BlockSpec((3, tc), lambda c: (0, c)),
            pl.BlockSpec((1, tc), lambda c: (0, c)),
            pl.BlockSpec((tc, d), lambda c: (c, 0)),
            pl.BlockSpec((n, tc), lambda c: (0, c)),
            pl.BlockSpec((n, tc), lambda c: (0, c)),
        ],
        out_specs=[pl.BlockSpec((n, d), lambda c: (0, 0)),
                   pl.BlockSpec((n, tc), lambda c: (0, c))],
        out_shape=[jax.ShapeDtypeStruct((n, d), F32), jax.ShapeDtypeStruct((n, dff), F32)],
        scratch_shapes=[pltpu.VMEM((n, d), BF16), pltpu.VMEM((n, d), F32)],
        compiler_params=_params("arbitrary"),
    )(x, g.reshape(1, d), wg, wu, cw, cb.reshape(1, dff), wd, s0, s1)


def _out_ab_kernel(oa_ref, ob_ref, w_ref, x_ref, o_ref):
    ka = oa_ref.shape[1]
    o_ref[...] = (x_ref[...] + _dot(oa_ref[...].astype(BF16), w_ref[0:ka, :])
                  + _dot(ob_ref[...].astype(BF16), w_ref[ka:, :]))


def _out_ab(oa, ob, w_bf, x, tm):
    n, d = x.shape
    ka, kb = oa.shape[1], ob.shape[1]
    return pl.pallas_call(
        _out_ab_kernel,
        grid=(n // tm,),
        in_specs=[pl.BlockSpec((tm, ka), lambda i: (i, 0)),
                  pl.BlockSpec((tm, kb), lambda i: (i, 0)),
                  pl.BlockSpec((ka + kb, d), lambda i: (0, 0)),
                  pl.BlockSpec((tm, d), lambda i: (i, 0))],
        out_specs=pl.BlockSpec((tm, d), lambda i: (i, 0)),
        out_shape=jax.ShapeDtypeStruct((n, d), F32),
        compiler_params=_params("parallel"),
    )(oa, ob, w_bf, x)


def _merge_groups(os, ls):
    m = jnp.maximum(jnp.maximum(ls[0], ls[1]), ls[2])
    es = [jnp.exp(l - m) for l in ls]
    den = es[0] + es[1] + es[2]
    return (es[0] / den) * os[0] + (es[1] / den) * os[1] + (es[2] / den) * os[2]


def _out_c_kernel(o0, o1, o2, l0, l1, l2, w_ref, x_ref, o_ref):
    merged = jnp.concatenate(
        [_merge_groups([o0[0, hp], o1[0, hp], o2[0, hp]], [l0[0, hp], l1[0, hp], l2[0, hp]])
         for hp in range(HEADS // 2)], axis=1)
    o_ref[...] = x_ref[...] + _dot(merged.astype(BF16), w_ref[...])


def _out_c(os, ls, w_bf, x, tm):
    n, d = x.shape
    k = w_bf.shape[0]
    per_seq = os[0].shape[2] // tm
    act = pl.BlockSpec((1, HEADS // 2, tm, LANES), lambda i: (i // per_seq, 0, i % per_seq, 0))
    return pl.pallas_call(
        _out_c_kernel,
        grid=(n // tm,),
        in_specs=[act] * 6 + [pl.BlockSpec((k, d), lambda i: (0, 0)),
                              pl.BlockSpec((tm, d), lambda i: (i, 0))],
        out_specs=pl.BlockSpec((tm, d), lambda i: (i, 0)),
        out_shape=jax.ShapeDtypeStruct((n, d), F32),
        compiler_params=_params("parallel"),
    )(*os, *ls, w_bf, x)


def _out_ab_t_kernel(oat_ref, ob_ref, w_ref, x_ref, o_ref):
    ka = oat_ref.shape[0]
    o_ref[...] = (x_ref[...] + _dot(oat_ref[...].T.astype(BF16), w_ref[0:ka, :])
                  + _dot(ob_ref[...].astype(BF16), w_ref[ka:, :]))


def _out_ab_t(oa_t, ob, w_bf, x):
    return pl.pallas_call(
        _out_ab_t_kernel,
        out_shape=jax.ShapeDtypeStruct(x.shape, F32),
        compiler_params=pltpu.CompilerParams(vmem_limit_bytes=VMEM_LIMIT),
    )(oa_t, ob, w_bf, x)


def _out_c_t_kernel(o0, o1, o2, l0, l1, l2, w_ref, x_ref, o_ref):
    merged = _merge_groups([o0[...], o1[...], o2[...]], [l0[...], l1[...], l2[...]])
    o_ref[...] = x_ref[...] + _dot(merged.T.astype(BF16), w_ref[...])


def _out_c_t(os_t, ls_t, w_bf, x):
    return pl.pallas_call(
        _out_c_t_kernel,
        out_shape=jax.ShapeDtypeStruct(x.shape, F32),
        compiler_params=pltpu.CompilerParams(vmem_limit_bytes=VMEM_LIMIT),
    )(*os_t, *ls_t, w_bf, x)


def _top_k_mask(g, index, k, axis):
    sel = jnp.zeros(g.shape, F32)
    index_f = index.astype(F32)
    for _ in range(k):
        m = jnp.max(g, axis=axis, keepdims=True)
        first = jnp.min(jnp.where(g == m, index_f, float(g.shape[axis])), axis=axis, keepdims=True)
        hit = index_f == first
        sel = jnp.where(hit, 1.0, sel)
        g = jnp.where(hit, LOWEST, g)
    return sel


def _moba_prep_kernel(q_ref, k_ref, v_ref, qa_ref, ka_ref, va_ref, kmean_scr, *, nb_pad):
    i = pl.program_id(1)

    @pl.when(i == 0)
    def _():
        kmean_scr[...] = jnp.zeros(kmean_scr.shape, F32)

    rows = q_ref.shape[1]
    lane = lax.broadcasted_iota(jnp.int32, (rows, LANES), 1)
    low = lane < DH
    blk = lax.broadcasted_iota(jnp.int32, (nb_pad, rows), 0)
    past = blk < i
    for hp in range(HEADS // 2):
        cols = slice(hp * LANES, (hp + 1) * LANES)
        q2 = q_ref[0, :, cols]
        k2 = k_ref[0, :, cols]
        v2 = v_ref[0, :, cols]
        km2 = kmean_scr[0:nb_pad, cols]
        for e in range(2):
            mine = low if e == 0 else jnp.logical_not(low)
            gate = _dot_t(km2, jnp.where(mine, q2, 0.0), precision=HIGHEST)
            sel = _top_k_mask(jnp.where(past, gate, NEG_INF), blk, MOBA_TOPK, 0)
            keep = jnp.where(past, sel, 0.0) + jnp.where(blk == i, 1.0, 0.0)
            bias = jnp.where(keep > 0.5, 0.0, NEG_INF)
            bias = jnp.concatenate([bias, jnp.zeros((LANES - nb_pad, rows), F32)], axis=0).T
            onehot = jnp.where(lane == i, 1.0, 0.0)
            if e == 0:
                bias = pltpu.roll(bias, DH, axis=1)
                onehot = jnp.where(lane == i + DH, 1.0, 0.0)
                ones_col = jnp.where(lane == DH, 1.0, 0.0)
            else:
                ones_col = jnp.where(lane == 0, 1.0, 0.0)
            h = 2 * hp + e
            qa_ref[0, h] = jnp.where(mine, q2 * SCALE, bias).astype(BF16)
            ka_ref[0, h] = jnp.where(mine, k2, onehot).astype(BF16)
            va_ref[0, h] = jnp.where(mine, v2, ones_col).astype(BF16)

    kmean_scr[pl.ds(i, 1), :] = jnp.mean(k_ref[0], axis=0, keepdims=True)


def _moba_prep(proj):
    b, s, _ = proj.shape
    nb = s // MOBA_BLOCK
    nb_pad = -(-nb // 8) * 8
    assert nb_pad <= DH
    blk = lambda j: pl.BlockSpec((1, MOBA_BLOCK, W_ATT), lambda bi, i: (bi, i, j))
    aug = pl.BlockSpec((1, HEADS, MOBA_BLOCK, LANES), lambda bi, i: (bi, 0, i, 0))
    shape = jax.ShapeDtypeStruct((b, HEADS, s, LANES), BF16)
    return pl.pallas_call(
        functools.partial(_moba_prep_kernel, nb_pad=nb_pad),
        grid=(b, nb),
        in_specs=[blk(0), blk(1), blk(2)],
        out_specs=[aug, aug, aug],
        out_shape=[shape, shape, shape],
        scratch_shapes=[pltpu.VMEM((LANES, W_ATT), F32)],
        compiler_params=_params("arbitrary", "arbitrary"),
    )(proj, proj, proj)


def _moba_attn_kernel(q_ref, k_ref, v_ref, o_ref, m_scr, acc_scr, *, tq):
    qi = pl.program_id(2)
    tk = MOBA_BLOCK
    n_diag = tq // tk
    lane = lax.broadcasted_iota(jnp.int32, (tq, LANES), 1)
    hs = q_ref.shape[1]

    def step(n, causal_offset=None):
        st = pl.multiple_of(n * tk, tk)
        rows = slice(causal_offset or 0, tq)
        for e in range(hs):
            s = _dot_t(q_ref[0, e, rows, :], k_ref[0, e, pl.ds(st, tk), :])
            if causal_offset is not None:
                s = jnp.where(lax.broadcasted_iota(jnp.int32, s.shape, 1)
                              <= lax.broadcasted_iota(jnp.int32, s.shape, 0), s, NEG_INF)
            m = m_scr[e, rows, :]
            m_new = jnp.maximum(m, jnp.max(s, axis=1, keepdims=True))
            m_scr[e, rows, :] = m_new
            p = jnp.exp(s - jnp.concatenate([m_new] * (tk // LANES), axis=1)).astype(BF16)
            acc_scr[e, rows, :] = (jnp.exp(m - m_new) * acc_scr[e, rows, :]
                                   + _dot(p, v_ref[0, e, pl.ds(st, tk), :]))

    m_scr[...] = jnp.full(m_scr.shape, LOWEST, F32)
    acc_scr[...] = jnp.zeros(acc_scr.shape, F32)
    for j in range(n_diag):
        step(qi * n_diag + j, causal_offset=j * tk)

    def body(n, carry):
        step(n)
        return carry

    lax.fori_loop(0, qi * n_diag, body, 0)
    for e in range(0, hs, 2):
        even, odd = acc_scr[e], acc_scr[e + 1]
        o_ref[0, :, e // 2 * LANES:(e // 2 + 1) * LANES] = jnp.where(
            lane < DH, even / even[:, DH:DH + 1], odd / odd[:, 0:1])


def _moba_attn(qa, ka, va, tq, hs):
    b, h, s, _ = qa.shape
    return pl.pallas_call(
        functools.partial(_moba_attn_kernel, tq=tq),
        grid=(b, h // hs, s // tq),
        in_specs=[pl.BlockSpec((1, hs, tq, LANES), lambda bi, hp, i: (bi, hp, i, 0)),
                  pl.BlockSpec((1, hs, s, LANES), lambda bi, hp, i: (bi, hp, 0, 0),
                               pipeline_mode=pl.Buffered(1)),
                  pl.BlockSpec((1, hs, s, LANES), lambda bi, hp, i: (bi, hp, 0, 0),
                               pipeline_mode=pl.Buffered(1))],
        out_specs=pl.BlockSpec((1, tq, hs // 2 * LANES), lambda bi, hp, i: (bi, i, hp)),
        out_shape=jax.ShapeDtypeStruct((b, s, W_ATT), F32),
        scratch_shapes=[pltpu.VMEM((hs, tq, LANES), F32), pltpu.VMEM((hs, tq, LANES), F32)],
        compiler_params=_params("parallel", "parallel", "arbitrary"),
    )(qa, ka, va)


def _column(ref, rows, lane_is_seq):
    return jnp.sum(jnp.where(lane_is_seq, ref[rows, :], 0.0), axis=1, keepdims=True)


def _zero_refs(out_refs):
    for ref in out_refs:
        ref[...] = jnp.zeros(ref.shape, F32)


def _moba_dec_body(bi, _, in_refs, out_refs, *, n_pages, page):
    q_ref, kn_ref, vn_ref = in_refs[:3]
    k_refs, v_refs = in_refs[3:3 + n_pages], in_refs[3 + n_pages:]
    o_ref, = out_refs
    pages_per_block = MOBA_BLOCK // page
    n_past = n_pages // pages_per_block
    lane_is_seq = lax.broadcasted_iota(jnp.int32, (W_ATT, LANES), 1) == bi
    rows = slice(0, W_ATT)
    qc = _column(q_ref, rows, lane_is_seq)
    kc = _column(kn_ref, rows, lane_is_seq)
    vc = _column(vn_ref, rows, lane_is_seq)
    head = lambda x, h: x[h * DH:(h + 1) * DH]
    by_head = lambda f: jnp.concatenate([f(h) for h in range(HEADS)], axis=0)
    raw = [by_head(lambda h, j=j: jnp.sum(k_refs[j][0, h] * head(qc, h), axis=0, keepdims=True))
           for j in range(n_pages)]
    gates = []
    for n in range(n_past):
        tot = raw[n * pages_per_block]
        for j in range(1, pages_per_block):
            tot = tot + raw[n * pages_per_block + j]
        gates.append(jnp.sum(tot, axis=1, keepdims=True) * (1.0 / MOBA_BLOCK))
    sel = [jnp.zeros((HEADS, 1), F32)] * n_past
    for _ in range(min(MOBA_TOPK, n_past + 1)):
        best = functools.reduce(jnp.maximum, gates)
        free = jnp.ones((HEADS, 1), F32)
        for n in range(n_past):
            hit = jnp.where(gates[n] == best, free, 0.0)
            free = free - hit
            sel[n] = jnp.maximum(sel[n], hit)
            gates[n] = jnp.where(hit > 0.5, LOWEST, gates[n])
    qk = qc * kc
    s_self = by_head(lambda h: jnp.sum(head(qk, h), axis=0, keepdims=True)) * SCALE
    s = [jnp.where(sel[j // pages_per_block] > 0.5, raw[j] * SCALE, NEG_INF)
         for j in range(n_pages)]
    m = s_self
    for sj in s:
        m = jnp.maximum(m, jnp.max(sj, axis=1, keepdims=True))
    p_self = jnp.exp(s_self - m)
    p = [jnp.exp(sj - m) for sj in s]
    l = p_self
    for pj in p:
        l = l + jnp.sum(pj, axis=1, keepdims=True)

    def attend(h):
        acc = v_refs[0][0, h] * p[0][h:h + 1, :]
        for j in range(1, n_pages):
            acc = acc + v_refs[j][0, h] * p[j][h:h + 1, :]
        return ((jnp.sum(acc, axis=1, keepdims=True) + p_self[h:h + 1, :] * head(vc, h))
                / l[h:h + 1, :])

    o_ref[...] = jnp.where(lane_is_seq, by_head(attend), o_ref[...])


def _moba_dec_guest(proj_t, cache_k, cache_v, page_table):
    b = proj_t.shape[1]
    n_pages = page_table.shape[1]
    page = cache_k.shape[3]
    assert b == LANES
    vec = lambda j: ((W_ATT, b), lambda bi, pt: (j, 0))
    pg = lambda j: ((1, HEADS, DH, page), lambda bi, pt: (pt[bi * n_pages + j], 0, 0, 0))
    pages = [pg(j) for j in range(n_pages)]
    return _Guest(
        in_specs=[vec(0), vec(1), vec(2)] + pages + pages,
        args=[proj_t] * 3 + [cache_k] * n_pages + [cache_v] * n_pages,
        out_specs=[((W_ATT, b), lambda bi, pt: (0, 0))],
        out_shape=[jax.ShapeDtypeStruct((W_ATT, b), F32)],
        init=_zero_refs,
        body=functools.partial(_moba_dec_body, n_pages=n_pages, page=page),
        prefetch=page_table.reshape(-1))


def _hgrn_seq_kernel(q_ref, f_ref, i_ref, g_ref, lb_ref, gn_ref, o_ref, s_ref, st_scr, *, tb):
    t = pl.program_id(0)

    @pl.when(t == 0)
    def _():
        st_scr[...] = jnp.zeros(st_scr.shape, F32)

    c = HGRN_CHUNK
    nch = tb // c
    w = q_ref.shape[2]
    lb = lb_ref[...]
    gn = gn_ref[...]
    r_i = lax.broadcasted_iota(jnp.int32, (tb, tb), 0)
    c_i = lax.broadcasted_iota(jnp.int32, (tb, tb), 1)
    tril = jnp.logical_and(c_i <= r_i, c_i // c == r_i // c)
    ones_tril = jnp.where(tril, 1.0, 0.0)
    per_chunk = lambda x, row: jnp.broadcast_to(
        x.reshape(nch, c, w)[:, row:row + 1, :], (nch, c, w)).reshape(tb, w)
    for bi in range(q_ref.shape[0]):
        f = lb + (1.0 - lb) * _sigmoid(f_ref[bi])
        k = 1.0 - f
        q = _silu(q_ref[bi])
        v = i_ref[bi]
        b = jnp.dot(ones_tril, jnp.log(f), precision=HIGHEST, preferred_element_type=F32)
        b_mid = per_chunk(b, c // 2 - 1)
        b_last = per_chunk(b, c - 1)
        q_mid = (q * jnp.exp(b - b_mid)).astype(BF16)
        k_mid = (k * jnp.exp(b_mid - b)).astype(BF16)
        k_end = (k * jnp.exp(b_last - b)).astype(BF16)
        q_in = (q * jnp.exp(b)).astype(BF16)
        decay = jnp.exp(b_last)
        vb = v.astype(BF16)
        outs = []
        for h in range(H_B):
            cols = slice(h * D_B, (h + 1) * D_B)
            a = jnp.where(tril, _dot_t(q_mid[:, cols], k_mid[:, cols]), 0.0)
            o_intra = _dot(a.astype(BF16), vb[:, cols])
            o_inter = []
            st = st_scr[bi, h]
            for ch in range(nch):
                rows = slice(ch * c, (ch + 1) * c)
                o_inter.append(_dot_t(q_in[rows, cols], st.astype(BF16)))
                st = (st * decay[ch * c:ch * c + 1, cols]
                      + _dot(v[rows, cols].T.astype(BF16), k_end[rows, cols]))
            st_scr[bi, h] = st
            outs.append(_rms(o_intra + jnp.concatenate(o_inter, axis=0), gn[:, cols]))
        o_ref[bi] = jnp.concatenate(outs, axis=1) * _silu(g_ref[bi])

    @pl.when(t == pl.num_programs(0) - 1)
    def _():
        for bi in range(q_ref.shape[0]):
            for h in range(H_B):
                s_ref[bi, h] = st_scr[bi, h].T


def _hgrn_seq(proj, lb, gn, first_chunk, tb):
    b, s, _ = proj.shape
    w = H_B * D_B
    blk = lambda j: pl.BlockSpec((b, tb, w), lambda t: (0, t, first_chunk + j))
    vec = pl.BlockSpec((1, w), lambda t: (0, 0))
    return pl.pallas_call(
        functools.partial(_hgrn_seq_kernel, tb=tb),
        grid=(s // tb,),
        in_specs=[blk(0), blk(1), blk(2), blk(3), vec, vec],
        out_specs=[pl.BlockSpec((b, tb, w), lambda t: (0, t, 0)),
                   pl.BlockSpec((b, H_B, D_B, D_B), lambda t: (0, 0, 0, 0))],
        out_shape=[jax.ShapeDtypeStruct((b, s, w), F32),
                   jax.ShapeDtypeStruct((b, H_B, D_B, D_B), F32)],
        scratch_shapes=[pltpu.VMEM((b, H_B, D_B, D_B), F32)],
        compiler_params=_params("arbitrary"),
    )(proj, proj, proj, proj, lb.reshape(1, w), jnp.tile(gn.astype(F32), H_B).reshape(1, w))


def _hgrn_step_kernel(q_ref, f_ref, i_ref, g_ref, lb_ref, gn_ref, s0_ref, o_ref, s_ref):
    bt = s0_ref.shape[0]
    rows = pl.ds(pl.multiple_of(pl.program_id(0) * bt, bt), bt)
    lb = lb_ref[...]
    gn = gn_ref[...]
    f_all = lb + (1.0 - lb) * _sigmoid(f_ref[rows, :])
    q_all = _silu(q_ref[rows, :])
    v_all = i_ref[rows, :]
    gate_all = _silu(g_ref[rows, :])
    eye = (lax.broadcasted_iota(jnp.int32, (D_B, D_B), 0)
           == lax.broadcasted_iota(jnp.int32, (D_B, D_B), 1))
    for j in range(bt):
        f, q, v = f_all[j:j + 1], q_all[j:j + 1], v_all[j:j + 1]
        outs = []
        for h in range(H_B):
            cols = slice(h * D_B, (h + 1) * D_B)
            f_col = jnp.sum(jnp.where(eye, f[:, cols], 0.0), axis=1, keepdims=True)
            q_col = jnp.sum(jnp.where(eye, q[:, cols], 0.0), axis=1, keepdims=True)
            s_new = f_col * s0_ref[j, h] + (1.0 - f_col) * v[:, cols]
            s_ref[j, h] = s_new
            o = jnp.sum(q_col * s_new, axis=0, keepdims=True)
            outs.append(_rms(o, gn[:, cols]))
        o_ref[j:j + 1, :] = jnp.concatenate(outs, axis=1) * gate_all[j:j + 1]


def _hgrn_step(proj, lb, gn, s0, first_chunk, bt):
    b = proj.shape[0]
    w = H_B * D_B
    blk = lambda j: pl.BlockSpec((b, w), lambda bi: (0, first_chunk + j))
    vec = pl.BlockSpec((1, w), lambda bi: (0, 0))
    st = pl.BlockSpec((bt, H_B, D_B, D_B), lambda bi: (bi, 0, 0, 0))
    return pl.pallas_call(
        _hgrn_step_kernel,
        grid=(b // bt,),
        in_specs=[blk(0), blk(1), blk(2), blk(3), vec, vec, st],
        out_specs=[pl.BlockSpec((bt, w), lambda bi: (bi, 0)), st],
        out_shape=[jax.ShapeDtypeStruct((b, w), F32),
                   jax.ShapeDtypeStruct((b, H_B, D_B, D_B), F32)],
        compiler_params=_params("parallel"),
    )(proj, proj, proj, proj, lb.reshape(1, w),
      jnp.tile(gn.astype(F32), H_B).reshape(1, w), s0)


def _band_kernel(q_ref, kc_ref, kp_ref, vc_ref, vp_ref, o_ref, l_ref, *, tq, dil):
    first = pl.program_id(1) == 0
    r = pl.program_id(2)
    row = lax.broadcasted_iota(jnp.int32, (BAND, 2 * BAND), 0)
    col = lax.broadcasted_iota(jnp.int32, (BAND, 2 * BAND), 1)
    band = jnp.logical_and(col >= row, col <= row + BAND)
    lane = lax.broadcasted_iota(jnp.int32, (BAND, LANES), 1)
    low = lane < DH
    for i in range(tq // BAND):
        rows = slice(i * BAND, (i + 1) * BAND)
        if i == 0:
            k_prev, v_prev = kp_ref[0, 0], vp_ref[0, 0]
            mask = jnp.logical_and(band, jnp.logical_or(col >= BAND, jnp.logical_not(first)))
        else:
            prev = slice((i - 1) * BAND, i * BAND)
            k_prev, v_prev = kc_ref[0, 0, prev, :], vc_ref[0, 0, prev, :]
            mask = band
        kk = jnp.concatenate([k_prev, kc_ref[0, 0, rows, :]], axis=0).astype(BF16)
        vv = jnp.concatenate([v_prev, vc_ref[0, 0, rows, :]], axis=0).astype(BF16)
        out_rows = rows if dil == 1 else pl.ds(r + i * BAND * dil, BAND, stride=dil)
        for hp in range(HEADS // 2):
            cols = slice(hp * LANES, (hp + 1) * LANES)
            q2 = q_ref[0, 0, rows, cols] * SCALE
            res = []
            for e in range(2):
                mine = low if e == 0 else jnp.logical_not(low)
                s = _dot_t(jnp.where(mine, q2, 0.0).astype(BF16), kk[:, cols])
                s = jnp.where(mask, s, NEG_INF)
                m = jnp.max(s, axis=1, keepdims=True)
                p = jnp.exp(s - m)
                l = jnp.sum(p, axis=1, keepdims=True)
                res.append((_dot((p / l).astype(BF16), vv[:, cols]), m + jnp.log(l)))
            o_ref[0, hp, out_rows, :] = jnp.where(low, res[0][0], res[1][0])
            l_ref[0, hp, out_rows, :] = jnp.where(low, res[0][1], res[1][1])


def _band_attention(proj, tq):
    b, dil, ln, _ = proj.shape
    sub = tq // BAND
    cur = lambda comp: pl.BlockSpec((1, 1, tq, W_ATT), lambda bi, l, r: (bi, r, l, comp))
    prev = lambda comp: pl.BlockSpec((1, 1, BAND, W_ATT),
                                     lambda bi, l, r: (bi, r, jnp.maximum(l * sub - 1, 0), comp))
    out = pl.BlockSpec((1, HEADS // 2, tq * dil, LANES), lambda bi, l, r: (bi, 0, l, 0))
    shape = jax.ShapeDtypeStruct((b, HEADS // 2, ln * dil, LANES), F32)
    return pl.pallas_call(
        functools.partial(_band_kernel, tq=tq, dil=dil),
        grid=(b, ln // tq, dil),
        in_specs=[cur(0), cur(1), prev(1), cur(2), prev(2)],
        out_specs=[out, out],
        out_shape=[shape, shape],
        compiler_params=_params("parallel", "arbitrary", "arbitrary"),
    )(proj, proj, proj, proj, proj)


def _dil_cache_body(step, _, in_refs, out_refs, *, dils, hb, head_groups):
    bi, hg = step // head_groups, step % head_groups
    for g, dil in enumerate(dils):
        _dil_cache_group(bi, hg, *in_refs[5 * g:5 * g + 5], *out_refs[4 * g:4 * g + 4],
                         dil=dil, hb=hb)


def _dil_cache_init(out_refs):
    for g in range(len(out_refs) // 4):
        _zero_refs(out_refs[4 * g + 2:4 * g + 4])


def _dil_cache_kernel(*refs, dils, hb):
    n_in = 5 * len(dils)

    @pl.when(jnp.logical_and(pl.program_id(0) == 0, pl.program_id(1) == 0))
    def _():
        _dil_cache_init(refs[n_in:])

    _dil_cache_body(pl.program_id(0) * pl.num_programs(1) + pl.program_id(1), None,
                    refs[:n_in], refs[n_in:], dils=dils, hb=hb, head_groups=HEADS // hb)


def _dil_cache_group(bi, hg, q_ref, kn_ref, vn_ref, ck_ref, cv_ref, ok_ref, ov_ref, o_ref, l_ref,
                     *, dil, hb):
    lw = ck_ref.shape[3]
    nr = hb * DH
    rows = pl.ds(pl.multiple_of(hg * nr, nr), nr)
    lane_is_seq = lax.broadcasted_iota(jnp.int32, (nr, LANES), 1) == bi
    pos = lax.broadcasted_iota(jnp.int32, (hb, lw), 1)
    on_grid = (pos & (dil - 1)) == 0
    newest = lax.broadcasted_iota(jnp.int32, (DH, lw), 1) == lw - 1
    qc = _column(q_ref, rows, lane_is_seq) * SCALE
    kc = _column(kn_ref, rows, lane_is_seq)
    vc = _column(vn_ref, rows, lane_is_seq)
    head = lambda x, hh: x[hh * DH:(hh + 1) * DH]
    s = jnp.concatenate([jnp.sum(ck_ref[0, hh] * head(qc, hh), axis=0, keepdims=True)
                         for hh in range(hb)], axis=0)
    s = jnp.where(on_grid, s, NEG_INF)
    qk = qc * kc
    s_self = jnp.concatenate([jnp.sum(head(qk, hh), axis=0, keepdims=True)
                              for hh in range(hb)], axis=0)
    m = jnp.maximum(jnp.max(s, axis=1, keepdims=True), s_self)
    p = jnp.exp(s - m)
    p_self = jnp.exp(s_self - m)
    l = jnp.sum(p, axis=1, keepdims=True) + p_self
    pn = p / l
    w_self = p_self / l
    lse = m + jnp.log(l)
    os, ls = [], []
    for hh in range(hb):
        k = ck_ref[0, hh]
        v = cv_ref[0, hh]
        os.append(jnp.sum(v * pn[hh:hh + 1, :], axis=1, keepdims=True)
                  + w_self[hh:hh + 1, :] * head(vc, hh))
        ls.append(jnp.broadcast_to(lse[hh:hh + 1, :], (DH, 1)))
        ok_ref[0, hh] = jnp.where(newest, head(kc, hh), pltpu.roll(k, lw - 1, axis=1))
        ov_ref[0, hh] = jnp.where(newest, head(vc, hh), pltpu.roll(v, lw - 1, axis=1))
    o_ref[rows, :] = jnp.where(lane_is_seq, jnp.concatenate(os, axis=0), o_ref[rows, :])
    l_ref[rows, :] = jnp.where(lane_is_seq, jnp.concatenate(ls, axis=0), l_ref[rows, :])


def _dil_cache_guest(proj_t, n_groups, groups, dils, caches, hb):
    b, h, dh, _ = caches[0][0].shape
    hgs = h // hb
    assert b == LANES and all(dil & (dil - 1) == 0 for dil in dils)
    vec = lambda j: ((W_ATT, b), lambda step, *_: (j, 0))
    res = ((W_ATT, b), lambda step, *_: (0, 0))
    res_shape = jax.ShapeDtypeStruct((W_ATT, b), F32)
    in_specs, out_specs, out_shape, args = [], [], [], []
    for g, (ck, cv) in zip(groups, caches):
        cache = ((1, hb, dh, ck.shape[3]), lambda step, *_: (step // hgs, step % hgs, 0, 0))
        in_specs += [vec(g), vec(n_groups + g), vec(2 * n_groups + g), cache, cache]
        args += [proj_t, proj_t, proj_t, ck, cv]
        out_specs += [cache, cache, res, res]
        out_shape += [jax.ShapeDtypeStruct(ck.shape, F32), jax.ShapeDtypeStruct(cv.shape, F32),
                      res_shape, res_shape]
    return _Guest(in_specs, args, out_specs, out_shape, _dil_cache_init,
                  functools.partial(_dil_cache_body, dils=tuple(dils), hb=hb, head_groups=hgs))


def _dil_cache(guest, b, head_groups):
    by_step = lambda f: (lambda bi, hg: f(bi * head_groups + hg))
    return pl.pallas_call(
        functools.partial(_dil_cache_kernel, dils=guest.body.keywords["dils"],
                          hb=guest.body.keywords["hb"]),
        grid=(b, head_groups),
        in_specs=[pl.BlockSpec(shape, by_step(f)) for shape, f in guest.in_specs],
        out_specs=[pl.BlockSpec(shape, by_step(f)) for shape, f in guest.out_specs],
        out_shape=guest.out_shape,
        compiler_params=_params("arbitrary", "arbitrary"),
    )(*guest.args)


def kernel(x_prompt, x_sample, cache_k_a, cache_v_a, state_hgrn, cache_k_c0, cache_v_c0, cache_k_c1, cache_v_c1, cache_k_c2, cache_v_c2, state_conv, page_table, norm_mix, norm_ffn, w_in_ab, q_norm_a, k_norm_a, lb_logits, g_norm_b, w_out_ab, w_in_c, q_norm_c, k_norm_c, w_out_c, w_gate, w_up, conv_w, conv_b, w_down):
    bsz, seq, d = x_prompt.shape
    nb = x_sample.shape[0]
    n = bsz * seq
    dff = w_gate.shape[2]
    tm, tc = 512, dff // 2
    ng = len(C_PATTERNS)
    c_caches = ((cache_k_c0, cache_v_c0), (cache_k_c1, cache_v_c1), (cache_k_c2, cache_v_c2))

    lb_all = jnp.cumsum(jax.nn.softmax(lb_logits.astype(F32), axis=0), axis=0)
    bf = lambda w: w.astype(BF16)

    xp = x_prompt.reshape(n, d)
    xs = x_sample.reshape(nb, d)

    w_in = bf(w_in_ab[0])
    gains = _head_gains([q_norm_a[0], k_norm_a[0]])
    w_out = bf(w_out_ab[0])
    lb = lb_all[0]

    pp, k_t, v_t = _norm_proj(xp, norm_mix[0], w_in, gains, tm, transposed=(1, 2), n_seq=bsz)
    pp = pp.reshape(bsz, seq, -1)
    qa, ka, va = _moba_prep(pp)
    oa = _moba_attn(qa, ka, va, 512, HEADS).reshape(n, W_ATT)
    ob, hg_p = _hgrn_seq(pp, lb, g_norm_b[0], 3, 256)
    xp = _out_ab(oa, ob.reshape(n, -1), w_out, xp, tm)

    pos_minor = lambda c: jnp.transpose(c, (0, 2, 3, 1))
    pos_major = lambda c: jnp.transpose(c, (0, 3, 1, 2))
    k_a_p = pos_major(k_t.reshape(bsz, HEADS, DH, seq))[None]
    v_a_p = pos_major(v_t.reshape(bsz, HEADS, DH, seq))[None]

    conv_p, conv_s = [], []
    tm_ffn, nc_ffn = 256, 2

    def ffn(l, xp, guest):
        wg, wu, wd = bf(w_gate[l]), bf(w_up[l]), bf(w_down[l])
        xp, cp, guest_out = _ffn_seq(xp, norm_ffn[l], wg, wu, conv_w[l], conv_b[l], wd,
                                     bsz, tm_ffn, nc_ffn, guest)

        def sample(xs):
            xs, a_s = _ffn_step(xs, norm_ffn[l], wg, wu, conv_w[l], conv_b[l], wd,
                                state_conv[l, :, 0, :], state_conv[l, :, 1, :], tc)
            conv_s.append(jnp.stack([state_conv[l, :, 1, :], a_s], axis=1))
            return xs

        conv_p.append(cp)
        return xp, sample, guest_out

    ps = _norm_proj(xs, norm_mix[0], w_in, gains, nb)
    assert n // tm_ffn * nc_ffn == nb
    xp, ffn_sample, (oas_t,) = ffn(0, xp, _moba_dec_guest(
        ps.T, pos_minor(cache_k_a[0]), pos_minor(cache_v_a[0]), page_table))
    obs, hg_s = _hgrn_step(ps, lb, g_norm_b[0], state_hgrn[0], 3, 8)
    xs = ffn_sample(_out_ab_t(oas_t, obs, w_out, xs))
    k_a_s = ps[:, W_ATT:2 * W_ATT].reshape(1, nb, 1, HEADS, DH)
    v_a_s = ps[:, 2 * W_ATT:3 * W_ATT].reshape(1, nb, 1, HEADS, DH)

    w_in = bf(w_in_c[0])
    w_out = bf(w_out_c[0])
    gains = _head_gains([q_norm_c[0], k_norm_c[0]])

    os, ls, c_prompt = [], [], []
    for g, (win, dil) in enumerate(C_PATTERNS):
        ln = seq // dil
        w_g = jnp.concatenate([w_in[:, (c * ng + g) * W_ATT:(c * ng + g + 1) * W_ATT]
                               for c in range(3)], axis=1)
        xr = xp.reshape(bsz, ln, dil, d).transpose(0, 2, 1, 3).reshape(n, d)
        pg = _norm_proj(xr, norm_mix[1], w_g, gains, 2 * tm)
        o, lse = _band_attention(pg.reshape(bsz, dil, ln, 3 * W_ATT), min(512, 4096 // dil))
        os.append(o)
        ls.append(lse)
        keep = min(win, seq)
        kd = keep // dil
        tail = lax.optimization_barrier(lax.slice(
            pg.reshape(bsz * dil, ln, 3 * W_ATT), (0, ln - kd, W_ATT), (bsz * dil, ln, 3 * W_ATT)))
        tail = tail.reshape(bsz, dil, kd, 2, HEADS, DH).transpose(3, 0, 2, 1, 4, 5)
        tail = tail.reshape(2, 1, bsz, keep, HEADS, DH)
        c_prompt.append((tail[0], tail[1]))
    xp = _out_c(os, ls, w_out, xp, tm)

    gains = _head_gains([q_norm_c[0]] * ng + [k_norm_c[0]] * ng)
    ps_t = _norm_proj(xs, norm_mix[1], w_in, gains, nb).T
    dils = [dil for _, dil in C_PATTERNS]
    views = [(pos_minor(ck[0]), pos_minor(cv[0])) for ck, cv in c_caches]
    hosted = [g for g in range(ng) if C_PATTERNS[g][0] <= 512]
    alone = [g for g in range(ng) if g not in hosted]
    pick = lambda xs_, gs: [xs_[g] for g in gs]
    xp, ffn_sample, hosted_out = ffn(1, xp, _dil_cache_guest(
        ps_t, ng, hosted, pick(dils, hosted), pick(views, hosted), HEADS))
    alone_out = _dil_cache(_dil_cache_guest(ps_t, ng, alone, pick(dils, alone), pick(views, alone),
                                            HEADS), nb, 1)
    decoded = dict(zip(hosted + alone,
                       [outs[4 * j:4 * j + 4] for outs in (hosted_out, alone_out)
                        for j in range(len(outs) // 4)]))
    c_out, os_t, ls_t = [], [], []
    for g in range(ng):
        nk, nv, o_t, l_t = decoded[g]
        c_out += list(c_prompt[g])
        c_out += [pos_major(nk)[None], pos_major(nv)[None]]
        os_t.append(o_t)
        ls_t.append(l_t)
    xs = ffn_sample(_out_c_t(os_t, ls_t, w_out, xs))

    return (xp.reshape(bsz, seq, d), xs.reshape(nb, 1, d),
            k_a_p, v_a_p, k_a_s, v_a_s, hg_p[None], hg_s[None],
            *c_out, jnp.stack(conv_p), jnp.stack(conv_s))
```

```python
import functools

import jax
import jax.numpy as jnp
from jax import lax
from jax.experimental import pallas as pl
from jax.experimental.pallas import tpu as pltpu

F32 = jnp.float32
BF16 = jnp.bfloat16
HIGHEST = lax.Precision.HIGHEST

RMS_EPS = 1e-6
NEG_INF = -1e30
LOWEST = -3e38

LANES = 128
DH = 64
HEADS = 8
W_ATT = HEADS * DH
SCALE = DH ** -0.5
MOBA_BLOCK = 256
MOBA_TOPK = 3
H_B = 4
D_B = 128
HGRN_CHUNK = 32
BAND = 128
C_PATTERNS = ((128, 1), (512, 4), (2048, 16))
VMEM_LIMIT = 48 * 1024 * 1024
VMEM_LIMIT_HOSTED = 56 * 1024 * 1024

_CONTRACT_LAST = (((1,), (1,)), ((), ()))


def _params(*sem, vmem=VMEM_LIMIT):
    return pltpu.CompilerParams(dimension_semantics=sem, vmem_limit_bytes=vmem)


def _sigmoid(x):
    return 1.0 / (1.0 + jnp.exp(-x))


def _silu(x):
    return x * _sigmoid(x)


def _rms(x, g):
    return x * lax.rsqrt(jnp.mean(x * x, axis=-1, keepdims=True) + RMS_EPS) * g


def _dot(a, b):
    return jnp.dot(a, b, preferred_element_type=F32)


def _dot_t(a, b, precision=None):
    return lax.dot_general(a, b, _CONTRACT_LAST, precision=precision, preferred_element_type=F32)


def _norm_proj_kernel(x_ref, g_ref, w_ref, hg_ref, p_ref, o_ref, *t_refs, n_norm, transposed):
    h = _rms(x_ref[...], g_ref[...]).astype(BF16)
    for j in range(w_ref.shape[1] // W_ATT):
        cols = slice(j * W_ATT, (j + 1) * W_ATT)
        y = _dot(h, w_ref[:, cols])
        if j < n_norm:
            ms = _dot((y * y).astype(BF16), p_ref[...])
            y = y * lax.rsqrt(ms + RMS_EPS) * hg_ref[j:j + 1, :]
        o_ref[:, cols] = y
        if j in transposed:
            t_refs[transposed.index(j)][0] = y.T


def _norm_proj(x, g, w_bf, head_gains, tm, transposed=(), n_seq=1):
    n, d = x.shape
    m = w_bf.shape[1]
    n_norm = head_gains.shape[0]
    per_seq = n // n_seq // tm
    blockdiag = (jnp.arange(W_ATT)[:, None] // DH == jnp.arange(W_ATT)[None, :] // DH)
    p = (blockdiag.astype(F32) / DH).astype(BF16)
    t_spec = pl.BlockSpec((1, W_ATT, tm), lambda i: (i // per_seq, 0, i % per_seq))
    t_shape = jax.ShapeDtypeStruct((n_seq, W_ATT, n // n_seq), F32)
    outs = pl.pallas_call(
        functools.partial(_norm_proj_kernel, n_norm=n_norm, transposed=tuple(transposed)),
        grid=(n // tm,),
        in_specs=[
            pl.BlockSpec((tm, d), lambda i: (i, 0)),
            pl.BlockSpec((1, d), lambda i: (0, 0)),
            pl.BlockSpec((d, m), lambda i: (0, 0)),
            pl.BlockSpec((n_norm, W_ATT), lambda i: (0, 0)),
            pl.BlockSpec((W_ATT, W_ATT), lambda i: (0, 0)),
        ],
        out_specs=[pl.BlockSpec((tm, m), lambda i: (i, 0))] + [t_spec] * len(transposed),
        out_shape=[jax.ShapeDtypeStruct((n, m), F32)] + [t_shape] * len(transposed),
        compiler_params=_params("parallel"),
    )(x, g.reshape(1, d), w_bf, head_gains, p)
    return outs if transposed else outs[0]


def _head_gains(gains):
    return jnp.stack([jnp.tile(gv.astype(F32), HEADS) for gv in gains])


class _Guest:
    def __init__(self, in_specs, args, out_specs, out_shape, init, body, prefetch=None):
        self.in_specs, self.args = in_specs, args
        self.out_specs, self.out_shape = out_specs, out_shape
        self.init, self.body, self.prefetch = init, body, prefetch


def _ffn_seq_kernel(*refs, tm, blocks_per_seq, guest):
    refs = list(refs)
    pt_ref = refs.pop(0) if guest is not None and guest.prefetch is not None else None
    n_gi = len(guest.in_specs) if guest is not None else 0
    n_go = len(guest.out_specs) if guest is not None else 0
    x_ref, g_ref, wg_ref, wu_ref, cw_ref, cb_ref, wd_ref = refs[:7]
    guest_in = refs[7:7 + n_gi]
    o_ref, cs_ref = refs[7 + n_gi:9 + n_gi]
    guest_out = refs[9 + n_gi:9 + n_gi + n_go]
    a_scr, acc_scr = refs[9 + n_gi + n_go:]
    i = pl.program_id(0)
    c = pl.program_id(1)
    step = i * pl.num_programs(1) + c

    @pl.when(i % blocks_per_seq == 0)
    def _():
        a_scr[c, 0:8, :] = jnp.zeros((8, a_scr.shape[2]), F32)

    @pl.when(c == 0)
    def _():
        acc_scr[...] = jnp.zeros(acc_scr.shape, F32)

    if guest is not None:
        @pl.when(step == 0)
        def _():
            guest.init(guest_out)

    tc = a_scr.shape[2]
    cols = pl.ds(pl.multiple_of(c * tc, LANES), tc)
    h = _rms(x_ref[...], g_ref[...]).astype(BF16)
    a = _dot(h, wg_ref[:, cols])
    u = _dot(h, wu_ref[:, cols])
    a_scr[c, 8:8 + tm, :] = a
    a1 = a_scr[c, 7:7 + tm, :]
    a2 = a_scr[c, 6:6 + tm, :]
    cw = cw_ref[:, cols]
    conv = cb_ref[:, cols] + a2 * cw[0:1] + a1 * cw[1:2] + a * cw[2:3]
    y = (_silu(conv) * u).astype(BF16)
    acc = acc_scr[...] + _dot(y, wd_ref[c])
    acc_scr[...] = acc
    a_scr[c, 0:8, :] = a_scr[c, tm:tm + 8, :]
    cs_ref[0, c] = a_scr[c, tm + 6:tm + 8, :]
    o_ref[...] = x_ref[...] + acc
    if guest is not None:
        guest.body(step, pt_ref, guest_in, guest_out)


def _ffn_seq(x, g, wg, wu, cw, cb, wd, n_seq, tm, nc, guest=None):
    n, d = x.shape
    dff = wg.shape[1]
    tc = dff // nc
    bps = n // n_seq // tm
    resident = lambda shape: pl.BlockSpec(shape, lambda i, c, *_: (0,) * len(shape),
                                          pipeline_mode=pl.Buffered(1))
    in_specs = [
        pl.BlockSpec((tm, d), lambda i, c, *_: (i, 0)),
        resident((1, d)),
        resident((d, dff)), resident((d, dff)), resident((3, dff)), resident((1, dff)),
        resident((nc, tc, d)),
    ]
    args = [x, g.reshape(1, d), wg, wu, cw, cb.reshape(1, dff), wd.reshape(nc, tc, d)]
    out_specs = [pl.BlockSpec((tm, d), lambda i, c, *_: (i, 0)),
                 pl.BlockSpec((1, nc, 2, tc), lambda i, c, *_: (i // bps, 0, 0, 0))]
    out_shape = [jax.ShapeDtypeStruct((n, d), F32), jax.ShapeDtypeStruct((n_seq, nc, 2, tc), F32)]
    prefetch = []
    if guest is not None:
        by_step = lambda f: (lambda i, c, *pf: f(i * nc + c, *pf))
        in_specs += [pl.BlockSpec(shape, by_step(f)) for shape, f in guest.in_specs]
        out_specs += [pl.BlockSpec(shape, by_step(f)) for shape, f in guest.out_specs]
        args += guest.args
        out_shape += guest.out_shape
        prefetch = [] if guest.prefetch is None else [guest.prefetch]
    outs = pl.pallas_call(
        functools.partial(_ffn_seq_kernel, tm=tm, blocks_per_seq=bps, guest=guest),
        grid_spec=pltpu.PrefetchScalarGridSpec(
            num_scalar_prefetch=len(prefetch),
            grid=(n // tm, nc),
            in_specs=in_specs,
            out_specs=out_specs,
            scratch_shapes=[pltpu.VMEM((nc, tm + 8, tc), F32), pltpu.VMEM((tm, d), F32)],
        ),
        out_shape=out_shape,
        compiler_params=_params("arbitrary", "arbitrary", vmem=VMEM_LIMIT_HOSTED),
    )(*prefetch, *args)
    conv_state = outs[1].transpose(0, 2, 1, 3).reshape(n_seq, 2, dff)
    return outs[0], conv_state, outs[2:]


def _ffn_step_kernel(x_ref, g_ref, wg_ref, wu_ref, cw_ref, cb_ref, wd_ref, s0_ref, s1_ref,
                     o_ref, a_ref, h_scr, acc_scr):
    c = pl.program_id(0)

    @pl.when(c == 0)
    def _():
        h_scr[...] = _rms(x_ref[...], g_ref[...]).astype(BF16)

    h = h_scr[...]
    a = _dot(h, wg_ref[...])
    u = _dot(h, wu_ref[...])
    a_ref[...] = a
    cw = cw_ref[...]
    conv = cb_ref[...] + s0_ref[...] * cw[0:1] + s1_ref[...] * cw[1:2] + a * cw[2:3]
    y = (_silu(conv) * u).astype(BF16)
    part = _dot(y, wd_ref[...])

    @pl.when(c == 0)
    def _():
        acc_scr[...] = part

    @pl.when(c > 0)
    def _():
        acc_scr[...] = acc_scr[...] + part

    @pl.when(c == pl.num_programs(0) - 1)
    def _():
        o_ref[...] = x_ref[...] + acc_scr[...]


def _ffn_step(x, g, wg, wu, cw, cb, wd, s0, s1, tc):
    n, d = x.shape
    dff = wg.shape[1]
    return pl.pallas_call(
        _ffn_step_kernel,
        grid=(dff // tc,),
        in_specs=[
            pl.BlockSpec((n, d), lambda c: (0, 0)),
            pl.BlockSpec((1, d), lambda c: (0, 0)),
            pl.BlockSpec((d, tc), lambda c: (0, c)),
            pl.BlockSpec((d, tc), lambda c: (0, c)),
            pl.BlockSpec((3, tc), lambda c: (0, c)),
            pl.BlockSpec((1, tc), lambda c: (0, c)),
            pl.BlockSpec((tc, d), lambda c: (c, 0)),
            pl.BlockSpec((n, tc), lambda c: (0, c)),
            pl.BlockSpec((n, tc), lambda c: (0, c)),
        ],
        out_specs=[pl.BlockSpec((n, d), lambda c: (0, 0)),
                   pl.BlockSpec((n, tc), lambda c: (0, c))],
        out_shape=[jax.ShapeDtypeStruct((n, d), F32), jax.ShapeDtypeStruct((n, dff), F32)],
        scratch_shapes=[pltpu.VMEM((n, d), BF16), pltpu.VMEM((n, d), F32)],
        compiler_params=_params("arbitrary"),
    )(x, g.reshape(1, d), wg, wu, cw, cb.reshape(1, dff), wd, s0, s1)


def _out_ab_kernel(oa_ref, ob_ref, w_ref, x_ref, o_ref):
    ka = oa_ref.shape[1]
    o_ref[...] = (x_ref[...] + _dot(oa_ref[...].astype(BF16), w_ref[0:ka, :])
                  + _dot(ob_ref[...].astype(BF16), w_ref[ka:, :]))


def _out_ab(oa, ob, w_bf, x, tm):
    n, d = x.shape
    ka, kb = oa.shape[1], ob.shape[1]
    return pl.pallas_call(
        _out_ab_kernel,
        grid=(n // tm,),
        in_specs=[pl.BlockSpec((tm, ka), lambda i: (i, 0)),
                  pl.BlockSpec((tm, kb), lambda i: (i, 0)),
                  pl.BlockSpec((ka + kb, d), lambda i: (0, 0)),
                  pl.BlockSpec((tm, d), lambda i: (i, 0))],
        out_specs=pl.BlockSpec((tm, d), lambda i: (i, 0)),
        out_shape=jax.ShapeDtypeStruct((n, d), F32),
        compiler_params=_params("parallel"),
    )(oa, ob, w_bf, x)


def _merge_groups(os, ls):
    m = jnp.maximum(jnp.maximum(ls[0], ls[1]), ls[2])
    es = [jnp.exp(l - m) for l in ls]
    den = es[0] + es[1] + es[2]
    return (es[0] / den) * os[0] + (es[1] / den) * os[1] + (es[2] / den) * os[2]


def _out_c_kernel(o0, o1, o2, l0, l1, l2, w_ref, x_ref, o_ref):
    merged = jnp.concatenate(
        [_merge_groups([o0[0, hp], o1[0, hp], o2[0, hp]], [l0[0, hp], l1[0, hp], l2[0, hp]])
         for hp in range(HEADS // 2)], axis=1)
    o_ref[...] = x_ref[...] + _dot(merged.astype(BF16), w_ref[...])


def _out_c(os, ls, w_bf, x, tm):
    n, d = x.shape
    k = w_bf.shape[0]
    per_seq = os[0].shape[2] // tm
    act = pl.BlockSpec((1, HEADS // 2, tm, LANES), lambda i: (i // per_seq, 0, i % per_seq, 0))
    return pl.pallas_call(
        _out_c_kernel,
        grid=(n // tm,),
        in_specs=[act] * 6 + [pl.BlockSpec((k, d), lambda i: (0, 0)),
                              pl.BlockSpec((tm, d), lambda i: (i, 0))],
        out_specs=pl.BlockSpec((tm, d), lambda i: (i, 0)),
        out_shape=jax.ShapeDtypeStruct((n, d), F32),
        compiler_params=_params("parallel"),
    )(*os, *ls, w_bf, x)


def _out_ab_t_kernel(oat_ref, ob_ref, w_ref, x_ref, o_ref):
    ka = oat_ref.shape[0]
    o_ref[...] = (x_ref[...] + _dot(oat_ref[...].T.astype(BF16), w_ref[0:ka, :])
                  + _dot(ob_ref[...].astype(BF16), w_ref[ka:, :]))


def _out_ab_t(oa_t, ob, w_bf, x):
    return pl.pallas_call(
        _out_ab_t_kernel,
        out_shape=jax.ShapeDtypeStruct(x.shape, F32),
        compiler_params=pltpu.CompilerParams(vmem_limit_bytes=VMEM_LIMIT),
    )(oa_t, ob, w_bf, x)


def _out_c_t_kernel(o0, o1, o2, l0, l1, l2, w_ref, x_ref, o_ref):
    merged = _merge_groups([o0[...], o1[...], o2[...]], [l0[...], l1[...], l2[...]])
    o_ref[...] = x_ref[...] + _dot(merged.T.astype(BF16), w_ref[...])


def _out_c_t(os_t, ls_t, w_bf, x):
    return pl.pallas_call(
        _out_c_t_kernel,
        out_shape=jax.ShapeDtypeStruct(x.shape, F32),
        compiler_params=pltpu.CompilerParams(vmem_limit_bytes=VMEM_LIMIT),
    )(*os_t, *ls_t, w_bf, x)


def _top_k_mask(g, index, k, axis):
    sel = jnp.zeros(g.shape, F32)
    index_f = index.astype(F32)
    for _ in range(k):
        m = jnp.max(g, axis=axis, keepdims=True)
        first = jnp.min(jnp.where(g == m, index_f, float(g.shape[axis])), axis=axis, keepdims=True)
        hit = index_f == first
        sel = jnp.where(hit, 1.0, sel)
        g = jnp.where(hit, LOWEST, g)
    return sel


def _moba_prep_kernel(q_ref, k_ref, v_ref, qa_ref, ka_ref, va_ref, kmean_scr, *, nb_pad):
    i = pl.program_id(1)

    @pl.when(i == 0)
    def _():
        kmean_scr[...] = jnp.zeros(kmean_scr.shape, F32)

    rows = q_ref.shape[1]
    lane = lax.broadcasted_iota(jnp.int32, (rows, LANES), 1)
    low = lane < DH
    blk = lax.broadcasted_iota(jnp.int32, (nb_pad, rows), 0)
    past = blk < i
    for hp in range(HEADS // 2):
        cols = slice(hp * LANES, (hp + 1) * LANES)
        q2 = q_ref[0, :, cols]
        k2 = k_ref[0, :, cols]
        v2 = v_ref[0, :, cols]
        km2 = kmean_scr[0:nb_pad, cols]
        for e in range(2):
            mine = low if e == 0 else jnp.logical_not(low)
            gate = _dot_t(km2, jnp.where(mine, q2, 0.0), precision=HIGHEST)
            sel = _top_k_mask(jnp.where(past, gate, NEG_INF), blk, MOBA_TOPK, 0)
            keep = jnp.where(past, sel, 0.0) + jnp.where(blk == i, 1.0, 0.0)
            bias = jnp.where(keep > 0.5, 0.0, NEG_INF)
            bias = jnp.concatenate([bias, jnp.zeros((LANES - nb_pad, rows), F32)], axis=0).T
            onehot = jnp.where(lane == i, 1.0, 0.0)
            if e == 0:
                bias = pltpu.roll(bias, DH, axis=1)
                onehot = jnp.where(lane == i + DH, 1.0, 0.0)
                ones_col = jnp.where(lane == DH, 1.0, 0.0)
            else:
                ones_col = jnp.where(lane == 0, 1.0, 0.0)
            h = 2 * hp + e
            qa_ref[0, h] = jnp.where(mine, q2 * SCALE, bias).astype(BF16)
            ka_ref[0, h] = jnp.where(mine, k2, onehot).astype(BF16)
            va_ref[0, h] = jnp.where(mine, v2, ones_col).astype(BF16)

    kmean_scr[pl.ds(i, 1), :] = jnp.mean(k_ref[0], axis=0, keepdims=True)


def _moba_prep(proj):
    b, s, _ = proj.shape
    nb = s // MOBA_BLOCK
    nb_pad = -(-nb // 8) * 8
    assert nb_pad <= DH
    blk = lambda j: pl.BlockSpec((1, MOBA_BLOCK, W_ATT), lambda bi, i: (bi, i, j))
    aug = pl.BlockSpec((1, HEADS, MOBA_BLOCK, LANES), lambda bi, i: (bi, 0, i, 0))
    shape = jax.ShapeDtypeStruct((b, HEADS, s, LANES), BF16)
    return pl.pallas_call(
        functools.partial(_moba_prep_kernel, nb_pad=nb_pad),
        grid=(b, nb),
        in_specs=[blk(0), blk(1), blk(2)],
        out_specs=[aug, aug, aug],
        out_shape=[shape, shape, shape],
        scratch_shapes=[pltpu.VMEM((LANES, W_ATT), F32)],
        compiler_params=_params("arbitrary", "arbitrary"),
    )(proj, proj, proj)


def _moba_attn_kernel(q_ref, k_ref, v_ref, o_ref, m_scr, acc_scr, *, tq):
    qi = pl.program_id(2)
    tk = MOBA_BLOCK
    n_diag = tq // tk
    lane = lax.broadcasted_iota(jnp.int32, (tq, LANES), 1)
    hs = q_ref.shape[1]

    def step(n, causal_offset=None):
        st = pl.multiple_of(n * tk, tk)
        rows = slice(causal_offset or 0, tq)
        for e in range(hs):
            s = _dot_t(q_ref[0, e, rows, :], k_ref[0, e, pl.ds(st, tk), :])
            if causal_offset is not None:
                s = jnp.where(lax.broadcasted_iota(jnp.int32, s.shape, 1)
                              <= lax.broadcasted_iota(jnp.int32, s.shape, 0), s, NEG_INF)
            m = m_scr[e, rows, :]
            m_new = jnp.maximum(m, jnp.max(s, axis=1, keepdims=True))
            m_scr[e, rows, :] = m_new
            p = jnp.exp(s - jnp.concatenate([m_new] * (tk // LANES), axis=1)).astype(BF16)
            acc_scr[e, rows, :] = (jnp.exp(m - m_new) * acc_scr[e, rows, :]
                                   + _dot(p, v_ref[0, e, pl.ds(st, tk), :]))

    m_scr[...] = jnp.full(m_scr.shape, LOWEST, F32)
    acc_scr[...] = jnp.zeros(acc_scr.shape, F32)
    for j in range(n_diag):
        step(qi * n_diag + j, causal_offset=j * tk)

    def body(n, carry):
        for j in range(n_diag):
            step(n * n_diag + j)
        return carry

    lax.fori_loop(0, qi, body, 0)
    for e in range(0, hs, 2):
        even, odd = acc_scr[e], acc_scr[e + 1]
        o_ref[0, :, e // 2 * LANES:(e // 2 + 1) * LANES] = jnp.where(
            lane < DH, even / even[:, DH:DH + 1], odd / odd[:, 0:1])


def _moba_attn(qa, ka, va, tq, hs):
    b, h, s, _ = qa.shape
    return pl.pallas_call(
        functools.partial(_moba_attn_kernel, tq=tq),
        grid=(b, h // hs, s // tq),
        in_specs=[pl.BlockSpec((1, hs, tq, LANES), lambda bi, hp, i: (bi, hp, i, 0)),
                  pl.BlockSpec((1, hs, s, LANES), lambda bi, hp, i: (bi, hp, 0, 0),
                               pipeline_mode=pl.Buffered(1)),
                  pl.BlockSpec((1, hs, s, LANES), lambda bi, hp, i: (bi, hp, 0, 0),
                               pipeline_mode=pl.Buffered(1))],
        out_specs=pl.BlockSpec((1, tq, hs // 2 * LANES), lambda bi, hp, i: (bi, i, hp)),
        out_shape=jax.ShapeDtypeStruct((b, s, W_ATT), F32),
        scratch_shapes=[pltpu.VMEM((hs, tq, LANES), F32), pltpu.VMEM((hs, tq, LANES), F32)],
        compiler_params=_params("parallel", "parallel", "arbitrary"),
    )(qa, ka, va)


def _column(ref, rows, lane_is_seq):
    return jnp.sum(jnp.where(lane_is_seq, ref[rows, :], 0.0), axis=1, keepdims=True)


def _zero_refs(out_refs):
    for ref in out_refs:
        ref[...] = jnp.zeros(ref.shape, F32)


def _moba_dec_body(bi, _, in_refs, out_refs, *, n_pages, page):
    q_ref, kn_ref, vn_ref = in_refs[:3]
    k_refs, v_refs = in_refs[3:3 + n_pages], in_refs[3 + n_pages:]
    o_ref, = out_refs
    pages_per_block = MOBA_BLOCK // page
    n_past = n_pages // pages_per_block
    lane_is_seq = lax.broadcasted_iota(jnp.int32, (W_ATT, LANES), 1) == bi
    rows = slice(0, W_ATT)
    qc = _column(q_ref, rows, lane_is_seq)
    kc = _column(kn_ref, rows, lane_is_seq)
    vc = _column(vn_ref, rows, lane_is_seq)
    head = lambda x, h: x[h * DH:(h + 1) * DH]
    by_head = lambda f: jnp.concatenate([f(h) for h in range(HEADS)], axis=0)
    raw = [by_head(lambda h, j=j: jnp.sum(k_refs[j][0, h] * head(qc, h), axis=0, keepdims=True))
           for j in range(n_pages)]
    gates = []
    for n in range(n_past):
        tot = raw[n * pages_per_block]
        for j in range(1, pages_per_block):
            tot = tot + raw[n * pages_per_block + j]
        gates.append(jnp.sum(tot, axis=1, keepdims=True) * (1.0 / MOBA_BLOCK))
    sel = [jnp.zeros((HEADS, 1), F32)] * n_past
    for _ in range(min(MOBA_TOPK, n_past + 1)):
        best = functools.reduce(jnp.maximum, gates)
        free = jnp.ones((HEADS, 1), F32)
        for n in range(n_past):
            hit = jnp.where(gates[n] == best, free, 0.0)
            free = free - hit
            sel[n] = jnp.maximum(sel[n], hit)
            gates[n] = jnp.where(hit > 0.5, LOWEST, gates[n])
    qk = qc * kc
    s_self = by_head(lambda h: jnp.sum(head(qk, h), axis=0, keepdims=True)) * SCALE
    s = [jnp.where(sel[j // pages_per_block] > 0.5, raw[j] * SCALE, NEG_INF)
         for j in range(n_pages)]
    m = s_self
    for sj in s:
        m = jnp.maximum(m, jnp.max(sj, axis=1, keepdims=True))
    p_self = jnp.exp(s_self - m)
    p = [jnp.exp(sj - m) for sj in s]
    l = p_self
    for pj in p:
        l = l + jnp.sum(pj, axis=1, keepdims=True)

    def attend(h):
        acc = v_refs[0][0, h] * p[0][h:h + 1, :]
        for j in range(1, n_pages):
            acc = acc + v_refs[j][0, h] * p[j][h:h + 1, :]
        return ((jnp.sum(acc, axis=1, keepdims=True) + p_self[h:h + 1, :] * head(vc, h))
                / l[h:h + 1, :])

    o_ref[...] = jnp.where(lane_is_seq, by_head(attend), o_ref[...])


def _moba_dec_guest(proj_t, cache_k, cache_v, page_table):
    b = proj_t.shape[1]
    n_pages = page_table.shape[1]
    page = cache_k.shape[3]
    assert b == LANES
    vec = lambda j: ((W_ATT, b), lambda bi, pt: (j, 0))
    pg = lambda j: ((1, HEADS, DH, page), lambda bi, pt: (pt[bi * n_pages + j], 0, 0, 0))
    pages = [pg(j) for j in range(n_pages)]
    return _Guest(
        in_specs=[vec(0), vec(1), vec(2)] + pages + pages,
        args=[proj_t] * 3 + [cache_k] * n_pages + [cache_v] * n_pages,
        out_specs=[((W_ATT, b), lambda bi, pt: (0, 0))],
        out_shape=[jax.ShapeDtypeStruct((W_ATT, b), F32)],
        init=_zero_refs,
        body=functools.partial(_moba_dec_body, n_pages=n_pages, page=page),
        prefetch=page_table.reshape(-1))


def _hgrn_seq_kernel(q_ref, f_ref, i_ref, g_ref, lb_ref, gn_ref, o_ref, s_ref, st_scr, *, tb):
    t = pl.program_id(0)

    @pl.when(t == 0)
    def _():
        st_scr[...] = jnp.zeros(st_scr.shape, F32)

    c = HGRN_CHUNK
    nch = tb // c
    w = q_ref.shape[2]
    lb = lb_ref[...]
    gn = gn_ref[...]
    r_i = lax.broadcasted_iota(jnp.int32, (tb, tb), 0)
    c_i = lax.broadcasted_iota(jnp.int32, (tb, tb), 1)
    tril = jnp.logical_and(c_i <= r_i, c_i // c == r_i // c)
    ones_tril = jnp.where(tril, 1.0, 0.0)
    per_chunk = lambda x, row: jnp.broadcast_to(
        x.reshape(nch, c, w)[:, row:row + 1, :], (nch, c, w)).reshape(tb, w)
    for bi in range(q_ref.shape[0]):
        f = lb + (1.0 - lb) * _sigmoid(f_ref[bi])
        k = 1.0 - f
        q = _silu(q_ref[bi])
        v = i_ref[bi]
        b = jnp.dot(ones_tril, jnp.log(f), precision=HIGHEST, preferred_element_type=F32)
        b_mid = per_chunk(b, c // 2 - 1)
        b_last = per_chunk(b, c - 1)
        q_mid = (q * jnp.exp(b - b_mid)).astype(BF16)
        k_mid = (k * jnp.exp(b_mid - b)).astype(BF16)
        k_end = (k * jnp.exp(b_last - b)).astype(BF16)
        q_in = (q * jnp.exp(b)).astype(BF16)
        decay = jnp.exp(b_last)
        vb = v.astype(BF16)
        outs = []
        for h in range(H_B):
            cols = slice(h * D_B, (h + 1) * D_B)
            a = jnp.where(tril, _dot_t(q_mid[:, cols], k_mid[:, cols]), 0.0)
            o_intra = _dot(a.astype(BF16), vb[:, cols])
            o_inter = []
            st = st_scr[bi, h]
            for ch in range(nch):
                rows = slice(ch * c, (ch + 1) * c)
                o_inter.append(_dot_t(q_in[rows, cols], st.astype(BF16)))
                st = (st * decay[ch * c:ch * c + 1, cols]
                      + _dot(v[rows, cols].T.astype(BF16), k_end[rows, cols]))
            st_scr[bi, h] = st
            outs.append(_rms(o_intra + jnp.concatenate(o_inter, axis=0), gn[:, cols]))
        o_ref[bi] = jnp.concatenate(outs, axis=1) * _silu(g_ref[bi])

    @pl.when(t == pl.num_programs(0) - 1)
    def _():
        for bi in range(q_ref.shape[0]):
            for h in range(H_B):
                s_ref[bi, h] = st_scr[bi, h].T


def _hgrn_seq(proj, lb, gn, first_chunk, tb):
    b, s, _ = proj.shape
    w = H_B * D_B
    blk = lambda j: pl.BlockSpec((b, tb, w), lambda t: (0, t, first_chunk + j))
    vec = pl.BlockSpec((1, w), lambda t: (0, 0))
    return pl.pallas_call(
        functools.partial(_hgrn_seq_kernel, tb=tb),
        grid=(s // tb,),
        in_specs=[blk(0), blk(1), blk(2), blk(3), vec, vec],
        out_specs=[pl.BlockSpec((b, tb, w), lambda t: (0, t, 0)),
                   pl.BlockSpec((b, H_B, D_B, D_B), lambda t: (0, 0, 0, 0))],
        out_shape=[jax.ShapeDtypeStruct((b, s, w), F32),
                   jax.ShapeDtypeStruct((b, H_B, D_B, D_B), F32)],
        scratch_shapes=[pltpu.VMEM((b, H_B, D_B, D_B), F32)],
        compiler_params=_params("arbitrary"),
    )(proj, proj, proj, proj, lb.reshape(1, w), jnp.tile(gn.astype(F32), H_B).reshape(1, w))


def _hgrn_step_kernel(q_ref, f_ref, i_ref, g_ref, lb_ref, gn_ref, s0_ref, o_ref, s_ref):
    bt = s0_ref.shape[0]
    rows = pl.ds(pl.multiple_of(pl.program_id(0) * bt, bt), bt)
    lb = lb_ref[...]
    gn = gn_ref[...]
    f_all = lb + (1.0 - lb) * _sigmoid(f_ref[rows, :])
    q_all = _silu(q_ref[rows, :])
    v_all = i_ref[rows, :]
    gate_all = _silu(g_ref[rows, :])
    eye = (lax.broadcasted_iota(jnp.int32, (D_B, D_B), 0)
           == lax.broadcasted_iota(jnp.int32, (D_B, D_B), 1))
    for j in range(bt):
        f, q, v = f_all[j:j + 1], q_all[j:j + 1], v_all[j:j + 1]
        outs = []
        for h in range(H_B):
            cols = slice(h * D_B, (h + 1) * D_B)
            f_col = jnp.sum(jnp.where(eye, f[:, cols], 0.0), axis=1, keepdims=True)
            q_col = jnp.sum(jnp.where(eye, q[:, cols], 0.0), axis=1, keepdims=True)
            s_new = f_col * s0_ref[j, h] + (1.0 - f_col) * v[:, cols]
            s_ref[j, h] = s_new
            o = jnp.sum(q_col * s_new, axis=0, keepdims=True)
            outs.append(_rms(o, gn[:, cols]))
        o_ref[j:j + 1, :] = jnp.concatenate(outs, axis=1) * gate_all[j:j + 1]


def _hgrn_step(proj, lb, gn, s0, first_chunk, bt):
    b = proj.shape[0]
    w = H_B * D_B
    blk = lambda j: pl.BlockSpec((b, w), lambda bi: (0, first_chunk + j))
    vec = pl.BlockSpec((1, w), lambda bi: (0, 0))
    st = pl.BlockSpec((bt, H_B, D_B, D_B), lambda bi: (bi, 0, 0, 0))
    return pl.pallas_call(
        _hgrn_step_kernel,
        grid=(b // bt,),
        in_specs=[blk(0), blk(1), blk(2), blk(3), vec, vec, st],
        out_specs=[pl.BlockSpec((bt, w), lambda bi: (bi, 0)), st],
        out_shape=[jax.ShapeDtypeStruct((b, w), F32),
                   jax.ShapeDtypeStruct((b, H_B, D_B, D_B), F32)],
        compiler_params=_params("parallel"),
    )(proj, proj, proj, proj, lb.reshape(1, w),
      jnp.tile(gn.astype(F32), H_B).reshape(1, w), s0)


def _band_kernel(q_ref, kc_ref, kp_ref, vc_ref, vp_ref, o_ref, l_ref, *, tq, dil):
    first = pl.program_id(1) == 0
    r = pl.program_id(2)
    row = lax.broadcasted_iota(jnp.int32, (BAND, 2 * BAND), 0)
    col = lax.broadcasted_iota(jnp.int32, (BAND, 2 * BAND), 1)
    band = jnp.logical_and(col >= row, col <= row + BAND)
    lane = lax.broadcasted_iota(jnp.int32, (BAND, LANES), 1)
    low = lane < DH
    for i in range(tq // BAND):
        rows = slice(i * BAND, (i + 1) * BAND)
        if i == 0:
            k_prev, v_prev = kp_ref[0, 0], vp_ref[0, 0]
            mask = jnp.logical_and(band, jnp.logical_or(col >= BAND, jnp.logical_not(first)))
        else:
            prev = slice((i - 1) * BAND, i * BAND)
            k_prev, v_prev = kc_ref[0, 0, prev, :], vc_ref[0, 0, prev, :]
            mask = band
        kk = jnp.concatenate([k_prev, kc_ref[0, 0, rows, :]], axis=0).astype(BF16)
        vv = jnp.concatenate([v_prev, vc_ref[0, 0, rows, :]], axis=0).astype(BF16)
        out_rows = rows if dil == 1 else pl.ds(r + i * BAND * dil, BAND, stride=dil)
        for hp in range(HEADS // 2):
            cols = slice(hp * LANES, (hp + 1) * LANES)
            q2 = q_ref[0, 0, rows, cols] * SCALE
            res = []
            for e in range(2):
                mine = low if e == 0 else jnp.logical_not(low)
                s = _dot_t(jnp.where(mine, q2, 0.0).astype(BF16), kk[:, cols])
                s = jnp.where(mask, s, NEG_INF)
                m = jnp.max(s, axis=1, keepdims=True)
                p = jnp.exp(s - m)
                l = jnp.sum(p, axis=1, keepdims=True)
                res.append((_dot((p / l).astype(BF16), vv[:, cols]), m + jnp.log(l)))
            o_ref[0, hp, out_rows, :] = jnp.where(low, res[0][0], res[1][0])
            l_ref[0, hp, out_rows, :] = jnp.where(low, res[0][1], res[1][1])


def _band_attention(proj, tq):
    b, dil, ln, _ = proj.shape
    sub = tq // BAND
    cur = lambda comp: pl.BlockSpec((1, 1, tq, W_ATT), lambda bi, l, r: (bi, r, l, comp))
    prev = lambda comp: pl.BlockSpec((1, 1, BAND, W_ATT),
                                     lambda bi, l, r: (bi, r, jnp.maximum(l * sub - 1, 0), comp))
    out = pl.BlockSpec((1, HEADS // 2, tq * dil, LANES), lambda bi, l, r: (bi, 0, l, 0))
    shape = jax.ShapeDtypeStruct((b, HEADS // 2, ln * dil, LANES), F32)
    return pl.pallas_call(
        functools.partial(_band_kernel, tq=tq, dil=dil),
        grid=(b, ln // tq, dil),
        in_specs=[cur(0), cur(1), prev(1), cur(2), prev(2)],
        out_specs=[out, out],
        out_shape=[shape, shape],
        compiler_params=_params("parallel", "arbitrary", "arbitrary"),
    )(proj, proj, proj, proj, proj)


def _dil_cache_body(step, _, in_refs, out_refs, *, dils, hb, head_groups):
    bi, hg = step // head_groups, step % head_groups
    for g, dil in enumerate(dils):
        _dil_cache_group(bi, hg, *in_refs[5 * g:5 * g + 5], *out_refs[4 * g:4 * g + 4],
                         dil=dil, hb=hb)


def _dil_cache_init(out_refs):
    for g in range(len(out_refs) // 4):
        _zero_refs(out_refs[4 * g + 2:4 * g + 4])


def _dil_cache_kernel(*refs, dils, hb):
    n_in = 5 * len(dils)

    @pl.when(jnp.logical_and(pl.program_id(0) == 0, pl.program_id(1) == 0))
    def _():
        _dil_cache_init(refs[n_in:])

    _dil_cache_body(pl.program_id(0) * pl.num_programs(1) + pl.program_id(1), None,
                    refs[:n_in], refs[n_in:], dils=dils, hb=hb, head_groups=HEADS // hb)


def _dil_cache_group(bi, hg, q_ref, kn_ref, vn_ref, ck_ref, cv_ref, ok_ref, ov_ref, o_ref, l_ref,
                     *, dil, hb):
    lw = ck_ref.shape[3]
    nr = hb * DH
    rows = pl.ds(pl.multiple_of(hg * nr, nr), nr)
    lane_is_seq = lax.broadcasted_iota(jnp.int32, (nr, LANES), 1) == bi
    pos = lax.broadcasted_iota(jnp.int32, (hb, lw), 1)
    on_grid = (pos & (dil - 1)) == 0
    newest = lax.broadcasted_iota(jnp.int32, (DH, lw), 1) == lw - 1
    qc = _column(q_ref, rows, lane_is_seq) * SCALE
    kc = _column(kn_ref, rows, lane_is_seq)
    vc = _column(vn_ref, rows, lane_is_seq)
    head = lambda x, hh: x[hh * DH:(hh + 1) * DH]
    s = jnp.concatenate([jnp.sum(ck_ref[0, hh] * head(qc, hh), axis=0, keepdims=True)
                         for hh in range(hb)], axis=0)
    s = jnp.where(on_grid, s, NEG_INF)
    qk = qc * kc
    s_self = jnp.concatenate([jnp.sum(head(qk, hh), axis=0, keepdims=True)
                              for hh in range(hb)], axis=0)
    m = jnp.maximum(jnp.max(s, axis=1, keepdims=True), s_self)
    p = jnp.exp(s - m)
    p_self = jnp.exp(s_self - m)
    l = jnp.sum(p, axis=1, keepdims=True) + p_self
    pn = p / l
    w_self = p_self / l
    lse = m + jnp.log(l)
    os, ls = [], []
    for hh in range(hb):
        k = ck_ref[0, hh]
        v = cv_ref[0, hh]
        os.append(jnp.sum(v * pn[hh:hh + 1, :], axis=1, keepdims=True)
                  + w_self[hh:hh + 1, :] * head(vc, hh))
        ls.append(jnp.broadcast_to(lse[hh:hh + 1, :], (DH, 1)))
        ok_ref[0, hh] = jnp.where(newest, head(kc, hh), pltpu.roll(k, lw - 1, axis=1))
        ov_ref[0, hh] = jnp.where(newest, head(vc, hh), pltpu.roll(v, lw - 1, axis=1))
    o_ref[rows, :] = jnp.where(lane_is_seq, jnp.concatenate(os, axis=0), o_ref[rows, :])
    l_ref[rows, :] = jnp.where(lane_is_seq, jnp.concatenate(ls, axis=0), l_ref[rows, :])


def _dil_cache_guest(proj_t, n_groups, groups, dils, caches, hb):
    b, h, dh, _ = caches[0][0].shape
    hgs = h // hb
    assert b == LANES and all(dil & (dil - 1) == 0 for dil in dils)
    vec = lambda j: ((W_ATT, b), lambda step, *_: (j, 0))
    res = ((W_ATT, b), lambda step, *_: (0, 0))
    res_shape = jax.ShapeDtypeStruct((W_ATT, b), F32)
    in_specs, out_specs, out_shape, args = [], [], [], []
    for g, (ck, cv) in zip(groups, caches):
        cache = ((1, hb, dh, ck.shape[3]), lambda step, *_: (step // hgs, step % hgs, 0, 0))
        in_specs += [vec(g), vec(n_groups + g), vec(2 * n_groups + g), cache, cache]
        args += [proj_t, proj_t, proj_t, ck, cv]
        out_specs += [cache, cache, res, res]
        out_shape += [jax.ShapeDtypeStruct(ck.shape, F32), jax.ShapeDtypeStruct(cv.shape, F32),
                      res_shape, res_shape]
    return _Guest(in_specs, args, out_specs, out_shape, _dil_cache_init,
                  functools.partial(_dil_cache_body, dils=tuple(dils), hb=hb, head_groups=hgs))


def _dil_cache(guest, b, head_groups):
    by_step = lambda f: (lambda bi, hg: f(bi * head_groups + hg))
    return pl.pallas_call(
        functools.partial(_dil_cache_kernel, dils=guest.body.keywords["dils"],
                          hb=guest.body.keywords["hb"]),
        grid=(b, head_groups),
        in_specs=[pl.BlockSpec(shape, by_step(f)) for shape, f in guest.in_specs],
        out_specs=[pl.BlockSpec(shape, by_step(f)) for shape, f in guest.out_specs],
        out_shape=guest.out_shape,
        compiler_params=_params("arbitrary", "arbitrary"),
    )(*guest.args)


def kernel(x_prompt, x_sample, cache_k_a, cache_v_a, state_hgrn, cache_k_c0, cache_v_c0, cache_k_c1, cache_v_c1, cache_k_c2, cache_v_c2, state_conv, page_table, norm_mix, norm_ffn, w_in_ab, q_norm_a, k_norm_a, lb_logits, g_norm_b, w_out_ab, w_in_c, q_norm_c, k_norm_c, w_out_c, w_gate, w_up, conv_w, conv_b, w_down):
    bsz, seq, d = x_prompt.shape
    nb = x_sample.shape[0]
    n = bsz * seq
    dff = w_gate.shape[2]
    tm, tc = 512, dff // 2
    ng = len(C_PATTERNS)
    c_caches = ((cache_k_c0, cache_v_c0), (cache_k_c1, cache_v_c1), (cache_k_c2, cache_v_c2))

    lb_all = jnp.cumsum(jax.nn.softmax(lb_logits.astype(F32), axis=0), axis=0)
    bf = lambda w: w.astype(BF16)

    xp = x_prompt.reshape(n, d)
    xs = x_sample.reshape(nb, d)

    w_in = bf(w_in_ab[0])
    gains = _head_gains([q_norm_a[0], k_norm_a[0]])
    w_out = bf(w_out_ab[0])
    lb = lb_all[0]

    pp, k_t, v_t = _norm_proj(xp, norm_mix[0], w_in, gains, tm, transposed=(1, 2), n_seq=bsz)
    pp = pp.reshape(bsz, seq, -1)
    qa, ka, va = _moba_prep(pp)
    oa = _moba_attn(qa, ka, va, 512, HEADS).reshape(n, W_ATT)
    ob, hg_p = _hgrn_seq(pp, lb, g_norm_b[0], 3, 256)
    xp = _out_ab(oa, ob.reshape(n, -1), w_out, xp, tm)

    pos_minor = lambda c: jnp.transpose(c, (0, 2, 3, 1))
    pos_major = lambda c: jnp.transpose(c, (0, 3, 1, 2))
    k_a_p = pos_major(k_t.reshape(bsz, HEADS, DH, seq))[None]
    v_a_p = pos_major(v_t.reshape(bsz, HEADS, DH, seq))[None]

    conv_p, conv_s = [], []
    tm_ffn, nc_ffn = 256, 2

    def ffn(l, xp, guest):
        wg, wu, wd = bf(w_gate[l]), bf(w_up[l]), bf(w_down[l])
        xp, cp, guest_out = _ffn_seq(xp, norm_ffn[l], wg, wu, conv_w[l], conv_b[l], wd,
                                     bsz, tm_ffn, nc_ffn, guest)

        def sample(xs):
            xs, a_s = _ffn_step(xs, norm_ffn[l], wg, wu, conv_w[l], conv_b[l], wd,
                                state_conv[l, :, 0, :], state_conv[l, :, 1, :], tc)
            conv_s.append(jnp.stack([state_conv[l, :, 1, :], a_s], axis=1))
            return xs

        conv_p.append(cp)
        return xp, sample, guest_out

    ps = _norm_proj(xs, norm_mix[0], w_in, gains, nb)
    assert n // tm_ffn * nc_ffn == nb
    xp, ffn_sample, (oas_t,) = ffn(0, xp, _moba_dec_guest(
        ps.T, pos_minor(cache_k_a[0]), pos_minor(cache_v_a[0]), page_table))
    obs, hg_s = _hgrn_step(ps, lb, g_norm_b[0], state_hgrn[0], 3, 8)
    xs = ffn_sample(_out_ab_t(oas_t, obs, w_out, xs))
    k_a_s = ps[:, W_ATT:2 * W_ATT].reshape(1, nb, 1, HEADS, DH)
    v_a_s = ps[:, 2 * W_ATT:3 * W_ATT].reshape(1, nb, 1, HEADS, DH)

    w_in = bf(w_in_c[0])
    w_out = bf(w_out_c[0])
    gains = _head_gains([q_norm_c[0], k_norm_c[0]])

    os, ls, c_prompt = [], [], []
    for g, (win, dil) in enumerate(C_PATTERNS):
        ln = seq // dil
        w_g = jnp.concatenate([w_in[:, (c * ng + g) * W_ATT:(c * ng + g + 1) * W_ATT]
                               for c in range(3)], axis=1)
        xr = xp.reshape(bsz, ln, dil, d).transpose(0, 2, 1, 3).reshape(n, d)
        pg = _norm_proj(xr, norm_mix[1], w_g, gains, 2 * tm)
        o, lse = _band_attention(pg.reshape(bsz, dil, ln, 3 * W_ATT), min(512, 4096 // dil))
        os.append(o)
        ls.append(lse)
        keep = min(win, seq)
        kd = keep // dil
        tail = lax.optimization_barrier(lax.slice(
            pg.reshape(bsz * dil, ln, 3 * W_ATT), (0, ln - kd, W_ATT), (bsz * dil, ln, 3 * W_ATT)))
        tail = tail.reshape(bsz, dil, kd, 2, HEADS, DH).transpose(3, 0, 2, 1, 4, 5)
        tail = tail.reshape(2, 1, bsz, keep, HEADS, DH)
        c_prompt.append((tail[0], tail[1]))
    xp = _out_c(os, ls, w_out, xp, tm)

    gains = _head_gains([q_norm_c[0]] * ng + [k_norm_c[0]] * ng)
    ps_t = _norm_proj(xs, norm_mix[1], w_in, gains, nb).T
    dils = [dil for _, dil in C_PATTERNS]
    views = [(pos_minor(ck[0]), pos_minor(cv[0])) for ck, cv in c_caches]
    hosted = [g for g in range(ng) if C_PATTERNS[g][0] <= 512]
    alone = [g for g in range(ng) if g not in hosted]
    pick = lambda xs_, gs: [xs_[g] for g in gs]
    xp, ffn_sample, hosted_out = ffn(1, xp, _dil_cache_guest(
        ps_t, ng, hosted, pick(dils, hosted), pick(views, hosted), HEADS))
    alone_out = _dil_cache(_dil_cache_guest(ps_t, ng, alone, pick(dils, alone), pick(views, alone),
                                            HEADS), nb, 1)
    decoded = dict(zip(hosted + alone,
                       [outs[4 * j:4 * j + 4] for outs in (hosted_out, alone_out)
                        for j in range(len(outs) // 4)]))
    c_out, os_t, ls_t = [], [], []
    for g in range(ng):
        nk, nv, o_t, l_t = decoded[g]
        c_out += list(c_prompt[g])
        c_out += [pos_major(nk)[None], pos_major(nv)[None]]
        os_t.append(o_t)
        ls_t.append(l_t)
    xs = ffn_sample(_out_c_t(os_t, ls_t, w_out, xs))

    return (xp.reshape(bsz, seq, d), xs.reshape(nb, 1, d),
            k_a_p, v_a_p, k_a_s, v_a_s, hg_p[None], hg_s[None],
            *c_out, jnp.stack(conv_p), jnp.stack(conv_s))
```
